```python
import jax, jax.numpy as jnp
from jax import lax
import numpy as np

D_MODEL = 1024
BATCH = 16
SEQ = 256
DEPTH = 4
DEC_BATCH = 2
DEC_SEQ = 4096
PAST_LEN = 256

GRID_W = 64
N_MIXERS = 3
N_CONV_LAYERS = (DEPTH + 2) // 3
N_POOL_LAYERS = (DEPTH + 1) // 3
N_ATTN_LAYERS = DEPTH // 3
N_HEADS = 8
N_KV_HEADS = 2
HEAD_DIM = 128
ROPE_AXIS_DIM = HEAD_DIM // 2
ROPE_THETA = 10000.0
Q_BLOCK = 128
CONV_WIDTH = 3
POOL_WINDOWS = (2, 4, 8, 16)
N_POOL_GROUPS = 4
POOL_GROUP_DIM = D_MODEL // N_POOL_GROUPS
D_FF = ((8 * D_MODEL // 3 + 255) // 256) * 256
N_MOD = 6
EPS = 1e-6
ATTN_SCALE = HEAD_DIM ** -0.5

kernel_name = 'hybrid_flow_prefix_trunk_step'


def rms_norm(x, w):
    xf = x.astype(jnp.float32)
    y = xf * lax.rsqrt(jnp.mean(xf * xf, axis=-1, keepdims=True) + EPS)
    return (y * w.astype(jnp.float32)).astype(x.dtype)


def adaln(c, w, b):
    m = (jax.nn.silu(c) @ w + b).reshape(c.shape[0], N_MOD, 1, D_MODEL)
    return [m[:, i] for i in range(N_MOD)]


def modulate(x, shift, scale):
    return x * (1 + scale) + shift


def short_conv_mixer(h, w_in, conv_w, w_out):
    b, cg, u = jnp.split(h @ w_in, 3, axis=-1)
    z = cg * u
    zp = jnp.pad(z, ((0, 0), (1, 1), (0, 0)))
    conv = zp[:, :-2] * conv_w[0] + zp[:, 1:-1] * conv_w[1] + zp[:, 2:] * conv_w[2]
    return (b * conv) @ w_out


def pool_mixer(h, w_pool, scale):
    B, T, D = h.shape
    hf = h.astype(jnp.float32)
    cs = jnp.concatenate([jnp.zeros((B, 1, D), jnp.float32), jnp.cumsum(hf, axis=1)], axis=1)
    csg = cs.reshape(B, T + 1, N_POOL_GROUPS, POOL_GROUP_DIM)
    t = jnp.arange(T)
    pooled = []
    for g, w in enumerate(POOL_WINDOWS):
        lo = jnp.clip(t - w // 2, 0, T)
        hi = jnp.clip(t + w // 2, 0, T)
        cnt = (hi - lo).astype(jnp.float32)[None, :, None]
        pooled.append((csg[:, hi, g] - csg[:, lo, g]) / cnt)
    pooled = jnp.stack(pooled, axis=2)
    diff = (pooled - hf.reshape(B, T, N_POOL_GROUPS, POOL_GROUP_DIM)).astype(h.dtype)
    out = jnp.einsum('btgc,gcd->btgd', diff, w_pool).reshape(B, T, D)
    return out * scale


def swiglu(h, w_in, w_out):
    g, u = jnp.split(h @ w_in, 2, axis=-1)
    return (jax.nn.silu(g) * u) @ w_out


def qkv_project(h, w_qkv, q_norm_w, k_norm_w):
    B, T, _ = h.shape
    qkv = h @ w_qkv
    q = qkv[..., :N_HEADS * HEAD_DIM].reshape(B, T, N_HEADS, HEAD_DIM)
    k = qkv[..., N_HEADS * HEAD_DIM:(N_HEADS + N_KV_HEADS) * HEAD_DIM].reshape(B, T, N_KV_HEADS, HEAD_DIM)
    v = qkv[..., (N_HEADS + N_KV_HEADS) * HEAD_DIM:].reshape(B, T, N_KV_HEADS, HEAD_DIM)
    q = rms_norm(q, q_norm_w)
    k = rms_norm(k, k_norm_w)
    return (jnp.transpose(q, (0, 2, 1, 3)), jnp.transpose(k, (0, 2, 1, 3)),
            jnp.transpose(v, (0, 2, 1, 3)))


def rope_axis(x, cos, sin):
    x1, x2 = jnp.split(x, 2, axis=-1)
    return jnp.concatenate([x1 * cos - x2 * sin, x2 * cos + x1 * sin], axis=-1)


def apply_rope_2d(x, cos_r, sin_r, cos_c, sin_c):
    xr = rope_axis(x[..., :ROPE_AXIS_DIM], cos_r.astype(x.dtype), sin_r.astype(x.dtype))
    xc = rope_axis(x[..., ROPE_AXIS_DIM:], cos_c.astype(x.dtype), sin_c.astype(x.dtype))
    return jnp.concatenate([xr, xc], axis=-1)


def block_attention(q, k, v):
    B, Hq, T, Dh = q.shape
    Hkv = k.shape[1]
    G = Hq // Hkv
    nb = T // Q_BLOCK
    qb = jnp.moveaxis(q.reshape(B, Hkv, G, nb, Q_BLOCK, Dh), 3, 0)

    def one_block(qblk):
        s = jnp.einsum('bhgqd,bhkd->bhgqk', qblk, k,
                       preferred_element_type=jnp.float32) * ATTN_SCALE
        p = jax.nn.softmax(s, axis=-1).astype(v.dtype)
        return jnp.einsum('bhgqk,bhkd->bhgqd', p, v)

    ob = lax.map(one_block, qb)
    return jnp.moveaxis(ob, 0, 3).reshape(B, Hq, T, Dh)


def merge_heads(o, w_out):
    B, H, T, Dh = o.shape
    return jnp.transpose(o, (0, 2, 1, 3)).reshape(B, T, H * Dh) @ w_out


def setup_inputs(seed: int = 0) -> dict:
    key = jax.random.key(seed)
    ks = jax.random.split(key, 24)
    nrm = jax.random.normal
    f32 = jnp.float32
    D = D_MODEL
    qkv_out = (N_HEADS + 2 * N_KV_HEADS) * HEAD_DIM
    return {
        'x_prompt': nrm(ks[0], (BATCH, SEQ, D), f32),
        'x_sample': nrm(ks[1], (DEC_BATCH, DEC_SEQ, D), f32),
        'cache_k': nrm(ks[2], (DEC_BATCH, N_ATTN_LAYERS, N_KV_HEADS, PAST_LEN, HEAD_DIM), f32),
        'cache_v': nrm(ks[3], (DEC_BATCH, N_ATTN_LAYERS, N_KV_HEADS, PAST_LEN, HEAD_DIM), f32),
        'c': nrm(ks[4], (DEC_BATCH, D), f32),
        'c_ctx': nrm(ks[5], (D,), f32),
        'norm1_w': 1.0 + 0.05 * nrm(ks[6], (DEPTH, D), f32),
        'norm2_w': 1.0 + 0.05 * nrm(ks[7], (DEPTH, D), f32),
        'ada_w': 0.5 * D ** -0.5 * nrm(ks[8], (DEPTH, D, N_MOD * D), f32),
        'ada_b': 0.01 * nrm(ks[9], (DEPTH, N_MOD * D), f32),
        'conv_in_w': D ** -0.5 * nrm(ks[10], (N_CONV_LAYERS, D, 3 * D), f32),
        'conv_w': CONV_WIDTH ** -0.5 * nrm(ks[11], (N_CONV_LAYERS, CONV_WIDTH, D), f32),
        'conv_out_w': D ** -0.5 * nrm(ks[12], (N_CONV_LAYERS, D, D), f32),
        'pool_w': POOL_GROUP_DIM ** -0.5 * nrm(ks[13], (N_POOL_LAYERS, N_POOL_GROUPS, POOL_GROUP_DIM, POOL_GROUP_DIM), f32),
        'pool_scale': 1.0 + 0.1 * nrm(ks[14], (N_POOL_LAYERS, D), f32),
        'attn_qkv_w': D ** -0.5 * nrm(ks[15], (N_ATTN_LAYERS, D, qkv_out), f32),
        'q_norm_w': 1.0 + 0.05 * nrm(ks[16], (N_ATTN_LAYERS, HEAD_DIM), f32),
        'k_norm_w': 1.0 + 0.05 * nrm(ks[17], (N_ATTN_LAYERS, HEAD_DIM), f32),
        'attn_out_w': (N_HEADS * HEAD_DIM) ** -0.5 * nrm(ks[18], (N_ATTN_LAYERS, N_HEADS * HEAD_DIM, D), f32),
        'ffn_in_w': D ** -0.5 * nrm(ks[19], (DEPTH, D, 2 * D_FF), f32),
        'ffn_out_w': D_FF ** -0.5 * nrm(ks[20], (DEPTH, D_FF, D), f32),
        'final_norm_w': 1.0 + 0.05 * nrm(ks[21], (D,), f32),
    }


def reference(x_prompt, x_sample, cache_k, cache_v, c, c_ctx, norm1_w, norm2_w, ada_w, ada_b,
              conv_in_w, conv_w, conv_out_w, pool_w, pool_scale, attn_qkv_w, q_norm_w, k_norm_w,
              attn_out_w, ffn_in_w, ffn_out_w, final_norm_w):
    T = x_sample.shape[1]
    rows = T // GRID_W
    row = jnp.repeat(jnp.arange(rows), GRID_W).astype(jnp.float32)
    col = jnp.tile(jnp.arange(GRID_W), rows).astype(jnp.float32)
    inv_freq = ROPE_THETA ** (-jnp.arange(0, ROPE_AXIS_DIM, 2, dtype=jnp.float32) / ROPE_AXIS_DIM)
    ang_r = row[:, None] * inv_freq[None, :]
    ang_c = col[:, None] * inv_freq[None, :]
    cos_r, sin_r, cos_c, sin_c = jnp.cos(ang_r), jnp.sin(ang_r), jnp.cos(ang_c), jnp.sin(ang_c)

    xp, xs = x_prompt, x_sample
    new_k, new_v = [], []
    for i in range(DEPTH):
        kind = i % N_MIXERS
        j = i // N_MIXERS
        mod_p = adaln(c_ctx[None, :], ada_w[i], ada_b[i])
        mod_s = adaln(c, ada_w[i], ada_b[i])
        hp = modulate(rms_norm(xp, norm1_w[i]), mod_p[0], mod_p[1])
        hs = modulate(rms_norm(xs, norm1_w[i]), mod_s[0], mod_s[1])
        if kind == 0:
            mp = short_conv_mixer(hp, conv_in_w[j], conv_w[j], conv_out_w[j])
            ms = short_conv_mixer(hs, conv_in_w[j], conv_w[j], conv_out_w[j])
        elif kind == 1:
            mp = pool_mixer(hp, pool_w[j], pool_scale[j])
            ms = pool_mixer(hs, pool_w[j], pool_scale[j])
        else:
            qp, kp, vp = qkv_project(hp, attn_qkv_w[j], q_norm_w[j], k_norm_w[j])
            new_k.append(kp)
            new_v.append(vp)
            mp = merge_heads(block_attention(qp, kp, vp), attn_out_w[j])
            qs, ks_, vs = qkv_project(hs, attn_qkv_w[j], q_norm_w[j], k_norm_w[j])
            qs = apply_rope_2d(qs, cos_r, sin_r, cos_c, sin_c)
            ks_ = apply_rope_2d(ks_, cos_r, sin_r, cos_c, sin_c)
            k_all = jnp.concatenate([cache_k[:, j].astype(ks_.dtype), ks_], axis=2)
            v_all = jnp.concatenate([cache_v[:, j].astype(vs.dtype), vs], axis=2)
            ms = merge_heads(block_attention(qs, k_all, v_all), attn_out_w[j])
        xp = xp + mod_p[2] * mp
        xs = xs + mod_s[2] * ms
        hp = modulate(rms_norm(xp, norm2_w[i]), mod_p[3], mod_p[4])
        hs = modulate(rms_norm(xs, norm2_w[i]), mod_s[3], mod_s[4])
        xp = xp + mod_p[5] * swiglu(hp, ffn_in_w[i], ffn_out_w[i])
        xs = xs + mod_s[5] * swiglu(hs, ffn_in_w[i], ffn_out_w[i])

    y_prompt = rms_norm(xp, final_norm_w)
    y_sample = rms_norm(xs, final_norm_w)
    new_k_arr = jnp.stack(new_k, axis=1)
    new_v_arr = jnp.stack(new_v, axis=1)
    return (y_prompt, y_sample, new_k_arr, new_v_arr)
```

```python
import functools

import numpy as np
import jax
import jax.numpy as jnp
from jax import lax
from jax.experimental import pallas as pl
from jax.experimental.pallas import tpu as pltpu

D = 1024
BATCH = 16
SEQ = 256
DEPTH = 4
DEC_BATCH = 2
DEC_SEQ = 4096
PAST_LEN = 256
GRID_W = 64
N_HEADS = 8
N_KV_HEADS = 2
HEAD_DIM = 128
Q_PER_KV = N_HEADS // N_KV_HEADS
ROPE_AXIS_DIM = HEAD_DIM // 2
ROPE_THETA = 10000.0
POOL_WINDOWS = (2, 4, 8, 16)
POOL_GROUP_DIM = D // 4
D_FF = 2816
N_MOD = 6
EPS = 1e-6
ATTN_SCALE = HEAD_DIM ** -0.5

NP = BATCH * SEQ
NS = DEC_BATCH * DEC_SEQ
NT = NP + NS
N_GROUPS = 1 + DEC_BATCH

HALO = 16
FF_CHUNK = 256
BF16 = jnp.bfloat16
F32 = jnp.float32


def _dot(a, b):
    return jnp.dot(a, b, preferred_element_type=F32)


def _dot_nt(a, b):
    return lax.dot_general(a, b, (((1,), (1,)), ((), ())), preferred_element_type=F32)


def _norm_mod(x, nw, shift, scale):
    y = x * lax.rsqrt(jnp.mean(x * x, axis=-1, keepdims=True) + EPS)
    return (y * nw) * (1.0 + scale) + shift


def _mod_group(tile, tm):
    return jnp.where(tile < NP // tm, 0, 1 + (tile - NP // tm) // (DEC_SEQ // tm))


def _params(vmem_mb, n_axes=1):
    return pltpu.CompilerParams(
        dimension_semantics=("arbitrary",) * n_axes,
        vmem_limit_bytes=vmem_mb * 1024 * 1024)


def _const_spec(shape):
    return pl.BlockSpec(shape, lambda *_: (0,) * len(shape), pipeline_mode=pl.Buffered(1))


def _adaln_kernel(c_ref, w_ref, b_ref, o_ref):
    c = c_ref[...]
    sc = (c * jax.nn.sigmoid(c)).astype(BF16)
    o_ref[...] = _dot(sc, w_ref[...].astype(BF16)) + b_ref[...]


def _adaln(cvec, ada_w, ada_b):
    tn = 1536
    return pl.pallas_call(
        _adaln_kernel,
        out_shape=jax.ShapeDtypeStruct((DEPTH, 8, N_MOD * D), F32),
        grid=(DEPTH, N_MOD * D // tn),
        in_specs=[
            pl.BlockSpec((8, D), lambda l, j: (0, 0)),
            pl.BlockSpec((None, D, tn), lambda l, j: (l, 0, j)),
            pl.BlockSpec((None, 1, tn), lambda l, j: (l, 0, j)),
        ],
        out_specs=pl.BlockSpec((None, 8, tn), lambda l, j: (l, 0, j)),
        compiler_params=_params(40, 2),
        name="adaln",
    )(cvec, ada_w, ada_b.reshape(DEPTH, 1, N_MOD * D))


def _conv_kernel(x_ref, xp_ref, xn_ref, mod_ref, nw_ref, win_ref, cw_ref, wout_ref,
                 o_ref, h_ref, acc_ref, *, tm):
    i = pl.program_id(0)
    mod = mod_ref[...]
    nw = nw_ref[...]
    shift, scale, gate = mod[0:1], mod[1:2], mod[2:3]
    x = x_ref[...]
    h_ref[0:HALO] = _norm_mod(xp_ref[...], nw, shift, scale).astype(BF16)
    h_ref[HALO:HALO + tm] = _norm_mod(x, nw, shift, scale).astype(BF16)
    h_ref[HALO + tm:] = _norm_mod(xn_ref[...], nw, shift, scale).astype(BF16)

    seq_len = jnp.where(i < NP // tm, SEQ, DEC_SEQ)
    pos = (i * tm + lax.broadcasted_iota(jnp.int32, (tm, 1), 0)) & (seq_len - 1)
    has_prev = pos != 0
    has_next = pos != seq_len - 1
    cw = cw_ref[...]
    rows = tm + 2 * HALO
    for j in range(D // 256):
        c0 = j * 256
        b = _dot(h_ref[HALO:HALO + tm], win_ref[:, c0:c0 + 256])
        cg = _dot(h_ref[...], win_ref[:, D + c0:D + c0 + 256])
        u = _dot(h_ref[...], win_ref[:, 2 * D + c0:2 * D + c0 + 256])
        z = cg * u
        z_prev = pltpu.roll(z, 1, axis=0)[HALO:HALO + tm]
        z_next = pltpu.roll(z, rows - 1, axis=0)[HALO:HALO + tm]
        conv = (jnp.where(has_prev, z_prev, 0.0) * cw[0:1, c0:c0 + 256]
                + z[HALO:HALO + tm] * cw[1:2, c0:c0 + 256]
                + jnp.where(has_next, z_next, 0.0) * cw[2:3, c0:c0 + 256])
        part = _dot((b * conv).astype(BF16), wout_ref[c0:c0 + 256, :])
        if j == 0:
            acc_ref[...] = part
        else:
            acc_ref[...] += part
    o_ref[...] = x + gate * acc_ref[...]


def _conv_layer(x, mods_l, nw, w_in, cw, w_out):
    tm = 512
    nh = tm // HALO
    last = NT // HALO - 1
    return pl.pallas_call(
        functools.partial(_conv_kernel, tm=tm),
        out_shape=jax.ShapeDtypeStruct((NT, D), F32),
        grid=(NT // tm,),
        in_specs=[
            pl.BlockSpec((tm, D), lambda i: (i, 0)),
            pl.BlockSpec((HALO, D), lambda i: (jnp.maximum(i * nh - 1, 0), 0)),
            pl.BlockSpec((HALO, D), lambda i: (jnp.minimum((i + 1) * nh, last), 0)),
            pl.BlockSpec((None, N_MOD, D), lambda i: (_mod_group(i, tm), 0, 0)),
            _const_spec((1, D)),
            _const_spec((D, 3 * D)),
            _const_spec((3, D)),
            _const_spec((D, D)),
        ],
        out_specs=pl.BlockSpec((tm, D), lambda i: (i, 0)),
        scratch_shapes=[pltpu.VMEM((tm + 2 * HALO, D), BF16), pltpu.VMEM((tm, D), F32)],
        compiler_params=_params(48),
        name="conv_mixer",
    )(x, x, x, mods_l, nw, w_in, cw, w_out)


def _pool_kernel(x_ref, xp_ref, xn_ref, mod_ref, nw_ref, wp_ref, ps_ref, o_ref, h_ref, *, tm):
    i = pl.program_id(0)
    mod = mod_ref[...]
    nw = nw_ref[...]
    shift, scale, gate = mod[0:1], mod[1:2], mod[2:3]
    x = x_ref[...]
    seq_len = jnp.where(i < NP // tm, SEQ, DEC_SEQ)
    pos0 = (i * tm) & (seq_len - 1)
    at_start = pos0 == 0
    at_end = pos0 + tm == seq_len
    h_ref[0:HALO] = jnp.where(at_start, 0.0, _norm_mod(xp_ref[...], nw, shift, scale))
    h_ref[HALO:HALO + tm] = _norm_mod(x, nw, shift, scale)
    h_ref[HALO + tm:] = jnp.where(at_end, 0.0, _norm_mod(xn_ref[...], nw, shift, scale))

    rows = tm + 2 * HALO
    pos = pos0 + lax.broadcasted_iota(jnp.int32, (tm, 1), 0)
    ps = ps_ref[...]
    for g, w in enumerate(POOL_WINDOWS):
        c0 = g * POOL_GROUP_DIM
        hg = h_ref[:, c0:c0 + POOL_GROUP_DIM]
        p = hg + pltpu.roll(hg, 1, axis=0)
        step = 1
        while 2 * step < w:
            p = pltpu.roll(p, step, axis=0) + pltpu.roll(p, rows - step, axis=0)
            step *= 2
        cnt = (jnp.minimum(pos + w // 2, seq_len) - jnp.maximum(pos - w // 2, 0)).astype(F32)
        pooled = p[HALO:HALO + tm] / cnt
        diff = (pooled - hg[HALO:HALO + tm]).astype(BF16)
        mixed = _dot(diff, wp_ref[g]) * ps[:, c0:c0 + POOL_GROUP_DIM]
        o_ref[:, c0:c0 + POOL_GROUP_DIM] = (x[:, c0:c0 + POOL_GROUP_DIM]
                                            + gate[:, c0:c0 + POOL_GROUP_DIM] * mixed)


def _pool_layer(x, mods_l, nw, wp, ps):
    tm = 256
    nh = tm // HALO
    last = NT // HALO - 1
    return pl.pallas_call(
        functools.partial(_pool_kernel, tm=tm),
        out_shape=jax.ShapeDtypeStruct((NT, D), F32),
        grid=(NT // tm,),
        in_specs=[
            pl.BlockSpec((tm, D), lambda i: (i, 0)),
            pl.BlockSpec((HALO, D), lambda i: (jnp.maximum(i * nh - 1, 0), 0)),
            pl.BlockSpec((HALO, D), lambda i: (jnp.minimum((i + 1) * nh, last), 0)),
            pl.BlockSpec((None, N_MOD, D), lambda i: (_mod_group(i, tm), 0, 0)),
            _const_spec((1, D)),
            _const_spec((4, POOL_GROUP_DIM, POOL_GROUP_DIM)),
            _const_spec((1, D)),
        ],
        out_specs=pl.BlockSpec((tm, D), lambda i: (i, 0)),
        scratch_shapes=[pltpu.VMEM((tm + 2 * HALO, D), F32)],
        compiler_params=_params(32),
        name="pool_mixer",
    )(x, x, x, mods_l, nw, wp, ps)


def _head_norm(t, w):
    return t * lax.rsqrt(jnp.mean(t * t, axis=-1, keepdims=True) + EPS) * w


def _rope(t, cos, sin_signed, first_half):
    rot = jnp.where(first_half, pltpu.roll(t, HEAD_DIM - 32, axis=1), pltpu.roll(t, 32, axis=1))
    return t * cos + rot * sin_signed


def _qkv_kernel(*refs, tm, rope):
    if rope:
        (x_ref, mod_ref, nw_ref, w_ref, qn_ref, kn_ref, cos_ref, sin_ref,
         q_ref, k_ref, v_ref) = refs
    else:
        (x_ref, mod_ref, nw_ref, w_ref, qn_ref, kn_ref,
         q_ref, k_ref, v_ref, nk_ref, nv_ref) = refs
    mod = mod_ref[...]
    h = _norm_mod(x_ref[...], nw_ref[...], mod[0:1], mod[1:2]).astype(BF16)
    qkv = _dot(h, w_ref[...])
    qn = qn_ref[...]
    kn = kn_ref[...]
    if rope:
        cos = cos_ref[...]
        sin = sin_ref[...]
        first_half = (lax.broadcasted_iota(jnp.int32, (1, HEAD_DIM), 1) & 63) < 32
    for hd in range(N_HEADS):
        t = _head_norm(qkv[:, hd * HEAD_DIM:(hd + 1) * HEAD_DIM], qn)
        if rope:
            t = _rope(t, cos, sin, first_half)
        q_ref[:, hd * HEAD_DIM:(hd + 1) * HEAD_DIM] = t.astype(BF16)
    for g in range(N_KV_HEADS):
        ko = (N_HEADS + g) * HEAD_DIM
        vo = (N_HEADS + N_KV_HEADS + g) * HEAD_DIM
        t = _head_norm(qkv[:, ko:ko + HEAD_DIM], kn)
        vv = qkv[:, vo:vo + HEAD_DIM]
        if rope:
            t = _rope(t, cos, sin, first_half)
        else:
            for bb in range(tm // SEQ):
                nk_ref[bb, 0, g] = t[bb * SEQ:(bb + 1) * SEQ]
                nv_ref[bb, 0, g] = vv[bb * SEQ:(bb + 1) * SEQ]
        k_ref[:, g * HEAD_DIM:(g + 1) * HEAD_DIM] = t.astype(BF16)
        v_ref[:, g * HEAD_DIM:(g + 1) * HEAD_DIM] = vv.astype(BF16)


def _qkv_layer(x, mods_l, nw, w_qkv, qn, kn, rope_tabs):
    tm = 512
    kvd = N_KV_HEADS * HEAD_DIM
    qkv_out = (N_HEADS + 2 * N_KV_HEADS) * HEAD_DIM

    def call(rope, row0, nrows):
        t0 = row0 // tm
        in_specs = [
            pl.BlockSpec((tm, D), lambda i: (t0 + i, 0)),
            pl.BlockSpec((None, N_MOD, D), lambda i: (_mod_group(t0 + i, tm), 0, 0)),
            _const_spec((1, D)),
            _const_spec((D, qkv_out)),
            _const_spec((1, HEAD_DIM)),
            _const_spec((1, HEAD_DIM)),
        ]
        out_shape = [jax.ShapeDtypeStruct((nrows, D), BF16),
                     jax.ShapeDtypeStruct((nrows, kvd), BF16),
                     jax.ShapeDtypeStruct((nrows, kvd), BF16)]
        out_specs = [pl.BlockSpec((tm, D), lambda i: (i, 0)),
                     pl.BlockSpec((tm, kvd), lambda i: (i, 0)),
                     pl.BlockSpec((tm, kvd), lambda i: (i, 0))]
        args = [x, mods_l, nw, w_qkv, qn, kn]
        if rope:
            tps = DEC_SEQ // tm
            in_specs += [pl.BlockSpec((tm, HEAD_DIM), lambda i: (i % tps, 0)),
                         pl.BlockSpec((tm, HEAD_DIM), lambda i: (i % tps, 0))]
            args += list(rope_tabs)
        else:
            nb = tm // SEQ
            kv_shape = (BATCH, 1, N_KV_HEADS, SEQ, HEAD_DIM)
            out_shape += [jax.ShapeDtypeStruct(kv_shape, F32)] * 2
            out_specs += [pl.BlockSpec((nb, 1, N_KV_HEADS, SEQ, HEAD_DIM),
                                       lambda i: (i, 0, 0, 0, 0))] * 2
        return pl.pallas_call(
            functools.partial(_qkv_kernel, tm=tm, rope=rope),
            out_shape=out_shape,
            grid=(nrows // tm,),
            in_specs=in_specs,
            out_specs=out_specs,
            compiler_params=_params(40),
            name="qkv_rope" if rope else "qkv_ctx",
        )(*args)

    return call(False, 0, NP), call(True, NP, NS)


def _attn_kernel(*refs, cached):
    if cached:
        (x_ref, q_ref, k_ref, v_ref, ck_ref, cv_ref, mod_ref, wo_ref, o_ref, heads_ref) = refs
    else:
        (x_ref, q_ref, k_ref, v_ref, mod_ref, wo_ref, o_ref, heads_ref) = refs
    c = ATTN_SCALE * float(np.log2(np.e))
    for g in range(N_KV_HEADS):
        kg = k_ref[:, g * HEAD_DIM:(g + 1) * HEAD_DIM]
        vg = v_ref[:, g * HEAD_DIM:(g + 1) * HEAD_DIM]
        if cached:
            ckg = ck_ref[0, 0, g].astype(BF16)
            cvg = cv_ref[0, 0, g].astype(BF16)
        for hh in range(Q_PER_KV):
            hd = g * Q_PER_KV + hh
            qh = q_ref[:, hd * HEAD_DIM:(hd + 1) * HEAD_DIM]
            s = _dot_nt(qh, kg)
            m = jnp.max(s, axis=-1, keepdims=True)
            if cached:
                sc = _dot_nt(qh, ckg)
                m = jnp.maximum(m, jnp.max(sc, axis=-1, keepdims=True))
            p = jnp.exp2((s - m) * c)
            l = jnp.sum(p, axis=-1, keepdims=True)
            o = _dot(p.astype(BF16), vg)
            if cached:
                pc = jnp.exp2((sc - m) * c)
                l = l + jnp.sum(pc, axis=-1, keepdims=True)
                o = o + _dot(pc.astype(BF16), cvg)
            heads_ref[:, hd * HEAD_DIM:(hd + 1) * HEAD_DIM] = (o / l).astype(BF16)
    gate = mod_ref[...][2:3]
    o_ref[...] = x_ref[...] + gate * _dot(heads_ref[...], wo_ref[...])


def _attn_layer(x, qkv_p, qkv_s, cache_k, cache_v, mods_l, w_out):
    tq = 256
    kvd = N_KV_HEADS * HEAD_DIM

    def call(x, q, k, v, cached, row0, nb, seq):
        t0 = row0 // tq
        nq = seq // tq
        in_specs = [
            pl.BlockSpec((tq, D), lambda b, i: (t0 + b * nq + i, 0)),
            pl.BlockSpec((tq, D), lambda b, i: (b * nq + i, 0)),
            pl.BlockSpec((seq, kvd), lambda b, i: (b, 0)),
            pl.BlockSpec((seq, kvd), lambda b, i: (b, 0)),
        ]
        args = [x, q, k, v]
        if cached:
            cspec = pl.BlockSpec((1, 1, N_KV_HEADS, PAST_LEN, HEAD_DIM),
                                 lambda b, i: (b, 0, 0, 0, 0))
            in_specs += [cspec, cspec]
            args += [cache_k, cache_v]
        in_specs += [
            pl.BlockSpec((None, N_MOD, D), lambda b, i: (_mod_group(t0 + b * nq + i, tq), 0, 0)),
            _const_spec((D, D)),
        ]
        args += [mods_l, w_out]
        return pl.pallas_call(
            functools.partial(_attn_kernel, cached=cached),
            out_shape=jax.ShapeDtypeStruct((NT, D), F32),
            grid=(nb, nq),
            in_specs=in_specs,
            out_specs=pl.BlockSpec((tq, D), lambda b, i: (t0 + b * nq + i, 0)),
            scratch_shapes=[pltpu.VMEM((tq, D), BF16)],
            input_output_aliases={0: 0},
            compiler_params=_params(48, 2),
            name="attn_cached" if cached else "attn_ctx",
        )(*args)

    x = call(x, *qkv_p, False, 0, BATCH, SEQ)
    return call(x, *qkv_s, True, NP, DEC_BATCH, DEC_SEQ)


def _ffn_kernel(*refs, final):
    if final:
        x_ref, mod_ref, nw_ref, win_ref, wout_ref, fw_ref, o_ref, h_ref, acc_ref = refs
    else:
        x_ref, mod_ref, nw_ref, win_ref, wout_ref, o_ref, h_ref, acc_ref = refs
    mod = mod_ref[...]
    x = x_ref[...]
    h_ref[...] = _norm_mod(x, nw_ref[...], mod[3:4], mod[4:5]).astype(BF16)
    for j in range(D_FF // FF_CHUNK):
        c0 = j * FF_CHUNK
        g = _dot(h_ref[...], win_ref[:, c0:c0 + FF_CHUNK])
        u = _dot(h_ref[...], win_ref[:, D_FF + c0:D_FF + c0 + FF_CHUNK])
        a = (g * jax.nn.sigmoid(g) * u).astype(BF16)
        part = _dot(a, wout_ref[c0:c0 + FF_CHUNK, :])
        if j == 0:
            acc_ref[...] = part
        else:
            acc_ref[...] += part
    y = x + mod[5:6] * acc_ref[...]
    if final:
        y = y * lax.rsqrt(jnp.mean(y * y, axis=-1, keepdims=True) + EPS) * fw_ref[...]
    o_ref[...] = y


def _ffn_layer(x, mods_l, nw, w_in, w_out, final_w=None):
    tm = 512
    final = final_w is not None
    in_specs = [
        pl.BlockSpec((tm, D), lambda i: (i, 0)),
        pl.BlockSpec((None, N_MOD, D), lambda i: (_mod_group(i, tm), 0, 0)),
        _const_spec((1, D)),
        _const_spec((D, 2 * D_FF)),
        _const_spec((D_FF, D)),
    ]
    args = [x, mods_l, nw, w_in, w_out]
    if final:
        in_specs.append(_const_spec((1, D)))
        args.append(final_w)
    return pl.pallas_call(
        functools.partial(_ffn_kernel, final=final),
        out_shape=jax.ShapeDtypeStruct((NT, D), F32),
        grid=(NT // tm,),
        in_specs=in_specs,
        out_specs=pl.BlockSpec((tm, D), lambda i: (i, 0)),
        scratch_shapes=[pltpu.VMEM((tm, D), BF16), pltpu.VMEM((tm, D), F32)],
        compiler_params=_params(48),
        name="ffn_final" if final else "ffn",
    )(*args)


def _rope_tables():
    t = np.arange(DEC_SEQ)
    inv_freq = ROPE_THETA ** (-np.arange(0, ROPE_AXIS_DIM, 2, dtype=np.float64) / ROPE_AXIS_DIM)
    ang_r = (t // GRID_W)[:, None] * inv_freq[None, :]
    ang_c = (t % GRID_W)[:, None] * inv_freq[None, :]
    cos = np.concatenate([np.cos(ang_r)] * 2 + [np.cos(ang_c)] * 2, axis=-1)
    sin = np.concatenate([-np.sin(ang_r), np.sin(ang_r), -np.sin(ang_c), np.sin(ang_c)], axis=-1)
    return jnp.asarray(cos, F32), jnp.asarray(sin, F32)


def kernel(x_prompt, x_sample, cache_k, cache_v, c, c_ctx, norm1_w, norm2_w, ada_w, ada_b,
           conv_in_w, conv_w, conv_out_w, pool_w, pool_scale, attn_qkv_w, q_norm_w, k_norm_w,
           attn_out_w, ffn_in_w, ffn_out_w, final_norm_w):
    x = jnp.concatenate([x_prompt.reshape(NP, D), x_sample.reshape(NS, D)], axis=0)

    cvec = jnp.concatenate([c_ctx[None, :], c, jnp.zeros((8 - N_GROUPS, D), F32)], axis=0)
    mods = _adaln(cvec, ada_w, ada_b)
    mods = mods[:, :N_GROUPS].reshape(DEPTH, N_GROUPS, N_MOD, D)

    new_k = new_v = None
    for i in range(DEPTH):
        kind, j = i % 3, i // 3
        nw1 = norm1_w[i][None, :]
        if kind == 0:
            x = _conv_layer(x, mods[i], nw1, conv_in_w[j].astype(BF16), conv_w[j],
                            conv_out_w[j].astype(BF16))
        elif kind == 1:
            x = _pool_layer(x, mods[i], nw1, pool_w[j].astype(BF16), pool_scale[j][None, :])
        else:
            (qp, kp, vp, new_k, new_v), qkv_s = _qkv_layer(
                x, mods[i], nw1, attn_qkv_w[j].astype(BF16), q_norm_w[j][None, :],
                k_norm_w[j][None, :], _rope_tables())
            x = _attn_layer(x, (qp, kp, vp), qkv_s, cache_k, cache_v, mods[i],
                            attn_out_w[j].astype(BF16))
        x = _ffn_layer(x, mods[i], norm2_w[i][None, :], ffn_in_w[i].astype(BF16),
                       ffn_out_w[i].astype(BF16),
                       final_norm_w[None, :] if i == DEPTH - 1 else None)

    y_prompt = x[:NP].reshape(BATCH, SEQ, D)
    y_sample = x[NP:].reshape(DEC_BATCH, DEC_SEQ, D)
    return (y_prompt, y_sample, new_k, new_v)
```

```python
import functools

import numpy as np
import jax
import jax.numpy as jnp
from jax import lax
from jax.experimental import pallas as pl
from jax.experimental.pallas import tpu as pltpu

D = 1024
BATCH = 16
SEQ = 256
DEPTH = 4
DEC_BATCH = 2
DEC_SEQ = 4096
PAST_LEN = 256
GRID_W = 64
N_HEADS = 8
N_KV_HEADS = 2
HEAD_DIM = 128
Q_PER_KV = N_HEADS // N_KV_HEADS
ROPE_AXIS_DIM = HEAD_DIM // 2
ROPE_THETA = 10000.0
POOL_WINDOWS = (2, 4, 8, 16)
POOL_GROUP_DIM = D // 4
D_FF = 2816
N_MOD = 6
EPS = 1e-6
ATTN_SCALE = HEAD_DIM ** -0.5

NP = BATCH * SEQ
NS = DEC_BATCH * DEC_SEQ
NT = NP + NS
N_GROUPS = 1 + DEC_BATCH

QK_SCALE_LOG2 = ATTN_SCALE * float(np.log2(np.e))
BF16_SUBLANES = 16
VEXT_ROWS = HEAD_DIM + BF16_SUBLANES
HALO = BF16_SUBLANES
FF_CHUNK = 256
BF16 = jnp.bfloat16
F32 = jnp.float32


def _dot(a, b):
    return jnp.dot(a, b, preferred_element_type=F32)


def _dot_nt(a, b):
    return lax.dot_general(a, b, (((1,), (1,)), ((), ())), preferred_element_type=F32)


def _norm_mod(x, nw, shift, scale):
    y = x * lax.rsqrt(jnp.mean(x * x, axis=-1, keepdims=True) + EPS)
    return (y * nw) * (1.0 + scale) + shift


def _mod_group(tile, tm):
    return jnp.where(tile < NP // tm, 0, 1 + (tile - NP // tm) // (DEC_SEQ // tm))


def _params(vmem_mb, n_axes=1):
    return pltpu.CompilerParams(
        dimension_semantics=("arbitrary",) * n_axes,
        vmem_limit_bytes=vmem_mb * 1024 * 1024)


def _const_spec(shape):
    return pl.BlockSpec(shape, lambda *_: (0,) * len(shape), pipeline_mode=pl.Buffered(1))


def _layer_spec(shape, j):
    return pl.BlockSpec((None,) + tuple(shape), lambda *_: (j,) + (0,) * len(shape),
                        pipeline_mode=pl.Buffered(1))


def _adaln_kernel(c_ref, w_ref, b_ref, o_ref):
    c = c_ref[...]
    sc = (c * jax.nn.sigmoid(c)).astype(BF16)
    o_ref[...] = _dot(sc, w_ref[...].astype(BF16)) + b_ref[...]


def _adaln(cvec, ada_w, ada_b):
    tn = 1536
    return pl.pallas_call(
        _adaln_kernel,
        out_shape=jax.ShapeDtypeStruct((DEPTH, 8, N_MOD * D), F32),
        grid=(DEPTH, N_MOD * D // tn),
        in_specs=[
            pl.BlockSpec((8, D), lambda l, j: (0, 0)),
            pl.BlockSpec((None, D, tn), lambda l, j: (l, 0, j)),
            pl.BlockSpec((None, 1, tn), lambda l, j: (l, 0, j)),
        ],
        out_specs=pl.BlockSpec((None, 8, tn), lambda l, j: (l, 0, j)),
        compiler_params=_params(40, 2),
        name="adaln",
    )(cvec, ada_w, ada_b.reshape(DEPTH, 1, N_MOD * D))


def _conv_kernel(*refs, tm, split):
    i = pl.program_id(0)
    if split:
        (pm_ref, pp_ref, pn_ref, sm_ref, sp_ref, sn_ref,
         mod_ref, nw_ref, win_ref, cw_ref, wout_ref, o_ref, h_ref, acc_ref) = refs
        is_prompt = i < NP // tm
        x = jnp.where(is_prompt, pm_ref[...], sm_ref[...])
        x_prev = jnp.where(is_prompt, pp_ref[...], sp_ref[...])
        x_next = jnp.where(is_prompt, pn_ref[...], sn_ref[...])
    else:
        (x_ref, xp_ref, xn_ref,
         mod_ref, nw_ref, win_ref, cw_ref, wout_ref, o_ref, h_ref, acc_ref) = refs
        x, x_prev, x_next = x_ref[...], xp_ref[...], xn_ref[...]
    mod = mod_ref[...]
    nw = nw_ref[...]
    shift, scale, gate = mod[0:1], mod[1:2], mod[2:3]
    h_ref[0:HALO] = _norm_mod(x_prev, nw, shift, scale).astype(BF16)
    h_ref[HALO:HALO + tm] = _norm_mod(x, nw, shift, scale).astype(BF16)
    h_ref[HALO + tm:] = _norm_mod(x_next, nw, shift, scale).astype(BF16)

    seq_len = jnp.where(i < NP // tm, SEQ, DEC_SEQ)
    pos = (i * tm + lax.broadcasted_iota(jnp.int32, (tm, 1), 0)) & (seq_len - 1)
    has_prev = pos != 0
    has_next = pos != seq_len - 1
    cw = cw_ref[...]
    rows = tm + 2 * HALO
    for j in range(D // 256):
        c0 = j * 256
        b = _dot(h_ref[HALO:HALO + tm], win_ref[:, c0:c0 + 256])
        cg = _dot(h_ref[...], win_ref[:, D + c0:D + c0 + 256])
        u = _dot(h_ref[...], win_ref[:, 2 * D + c0:2 * D + c0 + 256])
        z = cg * u
        z_prev = pltpu.roll(z, 1, axis=0)[HALO:HALO + tm]
        z_next = pltpu.roll(z, rows - 1, axis=0)[HALO:HALO + tm]
        conv = (jnp.where(has_prev, z_prev, 0.0) * cw[0:1, c0:c0 + 256]
                + z[HALO:HALO + tm] * cw[1:2, c0:c0 + 256]
                + jnp.where(has_next, z_next, 0.0) * cw[2:3, c0:c0 + 256])
        part = _dot((b * conv).astype(BF16), wout_ref[c0:c0 + 256, :])
        if j == 0:
            acc_ref[...] = part
        else:
            acc_ref[...] += part
    o_ref[...] = x + gate * acc_ref[...]


def _halo_specs(tm, nrows, tile_of):
    nh = tm // HALO
    last = nrows // HALO - 1
    return [
        pl.BlockSpec((tm, D), lambda i: (tile_of(i), 0)),
        pl.BlockSpec((HALO, D), lambda i: (jnp.maximum(tile_of(i) * nh - 1, 0), 0)),
        pl.BlockSpec((HALO, D), lambda i: (jnp.minimum((tile_of(i) + 1) * nh, last), 0)),
    ]


def _conv_layer(xs, mods_l, nw, w_in, cw, w_out, j):
    tm = 512
    split = isinstance(xs, tuple)
    if split:
        npt = NP // tm
        x_specs = (_halo_specs(tm, NP, lambda i: jnp.minimum(i, npt - 1))
                   + _halo_specs(tm, NS, lambda i: jnp.maximum(i - npt, 0)))
        x_args = [xs[0]] * 3 + [xs[1]] * 3
    else:
        x_specs = _halo_specs(tm, NT, lambda i: i)
        x_args = [xs] * 3
    return pl.pallas_call(
        functools.partial(_conv_kernel, tm=tm, split=split),
        out_shape=jax.ShapeDtypeStruct((NT, D), F32),
        grid=(NT // tm,),
        in_specs=x_specs + [
            pl.BlockSpec((None, N_MOD, D), lambda i: (_mod_group(i, tm), 0, 0)),
            _const_spec((1, D)),
            _layer_spec((D, 3 * D), j),
            _layer_spec((3, D), j),
            _layer_spec((D, D), j),
        ],
        out_specs=pl.BlockSpec((tm, D), lambda i: (i, 0)),
        scratch_shapes=[pltpu.VMEM((tm + 2 * HALO, D), BF16), pltpu.VMEM((tm, D), F32)],
        compiler_params=_params(48),
        name="conv_mixer",
    )(*x_args, mods_l, nw, w_in, cw, w_out)


def _pool_kernel(x_ref, xp_ref, xn_ref, mod_ref, nw_ref, wp_ref, ps_ref, o_ref, h_ref, *, tm):
    i = pl.program_id(0)
    mod = mod_ref[...]
    nw = nw_ref[...]
    shift, scale, gate = mod[0:1], mod[1:2], mod[2:3]
    x = x_ref[...]
    seq_len = jnp.where(i < NP // tm, SEQ, DEC_SEQ)
    pos0 = (i * tm) & (seq_len - 1)
    at_start = pos0 == 0
    at_end = pos0 + tm == seq_len
    h_ref[0:HALO] = jnp.where(at_start, 0.0, _norm_mod(xp_ref[...], nw, shift, scale))
    h_ref[HALO:HALO + tm] = _norm_mod(x, nw, shift, scale)
    h_ref[HALO + tm:] = jnp.where(at_end, 0.0, _norm_mod(xn_ref[...], nw, shift, scale))

    rows = tm + 2 * HALO
    pos = pos0 + lax.broadcasted_iota(jnp.int32, (tm, 1), 0)
    ps = ps_ref[...]
    for g, w in enumerate(POOL_WINDOWS):
        c0 = g * POOL_GROUP_DIM
        hg = h_ref[:, c0:c0 + POOL_GROUP_DIM]
        p = hg + pltpu.roll(hg, 1, axis=0)
        step = 1
        while 2 * step < w:
            p = pltpu.roll(p, step, axis=0) + pltpu.roll(p, rows - step, axis=0)
            step *= 2
        cnt = (jnp.minimum(pos + w // 2, seq_len) - jnp.maximum(pos - w // 2, 0)).astype(F32)
        pooled = p[HALO:HALO + tm] / cnt
        diff = (pooled - hg[HALO:HALO + tm]).astype(BF16)
        mixed = _dot(diff, wp_ref[g]) * ps[:, c0:c0 + POOL_GROUP_DIM]
        o_ref[:, c0:c0 + POOL_GROUP_DIM] = (x[:, c0:c0 + POOL_GROUP_DIM]
                                            + gate[:, c0:c0 + POOL_GROUP_DIM] * mixed)


def _pool_layer(x, mods_l, nw, wp, ps, j):
    tm = 256
    return pl.pallas_call(
        functools.partial(_pool_kernel, tm=tm),
        out_shape=jax.ShapeDtypeStruct((NT, D), F32),
        grid=(NT // tm,),
        in_specs=_halo_specs(tm, NT, lambda i: i) + [
            pl.BlockSpec((None, N_MOD, D), lambda i: (_mod_group(i, tm), 0, 0)),
            _const_spec((1, D)),
            _layer_spec((4, POOL_GROUP_DIM, POOL_GROUP_DIM), j),
            _layer_spec((1, D), j),
        ],
        out_specs=pl.BlockSpec((tm, D), lambda i: (i, 0)),
        scratch_shapes=[pltpu.VMEM((tm + 2 * HALO, D), F32)],
        compiler_params=_params(32),
        name="pool_mixer",
    )(x, x, x, mods_l, nw, wp, ps)


def _head_norm(t, w):
    return t * lax.rsqrt(jnp.mean(t * t, axis=-1, keepdims=True) + EPS) * w


def _rope(t, cos, sin_signed, first_half):
    rot = jnp.where(first_half, pltpu.roll(t, HEAD_DIM - 32, axis=1), pltpu.roll(t, 32, axis=1))
    return t * cos + rot * sin_signed


def _qkv_kernel(*refs, tm, rope):
    if rope:
        (x_ref, mod_ref, nw_ref, w_ref, qn_ref, kn_ref, cos_ref, sin_ref,
         q_ref, k_ref, vt_ref) = refs
    else:
        (x_ref, mod_ref, nw_ref, w_ref, qn_ref, kn_ref,
         q_ref, k_ref, vt_ref, nk_ref, nv_ref) = refs
    mod = mod_ref[...]
    h = _norm_mod(x_ref[...], nw_ref[...], mod[0:1], mod[1:2]).astype(BF16)
    qkv = _dot(h, w_ref[...])
    qn = qn_ref[...]
    kn = kn_ref[...]
    if rope:
        cos = cos_ref[...]
        sin = sin_ref[...]
        first_half = (lax.broadcasted_iota(jnp.int32, (1, HEAD_DIM), 1) & 63) < 32
    for hd in range(N_HEADS):
        t = _head_norm(qkv[:, hd * HEAD_DIM:(hd + 1) * HEAD_DIM], qn)
        if rope:
            t = _rope(t, cos, sin, first_half)
        q_ref[:, hd * HEAD_DIM:(hd + 1) * HEAD_DIM] = (t * QK_SCALE_LOG2).astype(BF16)
    for g in range(N_KV_HEADS):
        ko = (N_HEADS + g) * HEAD_DIM
        vo = (N_HEADS + N_KV_HEADS + g) * HEAD_DIM
        t = _head_norm(qkv[:, ko:ko + HEAD_DIM], kn)
        vv = qkv[:, vo:vo + HEAD_DIM]
        if rope:
            t = _rope(t, cos, sin, first_half)
            vt_ref[g] = vv.T.astype(BF16)
        else:
            for bb in range(tm // SEQ):
                nk_ref[bb, 0, g] = t[bb * SEQ:(bb + 1) * SEQ]
                nv_ref[bb, 0, g] = vv[bb * SEQ:(bb + 1) * SEQ]
                vt_ref[bb, g] = vv[bb * SEQ:(bb + 1) * SEQ].T.astype(BF16)
        k_ref[:, g * HEAD_DIM:(g + 1) * HEAD_DIM] = t.astype(BF16)


def _qkv_layer(x, mods_l, nw, w_qkv, qn, kn, rope_tabs, j):
    tm = 512
    kvd = N_KV_HEADS * HEAD_DIM
    qkv_out = (N_HEADS + 2 * N_KV_HEADS) * HEAD_DIM

    def call(rope, row0, nrows):
        t0 = row0 // tm
        in_specs = [
            pl.BlockSpec((tm, D), lambda i: (t0 + i, 0)),
            pl.BlockSpec((None, N_MOD, D), lambda i: (_mod_group(t0 + i, tm), 0, 0)),
            _const_spec((1, D)),
            _layer_spec((D, qkv_out), j),
            _layer_spec((1, HEAD_DIM), j),
            _layer_spec((1, HEAD_DIM), j),
        ]
        out_shape = [jax.ShapeDtypeStruct((nrows, D), BF16),
                     jax.ShapeDtypeStruct((nrows, kvd), BF16)]
        out_specs = [pl.BlockSpec((tm, D), lambda i: (i, 0)),
                     pl.BlockSpec((tm, kvd), lambda i: (i, 0))]
        args = [x, mods_l, nw, w_qkv, qn, kn]
        if rope:
            tps = DEC_SEQ // tm
            in_specs += [pl.BlockSpec((tm, HEAD_DIM), lambda i: (i % tps, 0)),
                         pl.BlockSpec((tm, HEAD_DIM), lambda i: (i % tps, 0))]
            args += list(rope_tabs)
            out_shape.append(jax.ShapeDtypeStruct((DEC_BATCH, N_KV_HEADS, HEAD_DIM, DEC_SEQ), BF16))
            out_specs.append(pl.BlockSpec((None, N_KV_HEADS, HEAD_DIM, tm),
                                          lambda i: (i // tps, 0, 0, i % tps)))
        else:
            nb = tm // SEQ
            out_shape.append(jax.ShapeDtypeStruct((BATCH, N_KV_HEADS, HEAD_DIM, SEQ), BF16))
            out_specs.append(pl.BlockSpec((nb, N_KV_HEADS, HEAD_DIM, SEQ), lambda i: (i, 0, 0, 0)))
            kv_shape = (BATCH, 1, N_KV_HEADS, SEQ, HEAD_DIM)
            out_shape += [jax.ShapeDtypeStruct(kv_shape, F32)] * 2
            out_specs += [pl.BlockSpec((nb, 1, N_KV_HEADS, SEQ, HEAD_DIM),
                                       lambda i: (i, 0, 0, 0, 0))] * 2
        return pl.pallas_call(
            functools.partial(_qkv_kernel, tm=tm, rope=rope),
            out_shape=out_shape,
            grid=(nrows // tm,),
            in_specs=in_specs,
            out_specs=out_specs,
            compiler_params=_params(40),
            name="qkv_rope" if rope else "qkv_ctx",
        )(*args)

    return call(False, 0, NP), call(True, NP, NS)


def _attn_kernel(*refs, past, hpb):
    if past:
        (x_ref, q_ref, k_ref, vt_ref, ck_ref, cv_ref, mod_ref, wo_ref,
         o_ref, kall_ref, vext_ref, heads_ref) = refs
    else:
        (x_ref, q_ref, k_ref, vt_ref, mod_ref, wo_ref,
         o_ref, kall_ref, vext_ref, heads_ref) = refs
    nkeys = kall_ref.shape[1]

    @pl.when(pl.program_id(1) == 0)
    def _():
        for g in range(N_KV_HEADS):
            if past:
                kall_ref[g, 0:past] = ck_ref[0, 0, g].astype(BF16)
                vext_ref[g, 0:HEAD_DIM, 0:past] = cv_ref[0, 0, g].T.astype(BF16)
            kall_ref[g, past:] = k_ref[:, g * HEAD_DIM:(g + 1) * HEAD_DIM]
            vext_ref[g, 0:HEAD_DIM, past:] = vt_ref[g]
            vext_ref[g, HEAD_DIM:] = jnp.ones((VEXT_ROWS - HEAD_DIM, nkeys), BF16)

    tq = q_ref.shape[0]
    blocks = [list(range(h0, h0 + hpb)) for h0 in range(0, N_HEADS, hpb)]

    def scores(block):
        qb = [q_ref[:, hd * HEAD_DIM:(hd + 1) * HEAD_DIM] for hd in block]
        qb = qb[0] if hpb == 1 else jnp.concatenate(qb, axis=0)
        return _dot_nt(kall_ref[block[0] // Q_PER_KV], qb)

    st_next = scores(blocks[0])
    for b, block in enumerate(blocks):
        st = st_next
        if b + 1 < len(blocks):
            st_next = scores(blocks[b + 1])
        m = jnp.max(st, axis=0, keepdims=True)
        pt = jnp.exp2(st - m).astype(BF16)
        ot = _dot(vext_ref[block[0] // Q_PER_KV], pt)
        o = (ot[0:HEAD_DIM] / ot[HEAD_DIM:HEAD_DIM + 1]).T
        for r, hd in enumerate(block):
            heads_ref[:, hd * HEAD_DIM:(hd + 1) * HEAD_DIM] = o[r * tq:(r + 1) * tq].astype(BF16)
    gate = mod_ref[...][2:3]
    o_ref[...] = x_ref[...] + gate * _dot(heads_ref[...], wo_ref[...])


def _attn_layer(x, qkv_p, qkv_s, cache_k, cache_v, mods_l, w_out, j):
    kvd = N_KV_HEADS * HEAD_DIM

    def call(x, q, k, vt, past, row0, nb, seq, tq, hpb):
        t0 = row0 // tq
        nq = seq // tq
        in_specs = [
            pl.BlockSpec((tq, D), lambda b, i: (t0 + b * nq + i, 0)),
            pl.BlockSpec((tq, D), lambda b, i: (b * nq + i, 0)),
            pl.BlockSpec((seq, kvd), lambda b, i: (b, 0)),
            pl.BlockSpec((None, N_KV_HEADS, HEAD_DIM, seq), lambda b, i: (b, 0, 0, 0)),
        ]
        args = [x, q, k, vt]
        if past:
            cspec = pl.BlockSpec((1, 1, N_KV_HEADS, PAST_LEN, HEAD_DIM),
                                 lambda b, i: (b, j, 0, 0, 0))
            in_specs += [cspec, cspec]
            args += [cache_k, cache_v]
        in_specs += [
            pl.BlockSpec((None, N_MOD, D), lambda b, i: (_mod_group(t0 + b * nq + i, tq), 0, 0)),
            _layer_spec((D, D), j),
        ]
        args += [mods_l, w_out]
        return pl.pallas_call(
            functools.partial(_attn_kernel, past=past, hpb=hpb),
            out_shape=jax.ShapeDtypeStruct((NT, D), F32),
            grid=(nb, nq),
            in_specs=in_specs,
            out_specs=pl.BlockSpec((tq, D), lambda b, i: (t0 + b * nq + i, 0)),
            scratch_shapes=[pltpu.VMEM((N_KV_HEADS, past + seq, HEAD_DIM), BF16),
                            pltpu.VMEM((N_KV_HEADS, VEXT_ROWS, past + seq), BF16),
                            pltpu.VMEM((tq, D), BF16)],
            input_output_aliases={0: 0},
            compiler_params=_params(48, 2),
            name="attn_cached" if past else "attn_ctx",
        )(*args)

    x = call(x, *qkv_p, 0, 0, BATCH, SEQ, SEQ, Q_PER_KV)
    return call(x, *qkv_s, PAST_LEN, NP, DEC_BATCH, DEC_SEQ, 512, 1)


def _ffn_kernel(*refs, final, tm):
    if final:
        x_ref, mod_ref, nw_ref, win_ref, wout_ref, fw_ref, op_ref, os_ref, h_ref, acc_ref = refs
    else:
        x_ref, mod_ref, nw_ref, win_ref, wout_ref, o_ref, h_ref, acc_ref = refs
    mod = mod_ref[...]
    x = x_ref[...]
    h_ref[...] = _norm_mod(x, nw_ref[...], mod[3:4], mod[4:5]).astype(BF16)
    for j in range(D_FF // FF_CHUNK):
        c0 = j * FF_CHUNK
        g = _dot(h_ref[...], win_ref[:, c0:c0 + FF_CHUNK])
        u = _dot(h_ref[...], win_ref[:, D_FF + c0:D_FF + c0 + FF_CHUNK])
        a = (g * jax.nn.sigmoid(g) * u).astype(BF16)
        part = _dot(a, wout_ref[c0:c0 + FF_CHUNK, :])
        if j == 0:
            acc_ref[...] = part
        else:
            acc_ref[...] += part
    y = x + mod[5:6] * acc_ref[...]
    if not final:
        o_ref[...] = y
        return
    y = y * lax.rsqrt(jnp.mean(y * y, axis=-1, keepdims=True) + EPS) * fw_ref[...]
    is_prompt = pl.program_id(0) < NP // tm

    @pl.when(is_prompt)
    def _():
        op_ref[...] = y

    @pl.when(jnp.logical_not(is_prompt))
    def _():
        os_ref[...] = y


def _ffn_layer(x, mods_l, nw, w_in, w_out, layer, final_w=None):
    tm = 512
    final = final_w is not None
    in_specs = [
        pl.BlockSpec((tm, D), lambda i: (i, 0)),
        pl.BlockSpec((None, N_MOD, D), lambda i: (_mod_group(i, tm), 0, 0)),
        _const_spec((1, D)),
        _layer_spec((D, 2 * D_FF), layer),
        _layer_spec((D_FF, D), layer),
    ]
    args = [x, mods_l, nw, w_in, w_out]
    if final:
        npt = NP // tm
        in_specs.append(_const_spec((1, D)))
        args.append(final_w)
        out_shape = [jax.ShapeDtypeStruct((NP, D), F32), jax.ShapeDtypeStruct((NS, D), F32)]
        out_specs = [pl.BlockSpec((tm, D), lambda i: (jnp.minimum(i, npt - 1), 0)),
                     pl.BlockSpec((tm, D), lambda i: (jnp.maximum(i - npt, 0), 0))]
    else:
        out_shape = jax.ShapeDtypeStruct((NT, D), F32)
        out_specs = pl.BlockSpec((tm, D), lambda i: (i, 0))
    return pl.pallas_call(
        functools.partial(_ffn_kernel, final=final, tm=tm),
        out_shape=out_shape,
        grid=(NT // tm,),
        in_specs=in_specs,
        out_specs=out_specs,
        scratch_shapes=[pltpu.VMEM((tm, D), BF16), pltpu.VMEM((tm, D), F32)],
        compiler_params=_params(48),
        name="ffn_final" if final else "ffn",
    )(*args)


def _rope_tables():
    t = np.arange(DEC_SEQ)
    inv_freq = ROPE_THETA ** (-np.arange(0, ROPE_AXIS_DIM, 2, dtype=np.float64) / ROPE_AXIS_DIM)
    ang_r = (t // GRID_W)[:, None] * inv_freq[None, :]
    ang_c = (t % GRID_W)[:, None] * inv_freq[None, :]
    cos = np.concatenate([np.cos(ang_r)] * 2 + [np.cos(ang_c)] * 2, axis=-1)
    sin = np.concatenate([-np.sin(ang_r), np.sin(ang_r), -np.sin(ang_c), np.sin(ang_c)], axis=-1)
    return jnp.asarray(cos, F32), jnp.asarray(sin, F32)


def kernel(x_prompt, x_sample, cache_k, cache_v, c, c_ctx, norm1_w, norm2_w, ada_w, ada_b,
           conv_in_w, conv_w, conv_out_w, pool_w, pool_scale, attn_qkv_w, q_norm_w, k_norm_w,
           attn_out_w, ffn_in_w, ffn_out_w, final_norm_w):
    cvec = jnp.concatenate([c_ctx[None, :], c, jnp.zeros((8 - N_GROUPS, D), F32)], axis=0)
    mods = _adaln(cvec, ada_w, ada_b)
    mods = mods[:, :N_GROUPS].reshape(DEPTH, N_GROUPS, N_MOD, D)

    conv_in_b, conv_out_b = conv_in_w.astype(BF16), conv_out_w.astype(BF16)
    pool_b = pool_w.astype(BF16)
    qkv_b, attn_out_b = attn_qkv_w.astype(BF16), attn_out_w.astype(BF16)
    ffn_in_b, ffn_out_b = ffn_in_w.astype(BF16), ffn_out_w.astype(BF16)
    pool_scale3 = pool_scale[:, None, :]
    qn3, kn3 = q_norm_w[:, None, :], k_norm_w[:, None, :]

    x = (x_prompt.reshape(NP, D), x_sample.reshape(NS, D))
    new_k = new_v = None
    for i in range(DEPTH):
        kind, j = i % 3, i // 3
        nw1 = norm1_w[i][None, :]
        if kind == 0:
            x = _conv_layer(x, mods[i], nw1, conv_in_b, conv_w, conv_out_b, j)
        elif kind == 1:
            x = _pool_layer(x, mods[i], nw1, pool_b, pool_scale3, j)
        else:
            (qp, kp, vtp, new_k, new_v), qkv_s = _qkv_layer(
                x, mods[i], nw1, qkv_b, qn3, kn3, _rope_tables(), j)
            x = _attn_layer(x, (qp, kp, vtp), qkv_s, cache_k, cache_v, mods[i], attn_out_b, j)
        x = _ffn_layer(x, mods[i], norm2_w[i][None, :], ffn_in_b, ffn_out_b, i,
                       final_norm_w[None, :] if i == DEPTH - 1 else None)

    y_prompt, y_sample = x
    return (y_prompt.reshape(BATCH, SEQ, D), y_sample.reshape(DEC_BATCH, DEC_SEQ, D), new_k, new_v)
```

```python
import functools

import numpy as np
import jax
import jax.numpy as jnp
from jax import lax
from jax.experimental import pallas as pl
from jax.experimental.pallas import tpu as pltpu

D = 1024
BATCH = 16
SEQ = 256
DEPTH = 4
DEC_BATCH = 2
DEC_SEQ = 4096
PAST_LEN = 256
GRID_W = 64
N_HEADS = 8
N_KV_HEADS = 2
HEAD_DIM = 128
Q_PER_KV = N_HEADS // N_KV_HEADS
ROPE_AXIS_DIM = HEAD_DIM // 2
ROPE_THETA = 10000.0
POOL_WINDOWS = (2, 4, 8, 16)
POOL_GROUP_DIM = D // 4
D_FF = 2816
N_MOD = 6
EPS = 1e-6
ATTN_SCALE = HEAD_DIM ** -0.5

NP = BATCH * SEQ
NS = DEC_BATCH * DEC_SEQ
NT = NP + NS
N_GROUPS = 1 + DEC_BATCH

QK_SCALE_LOG2 = ATTN_SCALE * float(np.log2(np.e))
BF16_SUBLANES = 16
VEXT_ROWS = HEAD_DIM + BF16_SUBLANES
HALO = BF16_SUBLANES
FF_CHUNK = 256
BF16 = jnp.bfloat16
F32 = jnp.float32


def _dot(a, b):
    return jnp.dot(a, b, preferred_element_type=F32)


def _wdot(a, w):
    return _dot(a, w.astype(BF16))


def _dot_nt(a, b):
    return lax.dot_general(a, b, (((1,), (1,)), ((), ())), preferred_element_type=F32)


def _norm_mod(x, nw, shift, scale):
    y = x * lax.rsqrt(jnp.mean(x * x, axis=-1, keepdims=True) + EPS)
    return (y * nw) * (1.0 + scale) + shift


def _mod_group(tile, tm):
    return jnp.where(tile < NP // tm, 0, 1 + (tile - NP // tm) // (DEC_SEQ // tm))


def _params(vmem_mb, n_axes=1):
    return pltpu.CompilerParams(
        dimension_semantics=("arbitrary",) * n_axes,
        vmem_limit_bytes=vmem_mb * 1024 * 1024)


def _const_spec(shape):
    return pl.BlockSpec(shape, lambda *_: (0,) * len(shape), pipeline_mode=pl.Buffered(1))


def _layer_spec(shape, j):
    return pl.BlockSpec((None,) + tuple(shape), lambda *_: (j,) + (0,) * len(shape),
                        pipeline_mode=pl.Buffered(1))


def _adaln_kernel(c_ref, w_ref, b_ref, o_ref):
    c = c_ref[...]
    sc = (c * jax.nn.sigmoid(c)).astype(BF16)
    o_ref[...] = _dot(sc, w_ref[...].astype(BF16)) + b_ref[...]


def _adaln(cvec, ada_w, ada_b):
    tn = 1536
    return pl.pallas_call(
        _adaln_kernel,
        out_shape=jax.ShapeDtypeStruct((DEPTH, 8, N_MOD * D), F32),
        grid=(DEPTH, N_MOD * D // tn),
        in_specs=[
            pl.BlockSpec((8, D), lambda l, j: (0, 0)),
            pl.BlockSpec((None, D, tn), lambda l, j: (l, 0, j)),
            pl.BlockSpec((None, 1, tn), lambda l, j: (l, 0, j)),
        ],
        out_specs=pl.BlockSpec((None, 8, tn), lambda l, j: (l, 0, j)),
        compiler_params=_params(40, 2),
        name="adaln",
    )(cvec, ada_w, ada_b.reshape(DEPTH, 1, N_MOD * D))


def _conv_kernel(*refs, tm, split):
    i = pl.program_id(0)
    if split:
        (pm_ref, pp_ref, pn_ref, sm_ref, sp_ref, sn_ref,
         mod_ref, nw_ref, win_ref, cw_ref, wout_ref, o_ref, h_ref, acc_ref) = refs
        is_prompt = i < NP // tm
        x = jnp.where(is_prompt, pm_ref[...], sm_ref[...])
        x_prev = jnp.where(is_prompt, pp_ref[...], sp_ref[...])
        x_next = jnp.where(is_prompt, pn_ref[...], sn_ref[...])
    else:
        (x_ref, xp_ref, xn_ref,
         mod_ref, nw_ref, win_ref, cw_ref, wout_ref, o_ref, h_ref, acc_ref) = refs
        x, x_prev, x_next = x_ref[...], xp_ref[...], xn_ref[...]
    mod = mod_ref[...]
    nw = nw_ref[...]
    shift, scale, gate = mod[0:1], mod[1:2], mod[2:3]
    h_ref[0:HALO] = _norm_mod(x_prev, nw, shift, scale).astype(BF16)
    h_ref[HALO:HALO + tm] = _norm_mod(x, nw, shift, scale).astype(BF16)
    h_ref[HALO + tm:] = _norm_mod(x_next, nw, shift, scale).astype(BF16)

    seq_len = jnp.where(i < NP // tm, SEQ, DEC_SEQ)
    pos = (i * tm + lax.broadcasted_iota(jnp.int32, (tm, 1), 0)) & (seq_len - 1)
    has_prev = pos != 0
    has_next = pos != seq_len - 1
    cw = cw_ref[...]
    rows = tm + 2 * HALO
    for j in range(D // 256):
        c0 = j * 256
        b = _wdot(h_ref[HALO:HALO + tm], win_ref[:, c0:c0 + 256])
        cg = _wdot(h_ref[...], win_ref[:, D + c0:D + c0 + 256])
        u = _wdot(h_ref[...], win_ref[:, 2 * D + c0:2 * D + c0 + 256])
        z = cg * u
        z_prev = pltpu.roll(z, 1, axis=0)[HALO:HALO + tm]
        z_next = pltpu.roll(z, rows - 1, axis=0)[HALO:HALO + tm]
        conv = (jnp.where(has_prev, z_prev, 0.0) * cw[0:1, c0:c0 + 256]
                + z[HALO:HALO + tm] * cw[1:2, c0:c0 + 256]
                + jnp.where(has_next, z_next, 0.0) * cw[2:3, c0:c0 + 256])
        part = _wdot((b * conv).astype(BF16), wout_ref[c0:c0 + 256, :])
        if j == 0:
            acc_ref[...] = part
        else:
            acc_ref[...] += part
    o_ref[...] = x + gate * acc_ref[...]


def _halo_specs(tm, nrows, tile_of):
    nh = tm // HALO
    last = nrows // HALO - 1
    return [
        pl.BlockSpec((tm, D), lambda i: (tile_of(i), 0)),
        pl.BlockSpec((HALO, D), lambda i: (jnp.maximum(tile_of(i) * nh - 1, 0), 0)),
        pl.BlockSpec((HALO, D), lambda i: (jnp.minimum((tile_of(i) + 1) * nh, last), 0)),
    ]


def _conv_layer(xs, mods_l, nw, w_in, cw, w_out, j):
    tm = 512
    split = isinstance(xs, tuple)
    if split:
        npt = NP // tm
        x_specs = (_halo_specs(tm, NP, lambda i: jnp.minimum(i, npt - 1))
                   + _halo_specs(tm, NS, lambda i: jnp.maximum(i - npt, 0)))
        x_args = [xs[0]] * 3 + [xs[1]] * 3
    else:
        x_specs = _halo_specs(tm, NT, lambda i: i)
        x_args = [xs] * 3
    return pl.pallas_call(
        functools.partial(_conv_kernel, tm=tm, split=split),
        out_shape=jax.ShapeDtypeStruct((NT, D), F32),
        grid=(NT // tm,),
        in_specs=x_specs + [
            pl.BlockSpec((None, N_MOD, D), lambda i: (_mod_group(i, tm), 0, 0)),
            _const_spec((1, D)),
            _layer_spec((D, 3 * D), j),
            _layer_spec((3, D), j),
            _layer_spec((D, D), j),
        ],
        out_specs=pl.BlockSpec((tm, D), lambda i: (i, 0)),
        scratch_shapes=[pltpu.VMEM((tm + 2 * HALO, D), BF16), pltpu.VMEM((tm, D), F32)],
        compiler_params=_params(48),
        name="conv_mixer",
    )(*x_args, mods_l, nw, w_in, cw, w_out)


def _pool_kernel(x_ref, xp_ref, xn_ref, mod_ref, nw_ref, wp_ref, ps_ref, o_ref, h_ref, *, tm):
    i = pl.program_id(0)
    mod = mod_ref[...]
    nw = nw_ref[...]
    shift, scale, gate = mod[0:1], mod[1:2], mod[2:3]
    x = x_ref[...]
    seq_len = jnp.where(i < NP // tm, SEQ, DEC_SEQ)
    pos0 = (i * tm) & (seq_len - 1)
    at_start = pos0 == 0
    at_end = pos0 + tm == seq_len
    h_ref[0:HALO] = jnp.where(at_start, 0.0, _norm_mod(xp_ref[...], nw, shift, scale))
    h_ref[HALO:HALO + tm] = _norm_mod(x, nw, shift, scale)
    h_ref[HALO + tm:] = jnp.where(at_end, 0.0, _norm_mod(xn_ref[...], nw, shift, scale))

    rows = tm + 2 * HALO
    pos = pos0 + lax.broadcasted_iota(jnp.int32, (tm, 1), 0)
    ps = ps_ref[...]
    for g, w in enumerate(POOL_WINDOWS):
        c0 = g * POOL_GROUP_DIM
        hg = h_ref[:, c0:c0 + POOL_GROUP_DIM]
        p = hg + pltpu.roll(hg, 1, axis=0)
        step = 1
        while 2 * step < w:
            p = pltpu.roll(p, step, axis=0) + pltpu.roll(p, rows - step, axis=0)
            step *= 2
        cnt = (jnp.minimum(pos + w // 2, seq_len) - jnp.maximum(pos - w // 2, 0)).astype(F32)
        pooled = p[HALO:HALO + tm] / cnt
        diff = (pooled - hg[HALO:HALO + tm]).astype(BF16)
        mixed = _wdot(diff, wp_ref[g]) * ps[:, c0:c0 + POOL_GROUP_DIM]
        o_ref[:, c0:c0 + POOL_GROUP_DIM] = (x[:, c0:c0 + POOL_GROUP_DIM]
                                            + gate[:, c0:c0 + POOL_GROUP_DIM] * mixed)


def _pool_layer(x, mods_l, nw, wp, ps, j):
    tm = 256
    return pl.pallas_call(
        functools.partial(_pool_kernel, tm=tm),
        out_shape=jax.ShapeDtypeStruct((NT, D), F32),
        grid=(NT // tm,),
        in_specs=_halo_specs(tm, NT, lambda i: i) + [
            pl.BlockSpec((None, N_MOD, D), lambda i: (_mod_group(i, tm), 0, 0)),
            _const_spec((1, D)),
            _layer_spec((4, POOL_GROUP_DIM, POOL_GROUP_DIM), j),
            _layer_spec((1, D), j),
        ],
        out_specs=pl.BlockSpec((tm, D), lambda i: (i, 0)),
        scratch_shapes=[pltpu.VMEM((tm + 2 * HALO, D), F32)],
        compiler_params=_params(32),
        name="pool_mixer",
    )(x, x, x, mods_l, nw, wp, ps)


def _head_norm(t, w):
    return t * lax.rsqrt(jnp.mean(t * t, axis=-1, keepdims=True) + EPS) * w


def _rope(t, cos, sin_signed, first_half):
    rot = jnp.where(first_half, pltpu.roll(t, HEAD_DIM - 32, axis=1), pltpu.roll(t, 32, axis=1))
    return t * cos + rot * sin_signed


def _qkv_kernel(*refs, tm, rope):
    if rope:
        (x_ref, mod_ref, nw_ref, w_ref, qn_ref, kn_ref, cos_ref, sin_ref,
         q_ref, k_ref, vt_ref) = refs
    else:
        (x_ref, mod_ref, nw_ref, w_ref, qn_ref, kn_ref,
         q_ref, k_ref, vt_ref, nk_ref, nv_ref) = refs
    mod = mod_ref[...]
    h = _norm_mod(x_ref[...], nw_ref[...], mod[0:1], mod[1:2]).astype(BF16)
    qkv = _wdot(h, w_ref[...])
    qn = qn_ref[...]
    kn = kn_ref[...]
    if rope:
        cos = cos_ref[...]
        sin = sin_ref[...]
        first_half = (lax.broadcasted_iota(jnp.int32, (1, HEAD_DIM), 1) & 63) < 32
    for hd in range(N_HEADS):
        t = _head_norm(qkv[:, hd * HEAD_DIM:(hd + 1) * HEAD_DIM], qn)
        if rope:
            t = _rope(t, cos, sin, first_half)
        q_ref[:, hd * HEAD_DIM:(hd + 1) * HEAD_DIM] = (t * QK_SCALE_LOG2).astype(BF16)
    for g in range(N_KV_HEADS):
        ko = (N_HEADS + g) * HEAD_DIM
        vo = (N_HEADS + N_KV_HEADS + g) * HEAD_DIM
        t = _head_norm(qkv[:, ko:ko + HEAD_DIM], kn)
        vv = qkv[:, vo:vo + HEAD_DIM]
        if rope:
            t = _rope(t, cos, sin, first_half)
            vt_ref[g] = vv.T.astype(BF16)
        else:
            for bb in range(tm // SEQ):
                nk_ref[bb, 0, g] = t[bb * SEQ:(bb + 1) * SEQ]
                nv_ref[bb, 0, g] = vv[bb * SEQ:(bb + 1) * SEQ]
                vt_ref[bb, g] = vv[bb * SEQ:(bb + 1) * SEQ].T.astype(BF16)
        k_ref[:, g * HEAD_DIM:(g + 1) * HEAD_DIM] = t.astype(BF16)


def _qkv_layer(x, mods_l, nw, w_qkv, qn, kn, rope_tabs, j):
    tm = 512
    kvd = N_KV_HEADS * HEAD_DIM
    qkv_out = (N_HEADS + 2 * N_KV_HEADS) * HEAD_DIM

    def call(rope, row0, nrows):
        t0 = row0 // tm
        in_specs = [
            pl.BlockSpec((tm, D), lambda i: (t0 + i, 0)),
            pl.BlockSpec((None, N_MOD, D), lambda i: (_mod_group(t0 + i, tm), 0, 0)),
            _const_spec((1, D)),
            _layer_spec((D, qkv_out), j),
            _layer_spec((1, HEAD_DIM), j),
            _layer_spec((1, HEAD_DIM), j),
        ]
        out_shape = [jax.ShapeDtypeStruct((nrows, D), BF16),
                     jax.ShapeDtypeStruct((nrows, kvd), BF16)]
        out_specs = [pl.BlockSpec((tm, D), lambda i: (i, 0)),
                     pl.BlockSpec((tm, kvd), lambda i: (i, 0))]
        args = [x, mods_l, nw, w_qkv, qn, kn]
        if rope:
            tps = DEC_SEQ // tm
            in_specs += [pl.BlockSpec((tm, HEAD_DIM), lambda i: (i % tps, 0)),
                         pl.BlockSpec((tm, HEAD_DIM), lambda i: (i % tps, 0))]
            args += list(rope_tabs)
            out_shape.append(jax.ShapeDtypeStruct((DEC_BATCH, N_KV_HEADS, HEAD_DIM, DEC_SEQ), BF16))
            out_specs.append(pl.BlockSpec((None, N_KV_HEADS, HEAD_DIM, tm),
                                          lambda i: (i // tps, 0, 0, i % tps)))
        else:
            nb = tm // SEQ
            out_shape.append(jax.ShapeDtypeStruct((BATCH, N_KV_HEADS, HEAD_DIM, SEQ), BF16))
            out_specs.append(pl.BlockSpec((nb, N_KV_HEADS, HEAD_DIM, SEQ), lambda i: (i, 0, 0, 0)))
            kv_shape = (BATCH, 1, N_KV_HEADS, SEQ, HEAD_DIM)
            out_shape += [jax.ShapeDtypeStruct(kv_shape, F32)] * 2
            out_specs += [pl.BlockSpec((nb, 1, N_KV_HEADS, SEQ, HEAD_DIM),
                                       lambda i: (i, 0, 0, 0, 0))] * 2
        return pl.pallas_call(
            functools.partial(_qkv_kernel, tm=tm, rope=rope),
            out_shape=out_shape,
            grid=(nrows // tm,),
            in_specs=in_specs,
            out_specs=out_specs,
            compiler_params=_params(40),
            name="qkv_rope" if rope else "qkv_ctx",
        )(*args)

    return call(False, 0, NP), call(True, NP, NS)


def _attn_kernel(*refs, past, hpb):
    if past:
        (x_ref, q_ref, k_ref, vt_ref, ck_ref, cv_ref, mod_ref, wo_ref,
         o_ref, kall_ref, vext_ref, heads_ref) = refs
    else:
        (x_ref, q_ref, k_ref, vt_ref, mod_ref, wo_ref,
         o_ref, kall_ref, vext_ref, heads_ref) = refs
    nkeys = kall_ref.shape[1]

    @pl.when(pl.program_id(1) == 0)
    def _():
        for g in range(N_KV_HEADS):
            if past:
                kall_ref[g, 0:past] = ck_ref[0, 0, g].astype(BF16)
                vext_ref[g, 0:HEAD_DIM, 0:past] = cv_ref[0, 0, g].T.astype(BF16)
            kall_ref[g, past:] = k_ref[:, g * HEAD_DIM:(g + 1) * HEAD_DIM]
            vext_ref[g, 0:HEAD_DIM, past:] = vt_ref[g]
            vext_ref[g, HEAD_DIM:] = jnp.ones((VEXT_ROWS - HEAD_DIM, nkeys), BF16)

    tq = q_ref.shape[0]
    blocks = [list(range(h0, h0 + hpb)) for h0 in range(0, N_HEADS, hpb)]

    def scores(block):
        qb = [q_ref[:, hd * HEAD_DIM:(hd + 1) * HEAD_DIM] for hd in block]
        qb = qb[0] if hpb == 1 else jnp.concatenate(qb, axis=0)
        return _dot_nt(kall_ref[block[0] // Q_PER_KV], qb)

    st_next = scores(blocks[0])
    for b, block in enumerate(blocks):
        st = st_next
        if b + 1 < len(blocks):
            st_next = scores(blocks[b + 1])
        m = jnp.max(st, axis=0, keepdims=True)
        pt = jnp.exp2(st - m).astype(BF16)
        ot = _dot(vext_ref[block[0] // Q_PER_KV], pt)
        o = (ot[0:HEAD_DIM] / ot[HEAD_DIM:HEAD_DIM + 1]).T
        for r, hd in enumerate(block):
            heads_ref[:, hd * HEAD_DIM:(hd + 1) * HEAD_DIM] = o[r * tq:(r + 1) * tq].astype(BF16)
    gate = mod_ref[...][2:3]
    o_ref[...] = x_ref[...] + gate * _wdot(heads_ref[...], wo_ref[...])


def _attn_layer(x, qkv_p, qkv_s, cache_k, cache_v, mods_l, w_out, j):
    kvd = N_KV_HEADS * HEAD_DIM

    def call(x, q, k, vt, past, row0, nb, seq, tq, hpb):
        t0 = row0 // tq
        nq = seq // tq
        in_specs = [
            pl.BlockSpec((tq, D), lambda b, i: (t0 + b * nq + i, 0)),
            pl.BlockSpec((tq, D), lambda b, i: (b * nq + i, 0)),
            pl.BlockSpec((seq, kvd), lambda b, i: (b, 0)),
            pl.BlockSpec((None, N_KV_HEADS, HEAD_DIM, seq), lambda b, i: (b, 0, 0, 0)),
        ]
        args = [x, q, k, vt]
        if past:
            cspec = pl.BlockSpec((1, 1, N_KV_HEADS, PAST_LEN, HEAD_DIM),
                                 lambda b, i: (b, j, 0, 0, 0))
            in_specs += [cspec, cspec]
            args += [cache_k, cache_v]
        in_specs += [
            pl.BlockSpec((None, N_MOD, D), lambda b, i: (_mod_group(t0 + b * nq + i, tq), 0, 0)),
            _layer_spec((D, D), j),
        ]
        args += [mods_l, w_out]
        return pl.pallas_call(
            functools.partial(_attn_kernel, past=past, hpb=hpb),
            out_shape=jax.ShapeDtypeStruct((NT, D), F32),
            grid=(nb, nq),
            in_specs=in_specs,
            out_specs=pl.BlockSpec((tq, D), lambda b, i: (t0 + b * nq + i, 0)),
            scratch_shapes=[pltpu.VMEM((N_KV_HEADS, past + seq, HEAD_DIM), BF16),
                            pltpu.VMEM((N_KV_HEADS, VEXT_ROWS, past + seq), BF16),
                            pltpu.VMEM((tq, D), BF16)],
            input_output_aliases={0: 0},
            compiler_params=_params(48, 2),
            name="attn_cached" if past else "attn_ctx",
        )(*args)

    x = call(x, *qkv_p, 0, 0, BATCH, SEQ, SEQ, Q_PER_KV)
    return call(x, *qkv_s, PAST_LEN, NP, DEC_BATCH, DEC_SEQ, 512, 1)


def _ffn_kernel(*refs, final, tm):
    if final:
        x_ref, mod_ref, nw_ref, win_ref, wout_ref, fw_ref, op_ref, os_ref, h_ref, acc_ref = refs
    else:
        x_ref, mod_ref, nw_ref, win_ref, wout_ref, o_ref, h_ref, acc_ref = refs
    mod = mod_ref[...]
    x = x_ref[...]
    h_ref[...] = _norm_mod(x, nw_ref[...], mod[3:4], mod[4:5]).astype(BF16)
    for j in range(D_FF // FF_CHUNK):
        c0 = j * FF_CHUNK
        g = _wdot(h_ref[...], win_ref[:, c0:c0 + FF_CHUNK])
        u = _wdot(h_ref[...], win_ref[:, D_FF + c0:D_FF + c0 + FF_CHUNK])
        a = (g * jax.nn.sigmoid(g) * u).astype(BF16)
        part = _wdot(a, wout_ref[c0:c0 + FF_CHUNK, :])
        if j == 0:
            acc_ref[...] = part
        else:
            acc_ref[...] += part
    y = x + mod[5:6] * acc_ref[...]
    if not final:
        o_ref[...] = y
        return
    y = y * lax.rsqrt(jnp.mean(y * y, axis=-1, keepdims=True) + EPS) * fw_ref[...]
    is_prompt = pl.program_id(0) < NP // tm

    @pl.when(is_prompt)
    def _():
        op_ref[...] = y

    @pl.when(jnp.logical_not(is_prompt))
    def _():
        os_ref[...] = y


def _ffn_layer(x, mods_l, nw, w_in, w_out, layer, final_w=None):
    tm = 512
    final = final_w is not None
    in_specs = [
        pl.BlockSpec((tm, D), lambda i: (i, 0)),
        pl.BlockSpec((None, N_MOD, D), lambda i: (_mod_group(i, tm), 0, 0)),
        _const_spec((1, D)),
        _layer_spec((D, 2 * D_FF), layer),
        _layer_spec((D_FF, D), layer),
    ]
    args = [x, mods_l, nw, w_in, w_out]
    if final:
        npt = NP // tm
        in_specs.append(_const_spec((1, D)))
        args.append(final_w)
        out_shape = [jax.ShapeDtypeStruct((NP, D), F32), jax.ShapeDtypeStruct((NS, D), F32)]
        out_specs = [pl.BlockSpec((tm, D), lambda i: (jnp.minimum(i, npt - 1), 0)),
                     pl.BlockSpec((tm, D), lambda i: (jnp.maximum(i - npt, 0), 0))]
    else:
        out_shape = jax.ShapeDtypeStruct((NT, D), F32)
        out_specs = pl.BlockSpec((tm, D), lambda i: (i, 0))
    return pl.pallas_call(
        functools.partial(_ffn_kernel, final=final, tm=tm),
        out_shape=out_shape,
        grid=(NT // tm,),
        in_specs=in_specs,
        out_specs=out_specs,
        scratch_shapes=[pltpu.VMEM((tm, D), BF16), pltpu.VMEM((tm, D), F32)],
        compiler_params=_params(56),
        name="ffn_final" if final else "ffn",
    )(*args)


def _rope_tables():
    t = np.arange(DEC_SEQ)
    inv_freq = ROPE_THETA ** (-np.arange(0, ROPE_AXIS_DIM, 2, dtype=np.float64) / ROPE_AXIS_DIM)
    ang_r = (t // GRID_W)[:, None] * inv_freq[None, :]
    ang_c = (t % GRID_W)[:, None] * inv_freq[None, :]
    cos = np.concatenate([np.cos(ang_r)] * 2 + [np.cos(ang_c)] * 2, axis=-1)
    sin = np.concatenate([-np.sin(ang_r), np.sin(ang_r), -np.sin(ang_c), np.sin(ang_c)], axis=-1)
    return jnp.asarray(cos, F32), jnp.asarray(sin, F32)


def kernel(x_prompt, x_sample, cache_k, cache_v, c, c_ctx, norm1_w, norm2_w, ada_w, ada_b,
           conv_in_w, conv_w, conv_out_w, pool_w, pool_scale, attn_qkv_w, q_norm_w, k_norm_w,
           attn_out_w, ffn_in_w, ffn_out_w, final_norm_w):
    cvec = jnp.concatenate([c_ctx[None, :], c, jnp.zeros((8 - N_GROUPS, D), F32)], axis=0)
    mods = _adaln(cvec, ada_w, ada_b)
    mods = mods[:, :N_GROUPS].reshape(DEPTH, N_GROUPS, N_MOD, D)

    pool_scale3 = pool_scale[:, None, :]
    qn3, kn3 = q_norm_w[:, None, :], k_norm_w[:, None, :]

    x = (x_prompt.reshape(NP, D), x_sample.reshape(NS, D))
    new_k = new_v = None
    for i in range(DEPTH):
        kind, j = i % 3, i // 3
        nw1 = norm1_w[i][None, :]
        if kind == 0:
            x = _conv_layer(x, mods[i], nw1, conv_in_w, conv_w, conv_out_w, j)
        elif kind == 1:
            x = _pool_layer(x, mods[i], nw1, pool_w, pool_scale3, j)
        else:
            (qp, kp, vtp, new_k, new_v), qkv_s = _qkv_layer(
                x, mods[i], nw1, attn_qkv_w, qn3, kn3, _rope_tables(), j)
            x = _attn_layer(x, (qp, kp, vtp), qkv_s, cache_k, cache_v, mods[i], attn_out_w, j)
        x = _ffn_layer(x, mods[i], norm2_w[i][None, :], ffn_in_w, ffn_out_w, i,
                       final_norm_w[None, :] if i == DEPTH - 1 else None)

    y_prompt, y_sample = x
    return (y_prompt.reshape(BATCH, SEQ, D), y_sample.reshape(DEC_BATCH, DEC_SEQ, D), new_k, new_v)
```

```python
import functools

import numpy as np
import jax
import jax.numpy as jnp
from jax import lax
from jax.experimental import pallas as pl
from jax.experimental.pallas import tpu as pltpu

D = 1024
BATCH = 16
SEQ = 256
DEPTH = 4
DEC_BATCH = 2
DEC_SEQ = 4096
PAST_LEN = 256
GRID_W = 64
N_HEADS = 8
N_KV_HEADS = 2
HEAD_DIM = 128
Q_PER_KV = N_HEADS // N_KV_HEADS
ROPE_AXIS_DIM = HEAD_DIM // 2
ROPE_THETA = 10000.0
POOL_WINDOWS = (2, 4, 8, 16)
POOL_GROUP_DIM = D // 4
D_FF = 2816
N_MOD = 6
EPS = 1e-6
ATTN_SCALE = HEAD_DIM ** -0.5

NP = BATCH * SEQ
NS = DEC_BATCH * DEC_SEQ
NT = NP + NS
N_GROUPS = 1 + DEC_BATCH

QK_SCALE_LOG2 = ATTN_SCALE * float(np.log2(np.e))
BF16_SUBLANES = 16
VEXT_ROWS = HEAD_DIM + BF16_SUBLANES
HALO = BF16_SUBLANES
FF_CHUNK = 256
BF16 = jnp.bfloat16
F32 = jnp.float32


def _dot(a, b):
    return jnp.dot(a, b, preferred_element_type=F32)


def _wdot(a, w):
    return _dot(a, w.astype(BF16))


def _dot_nt(a, b):
    return lax.dot_general(a, b, (((1,), (1,)), ((), ())), preferred_element_type=F32)


def _norm_mod(x, nw, shift, scale):
    y = x * lax.rsqrt(jnp.mean(x * x, axis=-1, keepdims=True) + EPS)
    return (y * nw) * (1.0 + scale) + shift


def _mod_group(tile, tm):
    return jnp.where(tile < NP // tm, 0, 1 + (tile - NP // tm) // (DEC_SEQ // tm))


def _params(vmem_mb, n_axes=1):
    return pltpu.CompilerParams(
        dimension_semantics=("arbitrary",) * n_axes,
        vmem_limit_bytes=vmem_mb * 1024 * 1024)


def _const_spec(shape):
    return pl.BlockSpec(shape, lambda *_: (0,) * len(shape), pipeline_mode=pl.Buffered(1))


def _layer_spec(shape, j):
    return pl.BlockSpec((None,) + tuple(shape), lambda *_: (j,) + (0,) * len(shape),
                        pipeline_mode=pl.Buffered(1))


def _adaln_kernel(c_ref, w_ref, b_ref, o_ref):
    c = c_ref[...]
    sc = (c * jax.nn.sigmoid(c)).astype(BF16)
    o_ref[...] = _dot(sc, w_ref[...].astype(BF16)) + b_ref[...]


def _adaln(cvec, ada_w, ada_b):
    tn = 1536
    return pl.pallas_call(
        _adaln_kernel,
        out_shape=jax.ShapeDtypeStruct((DEPTH, 8, N_MOD * D), F32),
        grid=(DEPTH, N_MOD * D // tn),
        in_specs=[
            pl.BlockSpec((8, D), lambda l, j: (0, 0)),
            pl.BlockSpec((None, D, tn), lambda l, j: (l, 0, j)),
            pl.BlockSpec((None, 1, tn), lambda l, j: (l, 0, j)),
        ],
        out_specs=pl.BlockSpec((None, 8, tn), lambda l, j: (l, 0, j)),
        compiler_params=_params(40, 2),
        name="adaln",
    )(cvec, ada_w, ada_b.reshape(DEPTH, 1, N_MOD * D))


def _conv_kernel(*refs, tm, split):
    i = pl.program_id(0)
    if split:
        (pm_ref, pp_ref, pn_ref, sm_ref, sp_ref, sn_ref,
         mod_ref, nw_ref, win_ref, cw_ref, wout_ref, o_ref, h_ref, acc_ref) = refs
        is_prompt = i < NP // tm
        x = jnp.where(is_prompt, pm_ref[...], sm_ref[...])
        x_prev = jnp.where(is_prompt, pp_ref[...], sp_ref[...])
        x_next = jnp.where(is_prompt, pn_ref[...], sn_ref[...])
    else:
        (x_ref, xp_ref, xn_ref,
         mod_ref, nw_ref, win_ref, cw_ref, wout_ref, o_ref, h_ref, acc_ref) = refs
        x, x_prev, x_next = x_ref[...], xp_ref[...], xn_ref[...]
    mod = mod_ref[...]
    nw = nw_ref[...]
    shift, scale, gate = mod[0:1], mod[1:2], mod[2:3]
    h_ref[0:HALO] = _norm_mod(x_prev, nw, shift, scale).astype(BF16)
    h_ref[HALO:HALO + tm] = _norm_mod(x, nw, shift, scale).astype(BF16)
    h_ref[HALO + tm:] = _norm_mod(x_next, nw, shift, scale).astype(BF16)

    seq_len = jnp.where(i < NP // tm, SEQ, DEC_SEQ)
    pos = (i * tm + lax.broadcasted_iota(jnp.int32, (tm, 1), 0)) & (seq_len - 1)
    has_prev = pos != 0
    has_next = pos != seq_len - 1
    cw = cw_ref[...]
    rows = tm + 2 * HALO
    def in_proj(j):
        c0 = j * 256
        return (_wdot(h_ref[HALO:HALO + tm], win_ref[:, c0:c0 + 256]),
                _wdot(h_ref[...], win_ref[:, D + c0:D + c0 + 256]),
                _wdot(h_ref[...], win_ref[:, 2 * D + c0:2 * D + c0 + 256]))

    nxt = in_proj(0)
    for j in range(D // 256):
        c0 = j * 256
        b, cg, u = nxt
        if j + 1 < D // 256:
            nxt = in_proj(j + 1)
        z = cg * u
        z_prev = pltpu.roll(z, 1, axis=0)[HALO:HALO + tm]
        z_next = pltpu.roll(z, rows - 1, axis=0)[HALO:HALO + tm]
        conv = (jnp.where(has_prev, z_prev, 0.0) * cw[0:1, c0:c0 + 256]
                + z[HALO:HALO + tm] * cw[1:2, c0:c0 + 256]
                + jnp.where(has_next, z_next, 0.0) * cw[2:3, c0:c0 + 256])
        part = _wdot((b * conv).astype(BF16), wout_ref[c0:c0 + 256, :])
        if j == 0:
            acc_ref[...] = part
        else:
            acc_ref[...] += part
    o_ref[...] = x + gate * acc_ref[...]


def _halo_specs(tm, nrows, tile_of):
    nh = tm // HALO
    last = nrows // HALO - 1
    return [
        pl.BlockSpec((tm, D), lambda i: (tile_of(i), 0)),
        pl.BlockSpec((HALO, D), lambda i: (jnp.maximum(tile_of(i) * nh - 1, 0), 0)),
        pl.BlockSpec((HALO, D), lambda i: (jnp.minimum((tile_of(i) + 1) * nh, last), 0)),
    ]


def _conv_layer(xs, mods_l, nw, w_in, cw, w_out, j):
    tm = 512
    split = isinstance(xs, tuple)
    if split:
        npt = NP // tm
        x_specs = (_halo_specs(tm, NP, lambda i: jnp.minimum(i, npt - 1))
                   + _halo_specs(tm, NS, lambda i: jnp.maximum(i - npt, 0)))
        x_args = [xs[0]] * 3 + [xs[1]] * 3
    else:
        x_specs = _halo_specs(tm, NT, lambda i: i)
        x_args = [xs] * 3
    return pl.pallas_call(
        functools.partial(_conv_kernel, tm=tm, split=split),
        out_shape=jax.ShapeDtypeStruct((NT, D), F32),
        grid=(NT // tm,),
        in_specs=x_specs + [
            pl.BlockSpec((None, N_MOD, D), lambda i: (_mod_group(i, tm), 0, 0)),
            _const_spec((1, D)),
            _layer_spec((D, 3 * D), j),
            _layer_spec((3, D), j),
            _layer_spec((D, D), j),
        ],
        out_specs=pl.BlockSpec((tm, D), lambda i: (i, 0)),
        scratch_shapes=[pltpu.VMEM((tm + 2 * HALO, D), BF16), pltpu.VMEM((tm, D), F32)],
        compiler_params=_params(48),
        name="conv_mixer",
    )(*x_args, mods_l, nw, w_in, cw, w_out)


def _pool_kernel(x_ref, xp_ref, xn_ref, mod_ref, nw_ref, wp_ref, ps_ref, o_ref, h_ref, *, tm):
    i = pl.program_id(0)
    mod = mod_ref[...]
    nw = nw_ref[...]
    shift, scale, gate = mod[0:1], mod[1:2], mod[2:3]
    x = x_ref[...]
    seq_len = jnp.where(i < NP // tm, SEQ, DEC_SEQ)
    pos0 = (i * tm) & (seq_len - 1)
    at_start = pos0 == 0
    at_end = pos0 + tm == seq_len
    h_ref[0:HALO] = jnp.where(at_start, 0.0, _norm_mod(xp_ref[...], nw, shift, scale))
    h_ref[HALO:HALO + tm] = _norm_mod(x, nw, shift, scale)
    h_ref[HALO + tm:] = jnp.where(at_end, 0.0, _norm_mod(xn_ref[...], nw, shift, scale))

    rows = tm + 2 * HALO
    pos = pos0 + lax.broadcasted_iota(jnp.int32, (tm, 1), 0)
    ps = ps_ref[...]
    for g, w in enumerate(POOL_WINDOWS):
        c0 = g * POOL_GROUP_DIM
        hg = h_ref[:, c0:c0 + POOL_GROUP_DIM]
        p = hg + pltpu.roll(hg, 1, axis=0)
        step = 1
        while 2 * step < w:
            p = pltpu.roll(p, step, axis=0) + pltpu.roll(p, rows - step, axis=0)
            step *= 2
        cnt = (jnp.minimum(pos + w // 2, seq_len) - jnp.maximum(pos - w // 2, 0)).astype(F32)
        pooled = p[HALO:HALO + tm] / cnt
        diff = (pooled - hg[HALO:HALO + tm]).astype(BF16)
        mixed = _wdot(diff, wp_ref[g]) * ps[:, c0:c0 + POOL_GROUP_DIM]
        o_ref[:, c0:c0 + POOL_GROUP_DIM] = (x[:, c0:c0 + POOL_GROUP_DIM]
                                            + gate[:, c0:c0 + POOL_GROUP_DIM] * mixed)


def _pool_layer(x, mods_l, nw, wp, ps, j):
    tm = 256
    return pl.pallas_call(
        functools.partial(_pool_kernel, tm=tm),
        out_shape=jax.ShapeDtypeStruct((NT, D), F32),
        grid=(NT // tm,),
        in_specs=_halo_specs(tm, NT, lambda i: i) + [
            pl.BlockSpec((None, N_MOD, D), lambda i: (_mod_group(i, tm), 0, 0)),
            _const_spec((1, D)),
            _layer_spec((4, POOL_GROUP_DIM, POOL_GROUP_DIM), j),
            _layer_spec((1, D), j),
        ],
        out_specs=pl.BlockSpec((tm, D), lambda i: (i, 0)),
        scratch_shapes=[pltpu.VMEM((tm + 2 * HALO, D), F32)],
        compiler_params=_params(32),
        name="pool_mixer",
    )(x, x, x, mods_l, nw, wp, ps)


def _head_norm(t, w):
    return t * lax.rsqrt(jnp.mean(t * t, axis=-1, keepdims=True) + EPS) * w


def _rope(t, cos, sin_signed, first_half):
    rot = jnp.where(first_half, pltpu.roll(t, HEAD_DIM - 32, axis=1), pltpu.roll(t, 32, axis=1))
    return t * cos + rot * sin_signed


def _qkv_kernel(*refs, tm, rope):
    if rope:
        (x_ref, mod_ref, nw_ref, w_ref, qn_ref, kn_ref, cos_ref, sin_ref,
         q_ref, k_ref, vt_ref) = refs
    else:
        (x_ref, mod_ref, nw_ref, w_ref, qn_ref, kn_ref,
         q_ref, k_ref, vt_ref, nk_ref, nv_ref) = refs
    mod = mod_ref[...]
    h = _norm_mod(x_ref[...], nw_ref[...], mod[0:1], mod[1:2]).astype(BF16)
    qkv = _wdot(h, w_ref[...])
    qn = qn_ref[...]
    kn = kn_ref[...]
    if rope:
        cos = cos_ref[...]
        sin = sin_ref[...]
        first_half = (lax.broadcasted_iota(jnp.int32, (1, HEAD_DIM), 1) & 63) < 32
    for hd in range(N_HEADS):
        t = _head_norm(qkv[:, hd * HEAD_DIM:(hd + 1) * HEAD_DIM], qn)
        if rope:
            t = _rope(t, cos, sin, first_half)
        q_ref[:, hd * HEAD_DIM:(hd + 1) * HEAD_DIM] = (t * QK_SCALE_LOG2).astype(BF16)
    for g in range(N_KV_HEADS):
        ko = (N_HEADS + g) * HEAD_DIM
        vo = (N_HEADS + N_KV_HEADS + g) * HEAD_DIM
        t = _head_norm(qkv[:, ko:ko + HEAD_DIM], kn)
        vv = qkv[:, vo:vo + HEAD_DIM]
        if rope:
            t = _rope(t, cos, sin, first_half)
            vt_ref[g] = vv.T.astype(BF16)
        else:
            for bb in range(tm // SEQ):
                nk_ref[bb, 0, g] = t[bb * SEQ:(bb + 1) * SEQ]
                nv_ref[bb, 0, g] = vv[bb * SEQ:(bb + 1) * SEQ]
                vt_ref[bb, g] = vv[bb * SEQ:(bb + 1) * SEQ].T.astype(BF16)
        k_ref[:, g * HEAD_DIM:(g + 1) * HEAD_DIM] = t.astype(BF16)


def _qkv_layer(x, mods_l, nw, w_qkv, qn, kn, rope_tabs, j):
    tm = 512
    kvd = N_KV_HEADS * HEAD_DIM
    qkv_out = (N_HEADS + 2 * N_KV_HEADS) * HEAD_DIM

    def call(rope, row0, nrows):
        t0 = row0 // tm
        in_specs = [
            pl.BlockSpec((tm, D), lambda i: (t0 + i, 0)),
            pl.BlockSpec((None, N_MOD, D), lambda i: (_mod_group(t0 + i, tm), 0, 0)),
            _const_spec((1, D)),
            _layer_spec((D, qkv_out), j),
            _layer_spec((1, HEAD_DIM), j),
            _layer_spec((1, HEAD_DIM), j),
        ]
        out_shape = [jax.ShapeDtypeStruct((nrows, D), BF16),
                     jax.ShapeDtypeStruct((nrows, kvd), BF16)]
        out_specs = [pl.BlockSpec((tm, D), lambda i: (i, 0)),
                     pl.BlockSpec((tm, kvd), lambda i: (i, 0))]
        args = [x, mods_l, nw, w_qkv, qn, kn]
        if rope:
            tps = DEC_SEQ // tm
            in_specs += [pl.BlockSpec((tm, HEAD_DIM), lambda i: (i % tps, 0)),
                         pl.BlockSpec((tm, HEAD_DIM), lambda i: (i % tps, 0))]
            args += list(rope_tabs)
            out_shape.append(jax.ShapeDtypeStruct((DEC_BATCH, N_KV_HEADS, HEAD_DIM, DEC_SEQ), BF16))
            out_specs.append(pl.BlockSpec((None, N_KV_HEADS, HEAD_DIM, tm),
                                          lambda i: (i // tps, 0, 0, i % tps)))
        else:
            nb = tm // SEQ
            out_shape.append(jax.ShapeDtypeStruct((BATCH, N_KV_HEADS, HEAD_DIM, SEQ), BF16))
            out_specs.append(pl.BlockSpec((nb, N_KV_HEADS, HEAD_DIM, SEQ), lambda i: (i, 0, 0, 0)))
            kv_shape = (BATCH, 1, N_KV_HEADS, SEQ, HEAD_DIM)
            out_shape += [jax.ShapeDtypeStruct(kv_shape, F32)] * 2
            out_specs += [pl.BlockSpec((nb, 1, N_KV_HEADS, SEQ, HEAD_DIM),
                                       lambda i: (i, 0, 0, 0, 0))] * 2
        return pl.pallas_call(
            functools.partial(_qkv_kernel, tm=tm, rope=rope),
            out_shape=out_shape,
            grid=(nrows // tm,),
            in_specs=in_specs,
            out_specs=out_specs,
            compiler_params=_params(40),
            name="qkv_rope" if rope else "qkv_ctx",
        )(*args)

    return call(False, 0, NP), call(True, NP, NS)


def _attn_kernel(*refs, past, hpb):
    if past:
        (x_ref, q_ref, k_ref, vt_ref, ck_ref, cv_ref, mod_ref, wo_ref,
         o_ref, kall_ref, vext_ref, heads_ref) = refs
    else:
        (x_ref, q_ref, k_ref, vt_ref, mod_ref, wo_ref,
         o_ref, kall_ref, vext_ref, heads_ref) = refs
    nkeys = kall_ref.shape[1]

    @pl.when(pl.program_id(1) == 0)
    def _():
        for g in range(N_KV_HEADS):
            if past:
                kall_ref[g, 0:past] = ck_ref[0, 0, g].astype(BF16)
                vext_ref[g, 0:HEAD_DIM, 0:past] = cv_ref[0, 0, g].T.astype(BF16)
            kall_ref[g, past:] = k_ref[:, g * HEAD_DIM:(g + 1) * HEAD_DIM]
            vext_ref[g, 0:HEAD_DIM, past:] = vt_ref[g]
            vext_ref[g, HEAD_DIM:] = jnp.ones((VEXT_ROWS - HEAD_DIM, nkeys), BF16)

    tq = q_ref.shape[0]
    blocks = [list(range(h0, h0 + hpb)) for h0 in range(0, N_HEADS, hpb)]

    def scores(block):
        qb = [q_ref[:, hd * HEAD_DIM:(hd + 1) * HEAD_DIM] for hd in block]
        qb = qb[0] if hpb == 1 else jnp.concatenate(qb, axis=0)
        return _dot_nt(kall_ref[block[0] // Q_PER_KV], qb)

    st_next = scores(blocks[0])
    for b, block in enumerate(blocks):
        st = st_next
        if b + 1 < len(blocks):
            st_next = scores(blocks[b + 1])
        m = jnp.max(st, axis=0, keepdims=True)
        pt = jnp.exp2(st - m).astype(BF16)
        ot = _dot(vext_ref[block[0] // Q_PER_KV], pt)
        o = (ot[0:HEAD_DIM] / ot[HEAD_DIM:HEAD_DIM + 1]).T
        for r, hd in enumerate(block):
            heads_ref[:, hd * HEAD_DIM:(hd + 1) * HEAD_DIM] = o[r * tq:(r + 1) * tq].astype(BF16)
    gate = mod_ref[...][2:3]
    o_ref[...] = x_ref[...] + gate * _wdot(heads_ref[...], wo_ref[...])


def _attn_layer(x, qkv_p, qkv_s, cache_k, cache_v, mods_l, w_out, j):
    kvd = N_KV_HEADS * HEAD_DIM

    def call(x, q, k, vt, past, row0, nb, seq, tq, hpb):
        t0 = row0 // tq
        nq = seq // tq
        in_specs = [
            pl.BlockSpec((tq, D), lambda b, i: (t0 + b * nq + i, 0)),
            pl.BlockSpec((tq, D), lambda b, i: (b * nq + i, 0)),
            pl.BlockSpec((seq, kvd), lambda b, i: (b, 0)),
            pl.BlockSpec((None, N_KV_HEADS, HEAD_DIM, seq), lambda b, i: (b, 0, 0, 0)),
        ]
        args = [x, q, k, vt]
        if past:
            cspec = pl.BlockSpec((1, 1, N_KV_HEADS, PAST_LEN, HEAD_DIM),
                                 lambda b, i: (b, j, 0, 0, 0))
            in_specs += [cspec, cspec]
            args += [cache_k, cache_v]
        in_specs += [
            pl.BlockSpec((None, N_MOD, D), lambda b, i: (_mod_group(t0 + b * nq + i, tq), 0, 0)),
            _layer_spec((D, D), j),
        ]
        args += [mods_l, w_out]
        return pl.pallas_call(
            functools.partial(_attn_kernel, past=past, hpb=hpb),
            out_shape=jax.ShapeDtypeStruct((NT, D), F32),
            grid=(nb, nq),
            in_specs=in_specs,
            out_specs=pl.BlockSpec((tq, D), lambda b, i: (t0 + b * nq + i, 0)),
            scratch_shapes=[pltpu.VMEM((N_KV_HEADS, past + seq, HEAD_DIM), BF16),
                            pltpu.VMEM((N_KV_HEADS, VEXT_ROWS, past + seq), BF16),
                            pltpu.VMEM((tq, D), BF16)],
            input_output_aliases={0: 0},
            compiler_params=_params(48, 2),
            name="attn_cached" if past else "attn_ctx",
        )(*args)

    x = call(x, *qkv_p, 0, 0, BATCH, SEQ, SEQ, Q_PER_KV)
    return call(x, *qkv_s, PAST_LEN, NP, DEC_BATCH, DEC_SEQ, 512, 1)


def _ffn_kernel(*refs, final, tm):
    if final:
        x_ref, mod_ref, nw_ref, win_ref, wout_ref, fw_ref, op_ref, os_ref, h_ref, acc_ref = refs
    else:
        x_ref, mod_ref, nw_ref, win_ref, wout_ref, o_ref, h_ref, acc_ref = refs
    mod = mod_ref[...]
    x = x_ref[...]
    h_ref[...] = _norm_mod(x, nw_ref[...], mod[3:4], mod[4:5]).astype(BF16)
    def in_proj(j):
        c0 = j * FF_CHUNK
        return (_wdot(h_ref[...], win_ref[:, c0:c0 + FF_CHUNK]),
                _wdot(h_ref[...], win_ref[:, D_FF + c0:D_FF + c0 + FF_CHUNK]))

    nxt = in_proj(0)
    for j in range(D_FF // FF_CHUNK):
        c0 = j * FF_CHUNK
        g, u = nxt
        if j + 1 < D_FF // FF_CHUNK:
            nxt = in_proj(j + 1)
        a = (g * jax.nn.sigmoid(g) * u).astype(BF16)
        part = _wdot(a, wout_ref[c0:c0 + FF_CHUNK, :])
        if j == 0:
            acc_ref[...] = part
        else:
            acc_ref[...] += part
    y = x + mod[5:6] * acc_ref[...]
    if not final:
        o_ref[...] = y
        return
    y = y * lax.rsqrt(jnp.mean(y * y, axis=-1, keepdims=True) + EPS) * fw_ref[...]
    is_prompt = pl.program_id(0) < NP // tm

    @pl.when(is_prompt)
    def _():
        op_ref[...] = y

    @pl.when(jnp.logical_not(is_prompt))
    def _():
        os_ref[...] = y


def _ffn_layer(x, mods_l, nw, w_in, w_out, layer, final_w=None):
    tm = 512
    final = final_w is not None
    in_specs = [
        pl.BlockSpec((tm, D), lambda i: (i, 0)),
        pl.BlockSpec((None, N_MOD, D), lambda i: (_mod_group(i, tm), 0, 0)),
        _const_spec((1, D)),
        _layer_spec((D, 2 * D_FF), layer),
        _layer_spec((D_FF, D), layer),
    ]
    args = [x, mods_l, nw, w_in, w_out]
    if final:
        npt = NP // tm
        in_specs.append(_const_spec((1, D)))
        args.append(final_w)
        out_shape = [jax.ShapeDtypeStruct((NP, D), F32), jax.ShapeDtypeStruct((NS, D), F32)]
        out_specs = [pl.BlockSpec((tm, D), lambda i: (jnp.minimum(i, npt - 1), 0)),
                     pl.BlockSpec((tm, D), lambda i: (jnp.maximum(i - npt, 0), 0))]
    else:
        out_shape = jax.ShapeDtypeStruct((NT, D), F32)
        out_specs = pl.BlockSpec((tm, D), lambda i: (i, 0))
    return pl.pallas_call(
        functools.partial(_ffn_kernel, final=final, tm=tm),
        out_shape=out_shape,
        grid=(NT // tm,),
        in_specs=in_specs,
        out_specs=out_specs,
        scratch_shapes=[pltpu.VMEM((tm, D), BF16), pltpu.VMEM((tm, D), F32)],
        compiler_params=_params(56),
        name="ffn_final" if final else "ffn",
    )(*args)


def _rope_tables():
    t = np.arange(DEC_SEQ)
    inv_freq = ROPE_THETA ** (-np.arange(0, ROPE_AXIS_DIM, 2, dtype=np.float64) / ROPE_AXIS_DIM)
    ang_r = (t // GRID_W)[:, None] * inv_freq[None, :]
    ang_c = (t % GRID_W)[:, None] * inv_freq[None, :]
    cos = np.concatenate([np.cos(ang_r)] * 2 + [np.cos(ang_c)] * 2, axis=-1)
    sin = np.concatenate([-np.sin(ang_r), np.sin(ang_r), -np.sin(ang_c), np.sin(ang_c)], axis=-1)
    return jnp.asarray(cos, F32), jnp.asarray(sin, F32)


def kernel(x_prompt, x_sample, cache_k, cache_v, c, c_ctx, norm1_w, norm2_w, ada_w, ada_b,
           conv_in_w, conv_w, conv_out_w, pool_w, pool_scale, attn_qkv_w, q_norm_w, k_norm_w,
           attn_out_w, ffn_in_w, ffn_out_w, final_norm_w):
    cvec = jnp.concatenate([c_ctx[None, :], c, jnp.zeros((8 - N_GROUPS, D), F32)], axis=0)
    mods = _adaln(cvec, ada_w, ada_b)
    mods = mods[:, :N_GROUPS].reshape(DEPTH, N_GROUPS, N_MOD, D)

    pool_scale3 = pool_scale[:, None, :]
    qn3, kn3 = q_norm_w[:, None, :], k_norm_w[:, None, :]

    x = (x_prompt.reshape(NP, D), x_sample.reshape(NS, D))
    new_k = new_v = None
    for i in range(DEPTH):
        kind, j = i % 3, i // 3
        nw1 = norm1_w[i][None, :]
        if kind == 0:
            x = _conv_layer(x, mods[i], nw1, conv_in_w, conv_w, conv_out_w, j)
        elif kind == 1:
            x = _pool_layer(x, mods[i], nw1, pool_w, pool_scale3, j)
        else:
            (qp, kp, vtp, new_k, new_v), qkv_s = _qkv_layer(
                x, mods[i], nw1, attn_qkv_w, qn3, kn3, _rope_tables(), j)
            x = _attn_layer(x, (qp, kp, vtp), qkv_s, cache_k, cache_v, mods[i], attn_out_w, j)
        x = _ffn_layer(x, mods[i], norm2_w[i][None, :], ffn_in_w, ffn_out_w, i,
                       final_norm_w[None, :] if i == DEPTH - 1 else None)

    y_prompt, y_sample = x
    return (y_prompt.reshape(BATCH, SEQ, D), y_sample.reshape(DEC_BATCH, DEC_SEQ, D), new_k, new_v)
```

```python
import functools

import numpy as np
import jax
import jax.numpy as jnp
from jax import lax
from jax.experimental import pallas as pl
from jax.experimental.pallas import tpu as pltpu

D = 1024
BATCH = 16
SEQ = 256
DEPTH = 4
DEC_BATCH = 2
DEC_SEQ = 4096
PAST_LEN = 256
GRID_W = 64
N_HEADS = 8
N_KV_HEADS = 2
HEAD_DIM = 128
Q_PER_KV = N_HEADS // N_KV_HEADS
ROPE_AXIS_DIM = HEAD_DIM // 2
ROPE_THETA = 10000.0
POOL_WINDOWS = (2, 4, 8, 16)
POOL_GROUP_DIM = D // 4
D_FF = 2816
N_MOD = 6
EPS = 1e-6
ATTN_SCALE = HEAD_DIM ** -0.5

NP = BATCH * SEQ
NS = DEC_BATCH * DEC_SEQ
NT = NP + NS
N_GROUPS = 1 + DEC_BATCH

QK_SCALE_LOG2 = ATTN_SCALE * float(np.log2(np.e))
MAX_SAFE_SCORE_BOUND = 60.0
BF16_SUBLANES = 16
VEXT_ROWS = HEAD_DIM + BF16_SUBLANES
HALO = BF16_SUBLANES
FF_CHUNK = 256
BF16 = jnp.bfloat16
F32 = jnp.float32


def _dot(a, b):
    return jnp.dot(a, b, preferred_element_type=F32)


def _wdot(a, w):
    return _dot(a, w.astype(BF16))


def _dot_nt(a, b):
    return lax.dot_general(a, b, (((1,), (1,)), ((), ())), preferred_element_type=F32)


def _norm_mod(x, nw, shift, scale):
    y = x * lax.rsqrt(jnp.mean(x * x, axis=-1, keepdims=True) + EPS)
    return (y * nw) * (1.0 + scale) + shift


def _mod_group(tile, tm):
    return jnp.where(tile < NP // tm, 0, 1 + (tile - NP // tm) // (DEC_SEQ // tm))


def _params(vmem_mb, n_axes=1):
    return pltpu.CompilerParams(
        dimension_semantics=("arbitrary",) * n_axes,
        vmem_limit_bytes=vmem_mb * 1024 * 1024)


def _const_spec(shape):
    return pl.BlockSpec(shape, lambda *_: (0,) * len(shape), pipeline_mode=pl.Buffered(1))


def _layer_spec(shape, j):
    return pl.BlockSpec((None,) + tuple(shape), lambda *_: (j,) + (0,) * len(shape),
                        pipeline_mode=pl.Buffered(1))


def _adaln_kernel(c_ref, w_ref, b_ref, o_ref):
    c = c_ref[...]
    sc = (c * jax.nn.sigmoid(c)).astype(BF16)
    o_ref[...] = _dot(sc, w_ref[...].astype(BF16)) + b_ref[...]


def _adaln(cvec, ada_w, ada_b):
    tn = 1536
    return pl.pallas_call(
        _adaln_kernel,
        out_shape=jax.ShapeDtypeStruct((DEPTH, 8, N_MOD * D), F32),
        grid=(DEPTH, N_MOD * D // tn),
        in_specs=[
            pl.BlockSpec((8, D), lambda l, j: (0, 0)),
            pl.BlockSpec((None, D, tn), lambda l, j: (l, 0, j)),
            pl.BlockSpec((None, 1, tn), lambda l, j: (l, 0, j)),
        ],
        out_specs=pl.BlockSpec((None, 8, tn), lambda l, j: (l, 0, j)),
        compiler_params=_params(40, 2),
        name="adaln",
    )(cvec, ada_w, ada_b.reshape(DEPTH, 1, N_MOD * D))


def _conv_kernel(*refs, tm, split):
    i = pl.program_id(0)
    if split:
        (pm_ref, pp_ref, pn_ref, sm_ref, sp_ref, sn_ref,
         mod_ref, nw_ref, win_ref, cw_ref, wout_ref, o_ref, h_ref, acc_ref) = refs
        is_prompt = i < NP // tm
        x = jnp.where(is_prompt, pm_ref[...], sm_ref[...])
        x_prev = jnp.where(is_prompt, pp_ref[...], sp_ref[...])
        x_next = jnp.where(is_prompt, pn_ref[...], sn_ref[...])
    else:
        (x_ref, xp_ref, xn_ref,
         mod_ref, nw_ref, win_ref, cw_ref, wout_ref, o_ref, h_ref, acc_ref) = refs
        x, x_prev, x_next = x_ref[...], xp_ref[...], xn_ref[...]
    mod = mod_ref[...]
    nw = nw_ref[...]
    shift, scale, gate = mod[0:1], mod[1:2], mod[2:3]
    h_ref[0:HALO] = _norm_mod(x_prev, nw, shift, scale).astype(BF16)
    h_ref[HALO:HALO + tm] = _norm_mod(x, nw, shift, scale).astype(BF16)
    h_ref[HALO + tm:] = _norm_mod(x_next, nw, shift, scale).astype(BF16)

    seq_len = jnp.where(i < NP // tm, SEQ, DEC_SEQ)
    pos = (i * tm + lax.broadcasted_iota(jnp.int32, (tm, 1), 0)) & (seq_len - 1)
    has_prev = pos != 0
    has_next = pos != seq_len - 1
    cw = cw_ref[...]
    rows = tm + 2 * HALO
    def in_proj(j):
        c0 = j * 256
        return (_wdot(h_ref[HALO:HALO + tm], win_ref[:, c0:c0 + 256]),
                _wdot(h_ref[...], win_ref[:, D + c0:D + c0 + 256]),
                _wdot(h_ref[...], win_ref[:, 2 * D + c0:2 * D + c0 + 256]))

    nxt = in_proj(0)
    for j in range(D // 256):
        c0 = j * 256
        b, cg, u = nxt
        if j + 1 < D // 256:
            nxt = in_proj(j + 1)
        z = cg * u
        z_prev = pltpu.roll(z, 1, axis=0)[HALO:HALO + tm]
        z_next = pltpu.roll(z, rows - 1, axis=0)[HALO:HALO + tm]
        conv = (jnp.where(has_prev, z_prev, 0.0) * cw[0:1, c0:c0 + 256]
                + z[HALO:HALO + tm] * cw[1:2, c0:c0 + 256]
                + jnp.where(has_next, z_next, 0.0) * cw[2:3, c0:c0 + 256])
        part = _wdot((b * conv).astype(BF16), wout_ref[c0:c0 + 256, :])
        if j == 0:
            acc_ref[...] = part
        else:
            acc_ref[...] += part
    o_ref[...] = x + gate * acc_ref[...]


def _halo_specs(tm, nrows, tile_of):
    nh = tm // HALO
    last = nrows // HALO - 1
    return [
        pl.BlockSpec((tm, D), lambda i: (tile_of(i), 0)),
        pl.BlockSpec((HALO, D), lambda i: (jnp.maximum(tile_of(i) * nh - 1, 0), 0)),
        pl.BlockSpec((HALO, D), lambda i: (jnp.minimum((tile_of(i) + 1) * nh, last), 0)),
    ]


def _conv_layer(xs, mods_l, nw, w_in, cw, w_out, j):
    tm = 512
    split = isinstance(xs, tuple)
    if split:
        npt = NP // tm
        x_specs = (_halo_specs(tm, NP, lambda i: jnp.minimum(i, npt - 1))
                   + _halo_specs(tm, NS, lambda i: jnp.maximum(i - npt, 0)))
        x_args = [xs[0]] * 3 + [xs[1]] * 3
    else:
        x_specs = _halo_specs(tm, NT, lambda i: i)
        x_args = [xs] * 3
    return pl.pallas_call(
        functools.partial(_conv_kernel, tm=tm, split=split),
        out_shape=jax.ShapeDtypeStruct((NT, D), F32),
        grid=(NT // tm,),
        in_specs=x_specs + [
            pl.BlockSpec((None, N_MOD, D), lambda i: (_mod_group(i, tm), 0, 0)),
            _const_spec((1, D)),
            _layer_spec((D, 3 * D), j),
            _layer_spec((3, D), j),
            _layer_spec((D, D), j),
        ],
        out_specs=pl.BlockSpec((tm, D), lambda i: (i, 0)),
        scratch_shapes=[pltpu.VMEM((tm + 2 * HALO, D), BF16), pltpu.VMEM((tm, D), F32)],
        compiler_params=_params(48),
        name="conv_mixer",
    )(*x_args, mods_l, nw, w_in, cw, w_out)


def _swiglu(h, win_ref, wout_ref, acc_ref, side_work=()):
    def in_proj(j):
        c0 = j * FF_CHUNK
        return (_wdot(h, win_ref[:, c0:c0 + FF_CHUNK]),
                _wdot(h, win_ref[:, D_FF + c0:D_FF + c0 + FF_CHUNK]))

    n_chunks = D_FF // FF_CHUNK
    assert len(side_work) <= n_chunks
    nxt = in_proj(0)
    for j in range(n_chunks):
        c0 = j * FF_CHUNK
        g, u = nxt
        if j + 1 < n_chunks:
            nxt = in_proj(j + 1)
        if j < len(side_work):
            side_work[j]()
        a = (g * jax.nn.sigmoid(g) * u).astype(BF16)
        part = _wdot(a, wout_ref[c0:c0 + FF_CHUNK, :])
        if j == 0:
            acc_ref[...] = part
        else:
            acc_ref[...] += part


def _pool_ffn_kernel(x_ref, xp_ref, xn_ref, modn_ref, modc_ref, nw1_ref, wp_ref, ps_ref, nw2_ref,
                     win_ref, wout_ref, o_ref, hp_ref, x1_ref, h2_ref, acc_ref, *, tm, n_tiles):
    s = pl.program_id(0)
    cur = (s + 1) % 2
    new = s % 2

    @pl.when(s == 0)
    def _():
        x1_ref[1] = jnp.zeros((tm, D), F32)
        h2_ref[1] = jnp.zeros((tm, D), BF16)

    t = jnp.minimum(s, n_tiles - 1)
    mod = modn_ref[...]
    nw = nw1_ref[...]
    shift, scale, gate = mod[0:1], mod[1:2], mod[2:3]
    seq_len = jnp.where(t < NP // tm, SEQ, DEC_SEQ)
    pos0 = (t * tm) & (seq_len - 1)
    rows = tm + 2 * HALO

    def fill_halo_slab():
        at_start = pos0 == 0
        at_end = pos0 + tm == seq_len
        hp_ref[0:HALO] = jnp.where(at_start, 0.0, _norm_mod(xp_ref[...], nw, shift, scale))
        hp_ref[HALO:HALO + tm] = _norm_mod(x_ref[...], nw, shift, scale)
        hp_ref[HALO + tm:] = jnp.where(at_end, 0.0, _norm_mod(xn_ref[...], nw, shift, scale))

    def pool_group(g):
        w = POOL_WINDOWS[g]
        c0 = g * POOL_GROUP_DIM
        pos = pos0 + lax.broadcasted_iota(jnp.int32, (tm, 1), 0)
        hg = hp_ref[:, c0:c0 + POOL_GROUP_DIM]
        p = hg + pltpu.roll(hg, 1, axis=0)
        step = 1
        while 2 * step < w:
            p = pltpu.roll(p, step, axis=0) + pltpu.roll(p, rows - step, axis=0)
            step *= 2
        cnt = (jnp.minimum(pos + w // 2, seq_len) - jnp.maximum(pos - w // 2, 0)).astype(F32)
        pooled = p[HALO:HALO + tm] / cnt
        diff = (pooled - hg[HALO:HALO + tm]).astype(BF16)
        mixed = _wdot(diff, wp_ref[g]) * ps_ref[...][:, c0:c0 + POOL_GROUP_DIM]
        x1_ref[new, :, c0:c0 + POOL_GROUP_DIM] = (
            x_ref[:, c0:c0 + POOL_GROUP_DIM] + gate[:, c0:c0 + POOL_GROUP_DIM] * mixed)

    def norm_for_swiglu():
        h2_ref[new] = _norm_mod(x1_ref[new], nw2_ref[...], mod[3:4], mod[4:5]).astype(BF16)

    side_work = [fill_halo_slab, None, functools.partial(pool_group, 0), None,
                 functools.partial(pool_group, 1), None, functools.partial(pool_group, 2), None,
                 functools.partial(pool_group, 3), None, norm_for_swiglu]
    side_work = [w if w is not None else (lambda: None) for w in side_work]

    modc = modc_ref[...]
    _swiglu(h2_ref[cur], win_ref, wout_ref, acc_ref, side_work)
    o_ref[...] = x1_ref[cur] + modc[5:6] * acc_ref[...]


def _pool_ffn_layer(x, mods_l, nw1, wp, ps, j, nw2, w_in, w_out, layer):
    tm = 256
    n = NT // tm
    return pl.pallas_call(
        functools.partial(_pool_ffn_kernel, tm=tm, n_tiles=n),
        out_shape=jax.ShapeDtypeStruct((NT, D), F32),
        grid=(n + 1,),
        in_specs=_halo_specs(tm, NT, lambda s: jnp.minimum(s, n - 1)) + [
            pl.BlockSpec((None, N_MOD, D),
                         lambda s: (_mod_group(jnp.minimum(s, n - 1), tm), 0, 0)),
            pl.BlockSpec((None, N_MOD, D),
                         lambda s: (_mod_group(jnp.maximum(s - 1, 0), tm), 0, 0)),
            _const_spec((1, D)),
            _layer_spec((4, POOL_GROUP_DIM, POOL_GROUP_DIM), j),
            _layer_spec((1, D), j),
            _const_spec((1, D)),
            _layer_spec((D, 2 * D_FF), layer),
            _layer_spec((D_FF, D), layer),
        ],
        out_specs=pl.BlockSpec((tm, D), lambda s: (jnp.maximum(s - 1, 0), 0)),
        scratch_shapes=[pltpu.VMEM((tm + 2 * HALO, D), F32),
                        pltpu.VMEM((2, tm, D), F32),
                        pltpu.VMEM((2, tm, D), BF16),
                        pltpu.VMEM((tm, D), F32)],
        compiler_params=_params(56),
        name="pool_ffn",
    )(x, x, x, mods_l, mods_l, nw1, wp, ps, nw2, w_in, w_out)


def _head_norm(t, w):
    return t * lax.rsqrt(jnp.mean(t * t, axis=-1, keepdims=True) + EPS) * w


def _rope(t, cos, sin_signed, first_half):
    rot = jnp.where(first_half, pltpu.roll(t, HEAD_DIM - 32, axis=1), pltpu.roll(t, 32, axis=1))
    return t * cos + rot * sin_signed


def _qkv_kernel(*refs, tm, rope):
    if rope:
        (x_ref, mod_ref, nw_ref, w_ref, qn_ref, kn_ref, cos_ref, sin_ref,
         q_ref, k_ref, vt_ref) = refs
    else:
        (x_ref, mod_ref, nw_ref, w_ref, qn_ref, kn_ref,
         q_ref, k_ref, vt_ref, nk_ref, nv_ref) = refs
    mod = mod_ref[...]
    h = _norm_mod(x_ref[...], nw_ref[...], mod[0:1], mod[1:2]).astype(BF16)
    qkv = _wdot(h, w_ref[...])
    qn = qn_ref[...]
    kn = kn_ref[...]
    if rope:
        cos = cos_ref[...]
        sin = sin_ref[...]
        first_half = (lax.broadcasted_iota(jnp.int32, (1, HEAD_DIM), 1) & 63) < 32
    for hd in range(N_HEADS):
        t = _head_norm(qkv[:, hd * HEAD_DIM:(hd + 1) * HEAD_DIM], qn)
        if rope:
            t = _rope(t, cos, sin, first_half)
        q_ref[:, hd * HEAD_DIM:(hd + 1) * HEAD_DIM] = (t * QK_SCALE_LOG2).astype(BF16)
    for g in range(N_KV_HEADS):
        ko = (N_HEADS + g) * HEAD_DIM
        vo = (N_HEADS + N_KV_HEADS + g) * HEAD_DIM
        t = _head_norm(qkv[:, ko:ko + HEAD_DIM], kn)
        vv = qkv[:, vo:vo + HEAD_DIM]
        if rope:
            t = _rope(t, cos, sin, first_half)
            vt_ref[g] = vv.T.astype(BF16)
        else:
            for bb in range(tm // SEQ):
                nk_ref[bb, 0, g] = t[bb * SEQ:(bb + 1) * SEQ]
                nv_ref[bb, 0, g] = vv[bb * SEQ:(bb + 1) * SEQ]
                vt_ref[bb, g] = vv[bb * SEQ:(bb + 1) * SEQ].T.astype(BF16)
        k_ref[:, g * HEAD_DIM:(g + 1) * HEAD_DIM] = t.astype(BF16)


def _qkv_layer(x, mods_l, nw, w_qkv, qn, kn, rope_tabs, j):
    tm = 512
    kvd = N_KV_HEADS * HEAD_DIM
    qkv_out = (N_HEADS + 2 * N_KV_HEADS) * HEAD_DIM

    def call(rope, row0, nrows):
        t0 = row0 // tm
        in_specs = [
            pl.BlockSpec((tm, D), lambda i: (t0 + i, 0)),
            pl.BlockSpec((None, N_MOD, D), lambda i: (_mod_group(t0 + i, tm), 0, 0)),
            _const_spec((1, D)),
            _layer_spec((D, qkv_out), j),
            _layer_spec((1, HEAD_DIM), j),
            _layer_spec((1, HEAD_DIM), j),
        ]
        out_shape = [jax.ShapeDtypeStruct((nrows, D), BF16),
                     jax.ShapeDtypeStruct((nrows, kvd), BF16)]
        out_specs = [pl.BlockSpec((tm, D), lambda i: (i, 0)),
                     pl.BlockSpec((tm, kvd), lambda i: (i, 0))]
        args = [x, mods_l, nw, w_qkv, qn, kn]
        if rope:
            tps = DEC_SEQ // tm
            in_specs += [pl.BlockSpec((tm, HEAD_DIM), lambda i: (i % tps, 0)),
                         pl.BlockSpec((tm, HEAD_DIM), lambda i: (i % tps, 0))]
            args += list(rope_tabs)
            out_shape.append(jax.ShapeDtypeStruct((DEC_BATCH, N_KV_HEADS, HEAD_DIM, DEC_SEQ), BF16))
            out_specs.append(pl.BlockSpec((None, N_KV_HEADS, HEAD_DIM, tm),
                                          lambda i: (i // tps, 0, 0, i % tps)))
        else:
            nb = tm // SEQ
            out_shape.append(jax.ShapeDtypeStruct((BATCH, N_KV_HEADS, HEAD_DIM, SEQ), BF16))
            out_specs.append(pl.BlockSpec((nb, N_KV_HEADS, HEAD_DIM, SEQ), lambda i: (i, 0, 0, 0)))
            kv_shape = (BATCH, 1, N_KV_HEADS, SEQ, HEAD_DIM)
            out_shape += [jax.ShapeDtypeStruct(kv_shape, F32)] * 2
            out_specs += [pl.BlockSpec((nb, 1, N_KV_HEADS, SEQ, HEAD_DIM),
                                       lambda i: (i, 0, 0, 0, 0))] * 2
        return pl.pallas_call(
            functools.partial(_qkv_kernel, tm=tm, rope=rope),
            out_shape=out_shape,
            grid=(nrows // tm,),
            in_specs=in_specs,
            out_specs=out_specs,
            compiler_params=_params(40),
            name="qkv_rope" if rope else "qkv_ctx",
        )(*args)

    return call(False, 0, NP), call(True, NP, NS)


def _attn_kernel(*refs, past, hpb):
    if past:
        (x_ref, bound_ref, q_ref, k_ref, vt_ref, ck_ref, cv_ref, mod_ref, wo_ref,
         o_ref, kall_ref, vext_ref, heads_ref) = refs
    else:
        (x_ref, bound_ref, q_ref, k_ref, vt_ref, mod_ref, wo_ref,
         o_ref, kall_ref, vext_ref, heads_ref) = refs
    nkeys = kall_ref.shape[1]

    @pl.when(pl.program_id(1) == 0)
    def _():
        for g in range(N_KV_HEADS):
            if past:
                kall_ref[g, 0:past] = ck_ref[0, 0, g].astype(BF16)
                vext_ref[g, 0:HEAD_DIM, 0:past] = cv_ref[0, 0, g].T.astype(BF16)
            kall_ref[g, past:] = k_ref[:, g * HEAD_DIM:(g + 1) * HEAD_DIM]
            vext_ref[g, 0:HEAD_DIM, past:] = vt_ref[g]
            vext_ref[g, HEAD_DIM:] = jnp.ones((VEXT_ROWS - HEAD_DIM, nkeys), BF16)

    tq = q_ref.shape[0]
    blocks = [list(range(h0, h0 + hpb)) for h0 in range(0, N_HEADS, hpb)]

    def scores(block):
        qb = [q_ref[:, hd * HEAD_DIM:(hd + 1) * HEAD_DIM] for hd in block]
        qb = qb[0] if hpb == 1 else jnp.concatenate(qb, axis=0)
        return _dot_nt(kall_ref[block[0] // Q_PER_KV], qb)

    def attend(shift_of):
        st_next = scores(blocks[0])
        for b, block in enumerate(blocks):
            st = st_next
            if b + 1 < len(blocks):
                st_next = scores(blocks[b + 1])
            pt = jnp.exp2(st - shift_of(st)).astype(BF16)
            ot = _dot(vext_ref[block[0] // Q_PER_KV], pt)
            o = (ot[0:HEAD_DIM] / ot[HEAD_DIM:HEAD_DIM + 1]).T
            for r, hd in enumerate(block):
                heads_ref[:, hd * HEAD_DIM:(hd + 1) * HEAD_DIM] = (
                    o[r * tq:(r + 1) * tq].astype(BF16))

    bound = bound_ref[0]
    use_bound = bound <= MAX_SAFE_SCORE_BOUND

    @pl.when(use_bound)
    def _():
        attend(lambda st: bound)

    @pl.when(jnp.logical_not(use_bound))
    def _():
        attend(lambda st: jnp.max(st, axis=0, keepdims=True))

    gate = mod_ref[...][2:3]
    o_ref[...] = x_ref[...] + gate * _wdot(heads_ref[...], wo_ref[...])


def _attn_layer(x, qkv_p, qkv_s, cache_k, cache_v, mods_l, w_out, qn, kn, j):
    kvd = N_KV_HEADS * HEAD_DIM

    q_len = QK_SCALE_LOG2 * HEAD_DIM ** 0.5 * jnp.max(jnp.abs(qn))
    k_len = HEAD_DIM ** 0.5 * jnp.max(jnp.abs(kn))
    cache_len = jnp.sqrt(jnp.max(jnp.sum(jnp.square(cache_k[:, j]), axis=-1)))
    slack = 1.0 + 2.0 ** -6
    bound_p = (q_len * k_len * slack).reshape(1)
    bound_s = (q_len * jnp.maximum(k_len, cache_len) * slack).reshape(1)

    def call(x, bound, q, k, vt, past, row0, nb, seq, tq, hpb):
        t0 = row0 // tq
        nq = seq // tq
        in_specs = [
            pl.BlockSpec((tq, D), lambda b, i: (t0 + b * nq + i, 0)),
            pl.BlockSpec(memory_space=pltpu.SMEM),
            pl.BlockSpec((tq, D), lambda b, i: (b * nq + i, 0)),
            pl.BlockSpec((seq, kvd), lambda b, i: (b, 0)),
            pl.BlockSpec((None, N_KV_HEADS, HEAD_DIM, seq), lambda b, i: (b, 0, 0, 0)),
        ]
        args = [x, bound, q, k, vt]
        if past:
            cspec = pl.BlockSpec((1, 1, N_KV_HEADS, PAST_LEN, HEAD_DIM),
                                 lambda b, i: (b, j, 0, 0, 0))
            in_specs += [cspec, cspec]
            args += [cache_k, cache_v]
        in_specs += [
            pl.BlockSpec((None, N_MOD, D), lambda b, i: (_mod_group(t0 + b * nq + i, tq), 0, 0)),
            _layer_spec((D, D), j),
        ]
        args += [mods_l, w_out]
        return pl.pallas_call(
            functools.partial(_attn_kernel, past=past, hpb=hpb),
            out_shape=jax.ShapeDtypeStruct((NT, D), F32),
            grid=(nb, nq),
            in_specs=in_specs,
            out_specs=pl.BlockSpec((tq, D), lambda b, i: (t0 + b * nq + i, 0)),
            scratch_shapes=[pltpu.VMEM((N_KV_HEADS, past + seq, HEAD_DIM), BF16),
                            pltpu.VMEM((N_KV_HEADS, VEXT_ROWS, past + seq), BF16),
                            pltpu.VMEM((tq, D), BF16)],
            input_output_aliases={0: 0},
            compiler_params=_params(48, 2),
            name="attn_cached" if past else "attn_ctx",
        )(*args)

    x = call(x, bound_p, *qkv_p, 0, 0, BATCH, SEQ, SEQ, Q_PER_KV)
    return call(x, bound_s, *qkv_s, PAST_LEN, NP, DEC_BATCH, DEC_SEQ, 512, 1)


def _ffn_kernel(*refs, final, tm):
    if final:
        x_ref, mod_ref, nw_ref, win_ref, wout_ref, fw_ref, op_ref, os_ref, h_ref, acc_ref = refs
    else:
        x_ref, mod_ref, nw_ref, win_ref, wout_ref, o_ref, h_ref, acc_ref = refs
    mod = mod_ref[...]
    x = x_ref[...]
    h_ref[...] = _norm_mod(x, nw_ref[...], mod[3:4], mod[4:5]).astype(BF16)
    _swiglu(h_ref[...], win_ref, wout_ref, acc_ref)
    y = x + mod[5:6] * acc_ref[...]
    if not final:
        o_ref[...] = y
        return
    y = y * lax.rsqrt(jnp.mean(y * y, axis=-1, keepdims=True) + EPS) * fw_ref[...]
    is_prompt = pl.program_id(0) < NP // tm

    @pl.when(is_prompt)
    def _():
        op_ref[...] = y

    @pl.when(jnp.logical_not(is_prompt))
    def _():
        os_ref[...] = y


def _ffn_layer(x, mods_l, nw, w_in, w_out, layer, final_w=None):
    tm = 512
    final = final_w is not None
    in_specs = [
        pl.BlockSpec((tm, D), lambda i: (i, 0)),
        pl.BlockSpec((None, N_MOD, D), lambda i: (_mod_group(i, tm), 0, 0)),
        _const_spec((1, D)),
        _layer_spec((D, 2 * D_FF), layer),
        _layer_spec((D_FF, D), layer),
    ]
    args = [x, mods_l, nw, w_in, w_out]
    if final:
        npt = NP // tm
        in_specs.append(_const_spec((1, D)))
        args.append(final_w)
        out_shape = [jax.ShapeDtypeStruct((NP, D), F32), jax.ShapeDtypeStruct((NS, D), F32)]
        out_specs = [pl.BlockSpec((tm, D), lambda i: (jnp.minimum(i, npt - 1), 0)),
                     pl.BlockSpec((tm, D), lambda i: (jnp.maximum(i - npt, 0), 0))]
    else:
        out_shape = jax.ShapeDtypeStruct((NT, D), F32)
        out_specs = pl.BlockSpec((tm, D), lambda i: (i, 0))
    return pl.pallas_call(
        functools.partial(_ffn_kernel, final=final, tm=tm),
        out_shape=out_shape,
        grid=(NT // tm,),
        in_specs=in_specs,
        out_specs=out_specs,
        scratch_shapes=[pltpu.VMEM((tm, D), BF16), pltpu.VMEM((tm, D), F32)],
        compiler_params=_params(56),
        name="ffn_final" if final else "ffn",
    )(*args)


def _rope_tables():
    t = np.arange(DEC_SEQ)
    inv_freq = ROPE_THETA ** (-np.arange(0, ROPE_AXIS_DIM, 2, dtype=np.float64) / ROPE_AXIS_DIM)
    ang_r = (t // GRID_W)[:, None] * inv_freq[None, :]
    ang_c = (t % GRID_W)[:, None] * inv_freq[None, :]
    cos = np.concatenate([np.cos(ang_r)] * 2 + [np.cos(ang_c)] * 2, axis=-1)
    sin = np.concatenate([-np.sin(ang_r), np.sin(ang_r), -np.sin(ang_c), np.sin(ang_c)], axis=-1)
    return jnp.asarray(cos, F32), jnp.asarray(sin, F32)


def kernel(x_prompt, x_sample, cache_k, cache_v, c, c_ctx, norm1_w, norm2_w, ada_w, ada_b,
           conv_in_w, conv_w, conv_out_w, pool_w, pool_scale, attn_qkv_w, q_norm_w, k_norm_w,
           attn_out_w, ffn_in_w, ffn_out_w, final_norm_w):
    cvec = jnp.concatenate([c_ctx[None, :], c, jnp.zeros((8 - N_GROUPS, D), F32)], axis=0)
    mods = _adaln(cvec, ada_w, ada_b)
    mods = mods[:, :N_GROUPS].reshape(DEPTH, N_GROUPS, N_MOD, D)

    pool_scale3 = pool_scale[:, None, :]
    qn3, kn3 = q_norm_w[:, None, :], k_norm_w[:, None, :]

    x = (x_prompt.reshape(NP, D), x_sample.reshape(NS, D))
    new_k = new_v = None
    for i in range(DEPTH):
        kind, j = i % 3, i // 3
        nw1 = norm1_w[i][None, :]
        if kind == 0:
            x = _conv_layer(x, mods[i], nw1, conv_in_w, conv_w, conv_out_w, j)
        elif kind == 1:
            x = _pool_ffn_layer(x, mods[i], nw1, pool_w, pool_scale3, j,
                                norm2_w[i][None, :], ffn_in_w, ffn_out_w, i)
            continue
        else:
            (qp, kp, vtp, new_k, new_v), qkv_s = _qkv_layer(
                x, mods[i], nw1, attn_qkv_w, qn3, kn3, _rope_tables(), j)
            x = _attn_layer(x, (qp, kp, vtp), qkv_s, cache_k, cache_v, mods[i], attn_out_w,
                            q_norm_w[j], k_norm_w[j], j)
        x = _ffn_layer(x, mods[i], norm2_w[i][None, :], ffn_in_w, ffn_out_w, i,
                       final_norm_w[None, :] if i == DEPTH - 1 else None)

    y_prompt, y_sample = x
    return (y_prompt.reshape(BATCH, SEQ, D), y_sample.reshape(DEC_BATCH, DEC_SEQ, D), new_k, new_v)
```

```python
import functools

import numpy as np
import jax
import jax.numpy as jnp
from jax import lax
from jax.experimental import pallas as pl
from jax.experimental.pallas import tpu as pltpu

D = 1024
BATCH = 16
SEQ = 256
DEPTH = 4
DEC_BATCH = 2
DEC_SEQ = 4096
PAST_LEN = 256
GRID_W = 64
N_HEADS = 8
N_KV_HEADS = 2
HEAD_DIM = 128
Q_PER_KV = N_HEADS // N_KV_HEADS
ROPE_AXIS_DIM = HEAD_DIM // 2
ROPE_THETA = 10000.0
POOL_WINDOWS = (2, 4, 8, 16)
POOL_GROUP_DIM = D // 4
D_FF = 2816
N_MOD = 6
EPS = 1e-6
ATTN_SCALE = HEAD_DIM ** -0.5

NP = BATCH * SEQ
NS = DEC_BATCH * DEC_SEQ
NT = NP + NS
N_GROUPS = 1 + DEC_BATCH

QK_SCALE_LOG2 = ATTN_SCALE * float(np.log2(np.e))
MAX_SAFE_SCORE_BOUND = 60.0
BF16_SUBLANES = 16
VEXT_ROWS = HEAD_DIM + BF16_SUBLANES
HALO = BF16_SUBLANES
FF_CHUNK = 256
BF16 = jnp.bfloat16
F32 = jnp.float32


def _dot(a, b):
    return jnp.dot(a, b, preferred_element_type=F32)


def _wdot(a, w):
    return _dot(a, w.astype(BF16))


def _dot_nt(a, b):
    return lax.dot_general(a, b, (((1,), (1,)), ((), ())), preferred_element_type=F32)


def _norm_mod(x, nw, shift, scale):
    y = x * lax.rsqrt(jnp.mean(x * x, axis=-1, keepdims=True) + EPS)
    return (y * nw) * (1.0 + scale) + shift


def _mod_group(tile, tm):
    return jnp.where(tile < NP // tm, 0, 1 + (tile - NP // tm) // (DEC_SEQ // tm))


def _params(vmem_mb, n_axes=1):
    return pltpu.CompilerParams(
        dimension_semantics=("arbitrary",) * n_axes,
        vmem_limit_bytes=vmem_mb * 1024 * 1024)


def _const_spec(shape):
    return pl.BlockSpec(shape, lambda *_: (0,) * len(shape), pipeline_mode=pl.Buffered(1))


def _layer_spec(shape, j):
    return pl.BlockSpec((None,) + tuple(shape), lambda *_: (j,) + (0,) * len(shape),
                        pipeline_mode=pl.Buffered(1))


def _adaln_kernel(c_ref, w_ref, b_ref, o_ref):
    c = c_ref[...]
    sc = (c * jax.nn.sigmoid(c)).astype(BF16)
    o_ref[...] = _dot(sc, w_ref[...].astype(BF16)) + b_ref[...]


def _adaln(cvec, ada_w, ada_b):
    tn = 1536
    return pl.pallas_call(
        _adaln_kernel,
        out_shape=jax.ShapeDtypeStruct((DEPTH, 8, N_MOD * D), F32),
        grid=(DEPTH, N_MOD * D // tn),
        in_specs=[
            pl.BlockSpec((8, D), lambda l, j: (0, 0)),
            pl.BlockSpec((None, D, tn), lambda l, j: (l, 0, j)),
            pl.BlockSpec((None, 1, tn), lambda l, j: (l, 0, j)),
        ],
        out_specs=pl.BlockSpec((None, 8, tn), lambda l, j: (l, 0, j)),
        compiler_params=_params(40, 2),
        name="adaln",
    )(cvec, ada_w, ada_b.reshape(DEPTH, 1, N_MOD * D))


CONV_CHUNK = 256
N_CONV_CHUNKS = D // CONV_CHUNK


def _conv_weight_copies(win_hbm, wout_hbm, win_ref, wout_ref, sem, layer):
    def quad(c):
        c0 = c * CONV_CHUNK
        cols = [pltpu.make_async_copy(win_hbm.at[layer, :, pl.ds(part * D + c0, CONV_CHUNK)],
                                      win_ref.at[:, pl.ds(part * D + c0, CONV_CHUNK)],
                                      sem.at[part, c]) for part in range(3)]
        rows = pltpu.make_async_copy(wout_hbm.at[layer, pl.ds(c0, CONV_CHUNK), :],
                                     wout_ref.at[pl.ds(c0, CONV_CHUNK), :], sem.at[3, c])
        return cols + [rows]
    return [quad(c) for c in range(N_CONV_CHUNKS)]


def _conv_kernel(*refs, tm, split, layer):
    pl.when(pl.program_id(0) == 0)(functools.partial(_conv_tile, True, refs, tm, split, layer))
    pl.when(pl.program_id(0) != 0)(functools.partial(_conv_tile, False, refs, tm, split, layer))


def _conv_tile(first, refs, tm, split, layer):
    i = pl.program_id(0)
    if split:
        (pm_ref, pp_ref, pn_ref, sm_ref, sp_ref, sn_ref, mod_ref, nw_ref, win_hbm, cw_ref, wout_hbm,
         o_ref, h_ref, acc_ref, win_ref, wout_ref, wsem) = refs
    else:
        (x_ref, xp_ref, xn_ref, mod_ref, nw_ref, win_hbm, cw_ref, wout_hbm,
         o_ref, h_ref, acc_ref, win_ref, wout_ref, wsem) = refs
    copies = _conv_weight_copies(win_hbm, wout_hbm, win_ref, wout_ref, wsem, layer)
    if first:
        for chunk in copies:
            for cp in chunk:
                cp.start()
    if split and first:
        x, x_prev, x_next = pm_ref[...], pp_ref[...], pn_ref[...]
    elif split:
        is_prompt = i < NP // tm
        x = jnp.where(is_prompt, pm_ref[...], sm_ref[...])
        x_prev = jnp.where(is_prompt, pp_ref[...], sp_ref[...])
        x_next = jnp.where(is_prompt, pn_ref[...], sn_ref[...])
    else:
        x, x_prev, x_next = x_ref[...], xp_ref[...], xn_ref[...]
    mod = mod_ref[...]
    nw = nw_ref[...]
    shift, scale, gate = mod[0:1], mod[1:2], mod[2:3]
    h_ref[0:HALO] = _norm_mod(x_prev, nw, shift, scale).astype(BF16)
    h_ref[HALO:HALO + tm] = _norm_mod(x, nw, shift, scale).astype(BF16)
    h_ref[HALO + tm:] = _norm_mod(x_next, nw, shift, scale).astype(BF16)

    seq_len = jnp.where(i < NP // tm, SEQ, DEC_SEQ)
    pos = (i * tm + lax.broadcasted_iota(jnp.int32, (tm, 1), 0)) & (seq_len - 1)
    has_prev = pos != 0
    has_next = pos != seq_len - 1
    cw = cw_ref[...]
    rows = tm + 2 * HALO
    def in_proj(j):
        c0 = j * CONV_CHUNK
        if first:
            for cp in copies[j][:3]:
                cp.wait()
        return (_wdot(h_ref[HALO:HALO + tm], win_ref[:, c0:c0 + CONV_CHUNK]),
                _wdot(h_ref[...], win_ref[:, D + c0:D + c0 + CONV_CHUNK]),
                _wdot(h_ref[...], win_ref[:, 2 * D + c0:2 * D + c0 + CONV_CHUNK]))

    nxt = in_proj(0)
    for j in range(N_CONV_CHUNKS):
        c0 = j * CONV_CHUNK
        b, cg, u = nxt
        if j + 1 < N_CONV_CHUNKS:
            nxt = in_proj(j + 1)
        z = cg * u
        z_prev = pltpu.roll(z, 1, axis=0)[HALO:HALO + tm]
        z_next = pltpu.roll(z, rows - 1, axis=0)[HALO:HALO + tm]
        conv = (jnp.where(has_prev, z_prev, 0.0) * cw[0:1, c0:c0 + CONV_CHUNK]
                + z[HALO:HALO + tm] * cw[1:2, c0:c0 + CONV_CHUNK]
                + jnp.where(has_next, z_next, 0.0) * cw[2:3, c0:c0 + CONV_CHUNK])
        if first:
            copies[j][3].wait()
        part = _wdot((b * conv).astype(BF16), wout_ref[c0:c0 + CONV_CHUNK, :])
        if j == 0:
            acc_ref[...] = part
        else:
            acc_ref[...] += part
    o_ref[...] = x + gate * acc_ref[...]


def _halo_specs(tm, nrows, tile_of):
    nh = tm // HALO
    last = nrows // HALO - 1
    return [
        pl.BlockSpec((tm, D), lambda i: (tile_of(i), 0)),
        pl.BlockSpec((HALO, D), lambda i: (jnp.maximum(tile_of(i) * nh - 1, 0), 0)),
        pl.BlockSpec((HALO, D), lambda i: (jnp.minimum((tile_of(i) + 1) * nh, last), 0)),
    ]


def _conv_layer(xs, mods_l, nw, w_in, cw, w_out, j):
    tm = 512
    split = isinstance(xs, tuple)
    if split:
        npt = NP // tm
        x_specs = (_halo_specs(tm, NP, lambda i: jnp.minimum(i, npt - 1))
                   + _halo_specs(tm, NS, lambda i: jnp.maximum(i - npt, 0)))
        x_args = [xs[0]] * 3 + [xs[1]] * 3
    else:
        x_specs = _halo_specs(tm, NT, lambda i: i)
        x_args = [xs] * 3
    return pl.pallas_call(
        functools.partial(_conv_kernel, tm=tm, split=split, layer=j),
        out_shape=jax.ShapeDtypeStruct((NT, D), F32),
        grid=(NT // tm,),
        in_specs=x_specs + [
            pl.BlockSpec((None, N_MOD, D), lambda i: (_mod_group(i, tm), 0, 0)),
            _const_spec((1, D)),
            pl.BlockSpec(memory_space=pl.ANY),
            _layer_spec((3, D), j),
            pl.BlockSpec(memory_space=pl.ANY),
        ],
        out_specs=pl.BlockSpec((tm, D), lambda i: (i, 0)),
        scratch_shapes=[pltpu.VMEM((tm + 2 * HALO, D), BF16), pltpu.VMEM((tm, D), F32),
                        pltpu.VMEM((D, 3 * D), F32), pltpu.VMEM((D, D), F32),
                        pltpu.SemaphoreType.DMA((4, N_CONV_CHUNKS))],
        compiler_params=_params(48),
        name="conv_mixer",
    )(*x_args, mods_l, nw, w_in, cw, w_out)


N_FF_CHUNKS = D_FF // FF_CHUNK


def _ffn_weight_copies(win_hbm, wout_hbm, win_ref, wout_ref, sem, layer):
    def triple(c):
        c0 = c * FF_CHUNK
        return (
            pltpu.make_async_copy(win_hbm.at[layer, :, pl.ds(c0, FF_CHUNK)],
                                  win_ref.at[:, pl.ds(c0, FF_CHUNK)], sem.at[0, c]),
            pltpu.make_async_copy(win_hbm.at[layer, :, pl.ds(D_FF + c0, FF_CHUNK)],
                                  win_ref.at[:, pl.ds(D_FF + c0, FF_CHUNK)], sem.at[1, c]),
            pltpu.make_async_copy(wout_hbm.at[layer, pl.ds(c0, FF_CHUNK), :],
                                  wout_ref.at[pl.ds(c0, FF_CHUNK), :], sem.at[2, c]))
    return [triple(c) for c in range(N_FF_CHUNKS)]


def _ffn_weight_scratch():
    return [pltpu.VMEM((D, 2 * D_FF), F32), pltpu.VMEM((D_FF, D), F32),
            pltpu.SemaphoreType.DMA((3, N_FF_CHUNKS))]


def _swiglu(h, win_ref, wout_ref, acc_ref, side_work=(), arriving=None):
    def in_proj(j):
        c0 = j * FF_CHUNK
        if arriving is not None:
            arriving[j][0].wait()
            arriving[j][1].wait()
        return (_wdot(h, win_ref[:, c0:c0 + FF_CHUNK]),
                _wdot(h, win_ref[:, D_FF + c0:D_FF + c0 + FF_CHUNK]))

    n_chunks = N_FF_CHUNKS
    assert len(side_work) <= n_chunks
    nxt = in_proj(0)
    for j in range(n_chunks):
        c0 = j * FF_CHUNK
        g, u = nxt
        if j + 1 < n_chunks:
            nxt = in_proj(j + 1)
        if j < len(side_work):
            side_work[j]()
        a = (g * jax.nn.sigmoid(g) * u).astype(BF16)
        if arriving is not None:
            arriving[j][2].wait()
        part = _wdot(a, wout_ref[c0:c0 + FF_CHUNK, :])
        if j == 0:
            acc_ref[...] = part
        else:
            acc_ref[...] += part


def _pool_ffn_kernel(x_ref, xp_ref, xn_ref, modn_ref, modc_ref, nw1_ref, wp_ref, ps_ref, nw2_ref,
                     win_hbm, wout_hbm, o_ref, hp_ref, x1_ref, h2_ref, acc_ref,
                     win_ref, wout_ref, wsem, *, tm, n_tiles, layer):
    s = pl.program_id(0)
    pl.when(s == 0)(functools.partial(
        _pool_ffn_step, True, x_ref, xp_ref, xn_ref, modn_ref, modc_ref, nw1_ref, wp_ref, ps_ref,
        nw2_ref, win_hbm, wout_hbm, o_ref, hp_ref, x1_ref, h2_ref, acc_ref, win_ref, wout_ref, wsem,
        tm, n_tiles, layer))
    pl.when(s != 0)(functools.partial(
        _pool_ffn_step, False, x_ref, xp_ref, xn_ref, modn_ref, modc_ref, nw1_ref, wp_ref, ps_ref,
        nw2_ref, win_hbm, wout_hbm, o_ref, hp_ref, x1_ref, h2_ref, acc_ref, win_ref, wout_ref, wsem,
        tm, n_tiles, layer))


def _pool_ffn_step(first, x_ref, xp_ref, xn_ref, modn_ref, modc_ref, nw1_ref, wp_ref, ps_ref,
                   nw2_ref, win_hbm, wout_hbm, o_ref, hp_ref, x1_ref, h2_ref, acc_ref,
                   win_ref, wout_ref, wsem, tm, n_tiles, layer):
    s = pl.program_id(0)
    cur = (s + 1) % 2
    new = s % 2
    copies = _ffn_weight_copies(win_hbm, wout_hbm, win_ref, wout_ref, wsem, layer)
    if first:
        for chunk in copies:
            for cp in chunk:
                cp.start()
        x1_ref[1] = jnp.zeros((tm, D), F32)
        h2_ref[1] = jnp.zeros((tm, D), BF16)

    t = jnp.minimum(s, n_tiles - 1)
    mod = modn_ref[...]
    nw = nw1_ref[...]
    shift, scale, gate = mod[0:1], mod[1:2], mod[2:3]
    seq_len = jnp.where(t < NP // tm, SEQ, DEC_SEQ)
    pos0 = (t * tm) & (seq_len - 1)
    rows = tm + 2 * HALO

    def fill_halo_slab():
        at_start = pos0 == 0
        at_end = pos0 + tm == seq_len
        hp_ref[0:HALO] = jnp.where(at_start, 0.0, _norm_mod(xp_ref[...], nw, shift, scale))
        hp_ref[HALO:HALO + tm] = _norm_mod(x_ref[...], nw, shift, scale)
        hp_ref[HALO + tm:] = jnp.where(at_end, 0.0, _norm_mod(xn_ref[...], nw, shift, scale))

    def pool_group(g):
        w = POOL_WINDOWS[g]
        c0 = g * POOL_GROUP_DIM
        pos = pos0 + lax.broadcasted_iota(jnp.int32, (tm, 1), 0)
        hg = hp_ref[:, c0:c0 + POOL_GROUP_DIM]
        p = hg + pltpu.roll(hg, 1, axis=0)
        step = 1
        while 2 * step < w:
            p = pltpu.roll(p, step, axis=0) + pltpu.roll(p, rows - step, axis=0)
            step *= 2
        cnt = (jnp.minimum(pos + w // 2, seq_len) - jnp.maximum(pos - w // 2, 0)).astype(F32)
        pooled = p[HALO:HALO + tm] / cnt
        diff = (pooled - hg[HALO:HALO + tm]).astype(BF16)
        mixed = _wdot(diff, wp_ref[g]) * ps_ref[...][:, c0:c0 + POOL_GROUP_DIM]
        x1_ref[new, :, c0:c0 + POOL_GROUP_DIM] = (
            x_ref[:, c0:c0 + POOL_GROUP_DIM] + gate[:, c0:c0 + POOL_GROUP_DIM] * mixed)

    def norm_for_swiglu():
        h2_ref[new] = _norm_mod(x1_ref[new], nw2_ref[...], mod[3:4], mod[4:5]).astype(BF16)

    side_work = [fill_halo_slab, None, functools.partial(pool_group, 0), None,
                 functools.partial(pool_group, 1), None, functools.partial(pool_group, 2), None,
                 functools.partial(pool_group, 3), None, norm_for_swiglu]
    side_work = [w if w is not None else (lambda: None) for w in side_work]

    modc = modc_ref[...]
    _swiglu(h2_ref[cur], win_ref, wout_ref, acc_ref, side_work, arriving=copies if first else None)
    o_ref[...] = x1_ref[cur] + modc[5:6] * acc_ref[...]


def _pool_ffn_layer(x, mods_l, nw1, wp, ps, j, nw2, w_in, w_out, layer):
    tm = 256
    n = NT // tm
    return pl.pallas_call(
        functools.partial(_pool_ffn_kernel, tm=tm, n_tiles=n, layer=layer),
        out_shape=jax.ShapeDtypeStruct((NT, D), F32),
        grid=(n + 1,),
        in_specs=_halo_specs(tm, NT, lambda s: jnp.minimum(s, n - 1)) + [
            pl.BlockSpec((None, N_MOD, D),
                         lambda s: (_mod_group(jnp.minimum(s, n - 1), tm), 0, 0)),
            pl.BlockSpec((None, N_MOD, D),
                         lambda s: (_mod_group(jnp.maximum(s - 1, 0), tm), 0, 0)),
            _const_spec((1, D)),
            _layer_spec((4, POOL_GROUP_DIM, POOL_GROUP_DIM), j),
            _layer_spec((1, D), j),
            _const_spec((1, D)),
            pl.BlockSpec(memory_space=pl.ANY),
            pl.BlockSpec(memory_space=pl.ANY),
        ],
        out_specs=pl.BlockSpec((tm, D), lambda s: (jnp.maximum(s - 1, 0), 0)),
        scratch_shapes=[pltpu.VMEM((tm + 2 * HALO, D), F32),
                        pltpu.VMEM((2, tm, D), F32),
                        pltpu.VMEM((2, tm, D), BF16),
                        pltpu.VMEM((tm, D), F32)] + _ffn_weight_scratch(),
        compiler_params=_params(56),
        name="pool_ffn",
    )(x, x, x, mods_l, mods_l, nw1, wp, ps, nw2, w_in, w_out)


def _head_norm(t, w):
    return t * lax.rsqrt(jnp.mean(t * t, axis=-1, keepdims=True) + EPS) * w


def _rope(t, cos, sin_signed, first_half):
    rot = jnp.where(first_half, pltpu.roll(t, HEAD_DIM - 32, axis=1), pltpu.roll(t, 32, axis=1))
    return t * cos + rot * sin_signed


def _qkv_kernel(*refs, tm, rope):
    if rope:
        (x_ref, mod_ref, nw_ref, w_ref, qn_ref, kn_ref, cos_ref, sin_ref,
         q_ref, k_ref, vt_ref) = refs
    else:
        (x_ref, mod_ref, nw_ref, w_ref, qn_ref, kn_ref,
         q_ref, k_ref, vt_ref, nk_ref, nv_ref) = refs
    mod = mod_ref[...]
    h = _norm_mod(x_ref[...], nw_ref[...], mod[0:1], mod[1:2]).astype(BF16)
    qkv = _wdot(h, w_ref[...])
    qn = qn_ref[...]
    kn = kn_ref[...]
    if rope:
        cos = cos_ref[...]
        sin = sin_ref[...]
        first_half = (lax.broadcasted_iota(jnp.int32, (1, HEAD_DIM), 1) & 63) < 32
    for hd in range(N_HEADS):
        t = _head_norm(qkv[:, hd * HEAD_DIM:(hd + 1) * HEAD_DIM], qn)
        if rope:
            t = _rope(t, cos, sin, first_half)
        q_ref[:, hd * HEAD_DIM:(hd + 1) * HEAD_DIM] = (t * QK_SCALE_LOG2).astype(BF16)
    for g in range(N_KV_HEADS):
        ko = (N_HEADS + g) * HEAD_DIM
        vo = (N_HEADS + N_KV_HEADS + g) * HEAD_DIM
        t = _head_norm(qkv[:, ko:ko + HEAD_DIM], kn)
        vv = qkv[:, vo:vo + HEAD_DIM]
        if rope:
            t = _rope(t, cos, sin, first_half)
            vt_ref[g] = vv.T.astype(BF16)
        else:
            for bb in range(tm // SEQ):
                nk_ref[bb, 0, g] = t[bb * SEQ:(bb + 1) * SEQ]
                nv_ref[bb, 0, g] = vv[bb * SEQ:(bb + 1) * SEQ]
                vt_ref[bb, g] = vv[bb * SEQ:(bb + 1) * SEQ].T.astype(BF16)
        k_ref[:, g * HEAD_DIM:(g + 1) * HEAD_DIM] = t.astype(BF16)


def _qkv_layer(x, mods_l, nw, w_qkv, qn, kn, rope_tabs, j):
    tm = 512
    kvd = N_KV_HEADS * HEAD_DIM
    qkv_out = (N_HEADS + 2 * N_KV_HEADS) * HEAD_DIM

    def call(rope, row0, nrows):
        t0 = row0 // tm
        in_specs = [
            pl.BlockSpec((tm, D), lambda i: (t0 + i, 0)),
            pl.BlockSpec((None, N_MOD, D), lambda i: (_mod_group(t0 + i, tm), 0, 0)),
            _const_spec((1, D)),
            _layer_spec((D, qkv_out), j),
            _layer_spec((1, HEAD_DIM), j),
            _layer_spec((1, HEAD_DIM), j),
        ]
        out_shape = [jax.ShapeDtypeStruct((nrows, D), BF16),
                     jax.ShapeDtypeStruct((nrows, kvd), BF16)]
        out_specs = [pl.BlockSpec((tm, D), lambda i: (i, 0)),
                     pl.BlockSpec((tm, kvd), lambda i: (i, 0))]
        args = [x, mods_l, nw, w_qkv, qn, kn]
        if rope:
            tps = DEC_SEQ // tm
            in_specs += [pl.BlockSpec((tm, HEAD_DIM), lambda i: (i % tps, 0)),
                         pl.BlockSpec((tm, HEAD_DIM), lambda i: (i % tps, 0))]
            args += list(rope_tabs)
            out_shape.append(jax.ShapeDtypeStruct((DEC_BATCH, N_KV_HEADS, HEAD_DIM, DEC_SEQ), BF16))
            out_specs.append(pl.BlockSpec((None, N_KV_HEADS, HEAD_DIM, tm),
                                          lambda i: (i // tps, 0, 0, i % tps)))
        else:
            nb = tm // SEQ
            out_shape.append(jax.ShapeDtypeStruct((BATCH, N_KV_HEADS, HEAD_DIM, SEQ), BF16))
            out_specs.append(pl.BlockSpec((nb, N_KV_HEADS, HEAD_DIM, SEQ), lambda i: (i, 0, 0, 0)))
            kv_shape = (BATCH, 1, N_KV_HEADS, SEQ, HEAD_DIM)
            out_shape += [jax.ShapeDtypeStruct(kv_shape, F32)] * 2
            out_specs += [pl.BlockSpec((nb, 1, N_KV_HEADS, SEQ, HEAD_DIM),
                                       lambda i: (i, 0, 0, 0, 0))] * 2
        return pl.pallas_call(
            functools.partial(_qkv_kernel, tm=tm, rope=rope),
            out_shape=out_shape,
            grid=(nrows // tm,),
            in_specs=in_specs,
            out_specs=out_specs,
            compiler_params=_params(40),
            name="qkv_rope" if rope else "qkv_ctx",
        )(*args)

    return call(False, 0, NP), call(True, NP, NS)


def _attn_kernel(*refs, past, hpb):
    if past:
        (x_ref, bound_ref, q_ref, k_ref, vt_ref, ck_ref, cv_ref, mod_ref, wo_ref,
         o_ref, kall_ref, vext_ref, heads_ref) = refs
    else:
        (x_ref, bound_ref, q_ref, k_ref, vt_ref, mod_ref, wo_ref,
         o_ref, kall_ref, vext_ref, heads_ref) = refs
    nkeys = kall_ref.shape[1]

    @pl.when(pl.program_id(1) == 0)
    def _():
        for g in range(N_KV_HEADS):
            if past:
                kall_ref[g, 0:past] = ck_ref[0, 0, g].astype(BF16)
                vext_ref[g, 0:HEAD_DIM, 0:past] = cv_ref[0, 0, g].T.astype(BF16)
            kall_ref[g, past:] = k_ref[:, g * HEAD_DIM:(g + 1) * HEAD_DIM]
            vext_ref[g, 0:HEAD_DIM, past:] = vt_ref[g]
            vext_ref[g, HEAD_DIM:] = jnp.ones((VEXT_ROWS - HEAD_DIM, nkeys), BF16)

    tq = q_ref.shape[0]
    blocks = [list(range(h0, h0 + hpb)) for h0 in range(0, N_HEADS, hpb)]

    def scores(block):
        qb = [q_ref[:, hd * HEAD_DIM:(hd + 1) * HEAD_DIM] for hd in block]
        qb = qb[0] if hpb == 1 else jnp.concatenate(qb, axis=0)
        return _dot_nt(kall_ref[block[0] // Q_PER_KV], qb)

    def attend(shift_of):
        st_next = scores(blocks[0])
        for b, block in enumerate(blocks):
            st = st_next
            if b + 1 < len(blocks):
                st_next = scores(blocks[b + 1])
            pt = jnp.exp2(st - shift_of(st)).astype(BF16)
            ot = _dot(vext_ref[block[0] // Q_PER_KV], pt)
            o = (ot[0:HEAD_DIM] / ot[HEAD_DIM:HEAD_DIM + 1]).T
            for r, hd in enumerate(block):
                heads_ref[:, hd * HEAD_DIM:(hd + 1) * HEAD_DIM] = (
                    o[r * tq:(r + 1) * tq].astype(BF16))

    bound = bound_ref[0]
    use_bound = bound <= MAX_SAFE_SCORE_BOUND

    @pl.when(use_bound)
    def _():
        attend(lambda st: bound)

    @pl.when(jnp.logical_not(use_bound))
    def _():
        attend(lambda st: jnp.max(st, axis=0, keepdims=True))

    gate = mod_ref[...][2:3]
    o_ref[...] = x_ref[...] + gate * _wdot(heads_ref[...], wo_ref[...])


def _attn_layer(x, qkv_p, qkv_s, cache_k, cache_v, mods_l, w_out, qn, kn, j):
    kvd = N_KV_HEADS * HEAD_DIM

    q_len = QK_SCALE_LOG2 * HEAD_DIM ** 0.5 * jnp.max(jnp.abs(qn))
    k_len = HEAD_DIM ** 0.5 * jnp.max(jnp.abs(kn))
    cache_len = jnp.sqrt(jnp.max(jnp.sum(jnp.square(cache_k[:, j]), axis=-1)))
    slack = 1.0 + 2.0 ** -6
    bound_p = (q_len * k_len * slack).reshape(1)
    bound_s = (q_len * jnp.maximum(k_len, cache_len) * slack).reshape(1)

    def call(x, bound, q, k, vt, past, row0, nb, seq, tq, hpb):
        t0 = row0 // tq
        nq = seq // tq
        in_specs = [
            pl.BlockSpec((tq, D), lambda b, i: (t0 + b * nq + i, 0)),
            pl.BlockSpec(memory_space=pltpu.SMEM),
            pl.BlockSpec((tq, D), lambda b, i: (b * nq + i, 0)),
            pl.BlockSpec((seq, kvd), lambda b, i: (b, 0)),
            pl.BlockSpec((None, N_KV_HEADS, HEAD_DIM, seq), lambda b, i: (b, 0, 0, 0)),
        ]
        args = [x, bound, q, k, vt]
        if past:
            cspec = pl.BlockSpec((1, 1, N_KV_HEADS, PAST_LEN, HEAD_DIM),
                                 lambda b, i: (b, j, 0, 0, 0))
            in_specs += [cspec, cspec]
            args += [cache_k, cache_v]
        in_specs += [
            pl.BlockSpec((None, N_MOD, D), lambda b, i: (_mod_group(t0 + b * nq + i, tq), 0, 0)),
            _layer_spec((D, D), j),
        ]
        args += [mods_l, w_out]
        return pl.pallas_call(
            functools.partial(_attn_kernel, past=past, hpb=hpb),
            out_shape=jax.ShapeDtypeStruct((NT, D), F32),
            grid=(nb, nq),
            in_specs=in_specs,
            out_specs=pl.BlockSpec((tq, D), lambda b, i: (t0 + b * nq + i, 0)),
            scratch_shapes=[pltpu.VMEM((N_KV_HEADS, past + seq, HEAD_DIM), BF16),
                            pltpu.VMEM((N_KV_HEADS, VEXT_ROWS, past + seq), BF16),
                            pltpu.VMEM((tq, D), BF16)],
            input_output_aliases={0: 0},
            compiler_params=_params(48, 2),
            name="attn_cached" if past else "attn_ctx",
        )(*args)

    x = call(x, bound_p, *qkv_p, 0, 0, BATCH, SEQ, SEQ, Q_PER_KV)
    return call(x, bound_s, *qkv_s, PAST_LEN, NP, DEC_BATCH, DEC_SEQ, 512, 1)


def _ffn_kernel(*refs, final, tm, layer):
    if final:
        (x_ref, mod_ref, nw_ref, win_hbm, wout_hbm, fw_ref, op_ref, os_ref,
         h_ref, acc_ref, win_ref, wout_ref, wsem) = refs
    else:
        (x_ref, mod_ref, nw_ref, win_hbm, wout_hbm, o_ref,
         h_ref, acc_ref, win_ref, wout_ref, wsem) = refs
    copies = _ffn_weight_copies(win_hbm, wout_hbm, win_ref, wout_ref, wsem, layer)

    def tile(first):
        if first:
            for chunk in copies:
                for cp in chunk:
                    cp.start()
        mod = mod_ref[...]
        x = x_ref[...]
        h_ref[...] = _norm_mod(x, nw_ref[...], mod[3:4], mod[4:5]).astype(BF16)
        _swiglu(h_ref[...], win_ref, wout_ref, acc_ref, arriving=copies if first else None)
        y = x + mod[5:6] * acc_ref[...]
        if not final:
            o_ref[...] = y
            return
        y = y * lax.rsqrt(jnp.mean(y * y, axis=-1, keepdims=True) + EPS) * fw_ref[...]
        if first:
            op_ref[...] = y
            return
        is_prompt = pl.program_id(0) < NP // tm

        @pl.when(is_prompt)
        def _():
            op_ref[...] = y

        @pl.when(jnp.logical_not(is_prompt))
        def _():
            os_ref[...] = y

    pl.when(pl.program_id(0) == 0)(functools.partial(tile, True))
    pl.when(pl.program_id(0) != 0)(functools.partial(tile, False))


def _ffn_layer(x, mods_l, nw, w_in, w_out, layer, final_w=None):
    tm = 512
    final = final_w is not None
    in_specs = [
        pl.BlockSpec((tm, D), lambda i: (i, 0)),
        pl.BlockSpec((None, N_MOD, D), lambda i: (_mod_group(i, tm), 0, 0)),
        _const_spec((1, D)),
        pl.BlockSpec(memory_space=pl.ANY),
        pl.BlockSpec(memory_space=pl.ANY),
    ]
    args = [x, mods_l, nw, w_in, w_out]
    if final:
        npt = NP // tm
        in_specs.append(_const_spec((1, D)))
        args.append(final_w)
        out_shape = [jax.ShapeDtypeStruct((NP, D), F32), jax.ShapeDtypeStruct((NS, D), F32)]
        out_specs = [pl.BlockSpec((tm, D), lambda i: (jnp.minimum(i, npt - 1), 0)),
                     pl.BlockSpec((tm, D), lambda i: (jnp.maximum(i - npt, 0), 0))]
    else:
        out_shape = jax.ShapeDtypeStruct((NT, D), F32)
        out_specs = pl.BlockSpec((tm, D), lambda i: (i, 0))
    return pl.pallas_call(
        functools.partial(_ffn_kernel, final=final, tm=tm, layer=layer),
        out_shape=out_shape,
        grid=(NT // tm,),
        in_specs=in_specs,
        out_specs=out_specs,
        scratch_shapes=[pltpu.VMEM((tm, D), BF16), pltpu.VMEM((tm, D), F32)]
        + _ffn_weight_scratch(),
        compiler_params=_params(56),
        name="ffn_final" if final else "ffn",
    )(*args)


def _rope_tables():
    t = np.arange(DEC_SEQ)
    inv_freq = ROPE_THETA ** (-np.arange(0, ROPE_AXIS_DIM, 2, dtype=np.float64) / ROPE_AXIS_DIM)
    ang_r = (t // GRID_W)[:, None] * inv_freq[None, :]
    ang_c = (t % GRID_W)[:, None] * inv_freq[None, :]
    cos = np.concatenate([np.cos(ang_r)] * 2 + [np.cos(ang_c)] * 2, axis=-1)
    sin = np.concatenate([-np.sin(ang_r), np.sin(ang_r), -np.sin(ang_c), np.sin(ang_c)], axis=-1)
    return jnp.asarray(cos, F32), jnp.asarray(sin, F32)


def kernel(x_prompt, x_sample, cache_k, cache_v, c, c_ctx, norm1_w, norm2_w, ada_w, ada_b,
           conv_in_w, conv_w, conv_out_w, pool_w, pool_scale, attn_qkv_w, q_norm_w, k_norm_w,
           attn_out_w, ffn_in_w, ffn_out_w, final_norm_w):
    cvec = jnp.concatenate([c_ctx[None, :], c, jnp.zeros((8 - N_GROUPS, D), F32)], axis=0)
    mods = _adaln(cvec, ada_w, ada_b)
    mods = mods[:, :N_GROUPS].reshape(DEPTH, N_GROUPS, N_MOD, D)

    pool_scale3 = pool_scale[:, None, :]
    qn3, kn3 = q_norm_w[:, None, :], k_norm_w[:, None, :]

    x = (x_prompt.reshape(NP, D), x_sample.reshape(NS, D))
    new_k = new_v = None
    for i in range(DEPTH):
        kind, j = i % 3, i // 3
        nw1 = norm1_w[i][None, :]
        if kind == 0:
            x = _conv_layer(x, mods[i], nw1, conv_in_w, conv_w, conv_out_w, j)
        elif kind == 1:
            x = _pool_ffn_layer(x, mods[i], nw1, pool_w, pool_scale3, j,
                                norm2_w[i][None, :], ffn_in_w, ffn_out_w, i)
            continue
        else:
            (qp, kp, vtp, new_k, new_v), qkv_s = _qkv_layer(
                x, mods[i], nw1, attn_qkv_w, qn3, kn3, _rope_tables(), j)
            x = _attn_layer(x, (qp, kp, vtp), qkv_s, cache_k, cache_v, mods[i], attn_out_w,
                            q_norm_w[j], k_norm_w[j], j)
        x = _ffn_layer(x, mods[i], norm2_w[i][None, :], ffn_in_w, ffn_out_w, i,
                       final_norm_w[None, :] if i == DEPTH - 1 else None)

    y_prompt, y_sample = x
    return (y_prompt.reshape(BATCH, SEQ, D), y_sample.reshape(DEC_BATCH, DEC_SEQ, D), new_k, new_v)
```

```python
import functools

import numpy as np
import jax
import jax.numpy as jnp
from jax import lax
from jax.experimental import pallas as pl
from jax.experimental.pallas import tpu as pltpu

D = 1024
BATCH = 16
SEQ = 256
DEPTH = 4
DEC_BATCH = 2
DEC_SEQ = 4096
PAST_LEN = 256
GRID_W = 64
N_HEADS = 8
N_KV_HEADS = 2
HEAD_DIM = 128
Q_PER_KV = N_HEADS // N_KV_HEADS
ROPE_AXIS_DIM = HEAD_DIM // 2
ROPE_THETA = 10000.0
POOL_WINDOWS = (2, 4, 8, 16)
POOL_GROUP_DIM = D // 4
D_FF = 2816
N_MOD = 6
EPS = 1e-6
ATTN_SCALE = HEAD_DIM ** -0.5

NP = BATCH * SEQ
NS = DEC_BATCH * DEC_SEQ
NT = NP + NS
N_GROUPS = 1 + DEC_BATCH

QK_SCALE_LOG2 = ATTN_SCALE * float(np.log2(np.e))
MAX_SAFE_SCORE_BOUND = 60.0
BF16_SUBLANES = 16
VEXT_ROWS = HEAD_DIM + BF16_SUBLANES
HALO = BF16_SUBLANES
FF_CHUNK = 256
BF16 = jnp.bfloat16
F32 = jnp.float32


def _dot(a, b):
    return jnp.dot(a, b, preferred_element_type=F32)


def _wdot(a, w):
    return _dot(a, w.astype(BF16))


def _dot_nt(a, b):
    return lax.dot_general(a, b, (((1,), (1,)), ((), ())), preferred_element_type=F32)


def _norm_mod(x, nw, shift, scale):
    y = x * lax.rsqrt(jnp.mean(x * x, axis=-1, keepdims=True) + EPS)
    return (y * nw) * (1.0 + scale) + shift


def _mod_group(tile, tm):
    return jnp.where(tile < NP // tm, 0, 1 + (tile - NP // tm) // (DEC_SEQ // tm))


def _params(vmem_mb, n_axes=1):
    return pltpu.CompilerParams(
        dimension_semantics=("arbitrary",) * n_axes,
        vmem_limit_bytes=vmem_mb * 1024 * 1024)


def _const_spec(shape):
    return pl.BlockSpec(shape, lambda *_: (0,) * len(shape), pipeline_mode=pl.Buffered(1))


def _layer_spec(shape, j):
    return pl.BlockSpec((None,) + tuple(shape), lambda *_: (j,) + (0,) * len(shape),
                        pipeline_mode=pl.Buffered(1))


def _adaln_kernel(c_ref, w_ref, b_ref, o_ref):
    c = c_ref[...]
    sc = (c * jax.nn.sigmoid(c)).astype(BF16)
    o_ref[...] = _dot(sc, w_ref[...].astype(BF16)) + b_ref[...]


def _adaln(cvec, ada_w, ada_b):
    tn = 1536
    return pl.pallas_call(
        _adaln_kernel,
        out_shape=jax.ShapeDtypeStruct((DEPTH, 8, N_MOD * D), F32),
        grid=(DEPTH, N_MOD * D // tn),
        in_specs=[
            pl.BlockSpec((8, D), lambda l, j: (0, 0)),
            pl.BlockSpec((None, D, tn), lambda l, j: (l, 0, j)),
            pl.BlockSpec((None, 1, tn), lambda l, j: (l, 0, j)),
        ],
        out_specs=pl.BlockSpec((None, 8, tn), lambda l, j: (l, 0, j)),
        compiler_params=_params(40, 2),
        name="adaln",
    )(cvec, ada_w, ada_b.reshape(DEPTH, 1, N_MOD * D))


def _conv_kernel(*refs, tm, split):
    i = pl.program_id(0)
    if split:
        (pm_ref, pp_ref, pn_ref, sm_ref, sp_ref, sn_ref,
         mod_ref, nw_ref, win_ref, cw_ref, wout_ref, o_ref, h_ref, acc_ref) = refs
        is_prompt = i < NP // tm
        x = jnp.where(is_prompt, pm_ref[...], sm_ref[...])
        x_prev = jnp.where(is_prompt, pp_ref[...], sp_ref[...])
        x_next = jnp.where(is_prompt, pn_ref[...], sn_ref[...])
    else:
        (x_ref, xp_ref, xn_ref,
         mod_ref, nw_ref, win_ref, cw_ref, wout_ref, o_ref, h_ref, acc_ref) = refs
        x, x_prev, x_next = x_ref[...], xp_ref[...], xn_ref[...]
    mod = mod_ref[...]
    nw = nw_ref[...]
    shift, scale, gate = mod[0:1], mod[1:2], mod[2:3]
    h_ref[0:HALO] = _norm_mod(x_prev, nw, shift, scale).astype(BF16)
    h_ref[HALO:HALO + tm] = _norm_mod(x, nw, shift, scale).astype(BF16)
    h_ref[HALO + tm:] = _norm_mod(x_next, nw, shift, scale).astype(BF16)

    seq_len = jnp.where(i < NP // tm, SEQ, DEC_SEQ)
    pos = (i * tm + lax.broadcasted_iota(jnp.int32, (tm, 1), 0)) & (seq_len - 1)
    has_prev = pos != 0
    has_next = pos != seq_len - 1
    cw = cw_ref[...]
    rows = tm + 2 * HALO

    def in_proj(j):
        c0 = j * 256
        return (_wdot(h_ref[HALO:HALO + tm], win_ref[:, c0:c0 + 256]),
                _wdot(h_ref[...], win_ref[:, D + c0:D + c0 + 256]),
                _wdot(h_ref[...], win_ref[:, 2 * D + c0:2 * D + c0 + 256]))

    nxt = in_proj(0)
    for j in range(D // 256):
        c0 = j * 256
        b, cg, u = nxt
        if j + 1 < D // 256:
            nxt = in_proj(j + 1)
        z = cg * u
        z_prev = pltpu.roll(z, 1, axis=0)[HALO:HALO + tm]
        z_next = pltpu.roll(z, rows - 1, axis=0)[HALO:HALO + tm]
        conv = (jnp.where(has_prev, z_prev, 0.0) * cw[0:1, c0:c0 + 256]
                + z[HALO:HALO + tm] * cw[1:2, c0:c0 + 256]
                + jnp.where(has_next, z_next, 0.0) * cw[2:3, c0:c0 + 256])
        part = _wdot((b * conv).astype(BF16), wout_ref[c0:c0 + 256, :])
        if j == 0:
            acc_ref[...] = part
        else:
            acc_ref[...] += part
    o_ref[...] = x + gate * acc_ref[...]


def _halo_specs(tm, nrows, tile_of):
    nh = tm // HALO
    last = nrows // HALO - 1
    return [
        pl.BlockSpec((tm, D), lambda i: (tile_of(i), 0)),
        pl.BlockSpec((HALO, D), lambda i: (jnp.maximum(tile_of(i) * nh - 1, 0), 0)),
        pl.BlockSpec((HALO, D), lambda i: (jnp.minimum((tile_of(i) + 1) * nh, last), 0)),
    ]


def _conv_layer(xs, mods_l, nw, w_in, cw, w_out, j):
    tm = 512
    split = isinstance(xs, tuple)
    if split:
        npt = NP // tm
        x_specs = (_halo_specs(tm, NP, lambda i: jnp.minimum(i, npt - 1))
                   + _halo_specs(tm, NS, lambda i: jnp.maximum(i - npt, 0)))
        x_args = [xs[0]] * 3 + [xs[1]] * 3
    else:
        x_specs = _halo_specs(tm, NT, lambda i: i)
        x_args = [xs] * 3
    return pl.pallas_call(
        functools.partial(_conv_kernel, tm=tm, split=split),
        out_shape=jax.ShapeDtypeStruct((NT, D), F32),
        grid=(NT // tm,),
        in_specs=x_specs + [
            pl.BlockSpec((None, N_MOD, D), lambda i: (_mod_group(i, tm), 0, 0)),
            _const_spec((1, D)),
            _layer_spec((D, 3 * D), j),
            _layer_spec((3, D), j),
            _layer_spec((D, D), j),
        ],
        out_specs=pl.BlockSpec((tm, D), lambda i: (i, 0)),
        scratch_shapes=[pltpu.VMEM((tm + 2 * HALO, D), BF16), pltpu.VMEM((tm, D), F32)],
        compiler_params=_params(48),
        name="conv_mixer",
    )(*x_args, mods_l, nw, w_in, cw, w_out)


def _swiglu(h, win_ref, wout_ref, acc_ref, side_work=()):
    def in_proj(j):
        c0 = j * FF_CHUNK
        return (_wdot(h, win_ref[:, c0:c0 + FF_CHUNK]),
                _wdot(h, win_ref[:, D_FF + c0:D_FF + c0 + FF_CHUNK]))

    n_chunks = D_FF // FF_CHUNK
    assert len(side_work) <= n_chunks
    nxt = in_proj(0)
    for j in range(n_chunks):
        c0 = j * FF_CHUNK
        g, u = nxt
        if j + 1 < n_chunks:
            nxt = in_proj(j + 1)
        if j < len(side_work):
            side_work[j]()
        a = (g * jax.nn.sigmoid(g) * u).astype(BF16)
        part = _wdot(a, wout_ref[c0:c0 + FF_CHUNK, :])
        if j == 0:
            acc_ref[...] = part
        else:
            acc_ref[...] += part


def _pool_ffn_kernel(x_ref, xp_ref, xn_ref, modn_ref, modc_ref, nw1_ref, wp_ref, ps_ref, nw2_ref,
                     win_ref, wout_ref, o_ref, hp_ref, x1_ref, h2_ref, acc_ref, *, tm, n_tiles):
    s = pl.program_id(0)
    cur = (s + 1) % 2
    new = s % 2

    @pl.when(s == 0)
    def _():
        x1_ref[1] = jnp.zeros((tm, D), F32)
        h2_ref[1] = jnp.zeros((tm, D), BF16)

    t = jnp.minimum(s, n_tiles - 1)
    mod = modn_ref[...]
    nw = nw1_ref[...]
    shift, scale, gate = mod[0:1], mod[1:2], mod[2:3]
    seq_len = jnp.where(t < NP // tm, SEQ, DEC_SEQ)
    pos0 = (t * tm) & (seq_len - 1)
    rows = tm + 2 * HALO

    def fill_halo_slab():
        at_start = pos0 == 0
        at_end = pos0 + tm == seq_len
        hp_ref[0:HALO] = jnp.where(at_start, 0.0, _norm_mod(xp_ref[...], nw, shift, scale))
        hp_ref[HALO:HALO + tm] = _norm_mod(x_ref[...], nw, shift, scale)
        hp_ref[HALO + tm:] = jnp.where(at_end, 0.0, _norm_mod(xn_ref[...], nw, shift, scale))

    def pool_group(g):
        w = POOL_WINDOWS[g]
        c0 = g * POOL_GROUP_DIM
        pos = pos0 + lax.broadcasted_iota(jnp.int32, (tm, 1), 0)
        hg = hp_ref[:, c0:c0 + POOL_GROUP_DIM]
        p = hg + pltpu.roll(hg, 1, axis=0)
        step = 1
        while 2 * step < w:
            p = pltpu.roll(p, step, axis=0) + pltpu.roll(p, rows - step, axis=0)
            step *= 2
        cnt = (jnp.minimum(pos + w // 2, seq_len) - jnp.maximum(pos - w // 2, 0)).astype(F32)
        pooled = p[HALO:HALO + tm] / cnt
        diff = (pooled - hg[HALO:HALO + tm]).astype(BF16)
        mixed = _wdot(diff, wp_ref[g]) * ps_ref[...][:, c0:c0 + POOL_GROUP_DIM]
        x1_ref[new, :, c0:c0 + POOL_GROUP_DIM] = (
            x_ref[:, c0:c0 + POOL_GROUP_DIM] + gate[:, c0:c0 + POOL_GROUP_DIM] * mixed)

    def norm_for_swiglu():
        h2_ref[new] = _norm_mod(x1_ref[new], nw2_ref[...], mod[3:4], mod[4:5]).astype(BF16)

    side_work = [fill_halo_slab, None, functools.partial(pool_group, 0), None,
                 functools.partial(pool_group, 1), None, functools.partial(pool_group, 2), None,
                 functools.partial(pool_group, 3), None, norm_for_swiglu]
    side_work = [w if w is not None else (lambda: None) for w in side_work]

    modc = modc_ref[...]
    _swiglu(h2_ref[cur], win_ref, wout_ref, acc_ref, side_work)
    o_ref[...] = x1_ref[cur] + modc[5:6] * acc_ref[...]


def _pool_ffn_layer(x, mods_l, nw1, wp, ps, j, nw2, w_in, w_out, layer):
    tm = 256
    n = NT // tm
    return pl.pallas_call(
        functools.partial(_pool_ffn_kernel, tm=tm, n_tiles=n),
        out_shape=jax.ShapeDtypeStruct((NT, D), F32),
        grid=(n + 1,),
        in_specs=_halo_specs(tm, NT, lambda s: jnp.minimum(s, n - 1)) + [
            pl.BlockSpec((None, N_MOD, D),
                         lambda s: (_mod_group(jnp.minimum(s, n - 1), tm), 0, 0)),
            pl.BlockSpec((None, N_MOD, D),
                         lambda s: (_mod_group(jnp.maximum(s - 1, 0), tm), 0, 0)),
            _const_spec((1, D)),
            _layer_spec((4, POOL_GROUP_DIM, POOL_GROUP_DIM), j),
            _layer_spec((1, D), j),
            _const_spec((1, D)),
            _layer_spec((D, 2 * D_FF), layer),
            _layer_spec((D_FF, D), layer),
        ],
        out_specs=pl.BlockSpec((tm, D), lambda s: (jnp.maximum(s - 1, 0), 0)),
        scratch_shapes=[pltpu.VMEM((tm + 2 * HALO, D), F32),
                        pltpu.VMEM((2, tm, D), F32),
                        pltpu.VMEM((2, tm, D), BF16),
                        pltpu.VMEM((tm, D), F32)],
        compiler_params=_params(56),
        name="pool_ffn",
    )(x, x, x, mods_l, mods_l, nw1, wp, ps, nw2, w_in, w_out)


def _head_norm(t, w):
    return t * lax.rsqrt(jnp.mean(t * t, axis=-1, keepdims=True) + EPS) * w


def _pair_halves_layout(a):
    q4 = HEAD_DIM // 4
    lane = lax.broadcasted_iota(jnp.int32, (1, HEAD_DIM), 1)
    from_next = pltpu.roll(a, HEAD_DIM - q4, axis=1)
    from_prev = pltpu.roll(a, q4, axis=1)
    return jnp.where((lane >= q4) & (lane < 2 * q4), from_next,
                     jnp.where((lane >= 2 * q4) & (lane < 3 * q4), from_prev, a))


def _qkv_kernel(*refs, tm, rope):
    if rope:
        (x_ref, mod_ref, nw_ref, w_ref, qn_ref, kn_ref, cos_ref, sin_ref,
         q_ref, k_ref, vt_ref, wb_ref) = refs
    else:
        (x_ref, mod_ref, nw_ref, w_ref, qn_ref, kn_ref,
         q_ref, k_ref, vt_ref, nk_ref, nv_ref) = refs
    n_qk = N_HEADS + N_KV_HEADS
    if rope:
        @pl.when(pl.program_id(0) == 0)
        def _():
            for hd in range(n_qk + N_KV_HEADS):
                cols = slice(hd * HEAD_DIM, (hd + 1) * HEAD_DIM)
                w = w_ref[:, cols]
                wb_ref[:, cols] = (_pair_halves_layout(w) if hd < n_qk else w).astype(BF16)

    mod = mod_ref[...]
    h = _norm_mod(x_ref[...], nw_ref[...], mod[0:1], mod[1:2]).astype(BF16)
    qn = qn_ref[...]
    kn = kn_ref[...]
    if rope:
        qn = _pair_halves_layout(qn)
        kn = _pair_halves_layout(kn)
        cos = cos_ref[...]
        sin = sin_ref[...]

    pair = 2 * HEAD_DIM
    n_pairs = (N_HEADS + 2 * N_KV_HEADS) // 2
    same_head = (lax.broadcasted_iota(jnp.int32, (pair, pair), 0) // HEAD_DIM
                 == lax.broadcasted_iota(jnp.int32, (pair, pair), 1) // HEAD_DIM)
    same_head = jnp.where(same_head, 1.0, 0.0).astype(BF16)

    def head_inv_rms(tt):
        sq = tt * tt
        hi = sq.astype(BF16)
        lo = (sq - hi.astype(F32)).astype(BF16)
        ssq = _dot(hi, same_head) + _dot(lo, same_head)
        return lax.rsqrt(ssq * (1.0 / HEAD_DIM) + EPS)

    def normed(t, inv_rms, w):
        if not rope:
            return _head_norm(t, w)
        t = t * inv_rms * w
        return t * cos + pltpu.roll(t, HEAD_DIM // 2, axis=1) * sin

    def proj(c):
        if rope:
            return _dot(h, wb_ref[:, c * pair:(c + 1) * pair])
        return _wdot(h, w_ref[:, c * pair:(c + 1) * pair])

    nxt = proj(0)
    for c in range(n_pairs):
        cur = nxt
        if c + 1 < n_pairs:
            nxt = proj(c + 1)
        inv = head_inv_rms(cur) if rope and 2 * c < n_qk else None
        for r in range(2):
            t = cur[:, r * HEAD_DIM:(r + 1) * HEAD_DIM]
            inv_r = None if inv is None else inv[:, r * HEAD_DIM:(r + 1) * HEAD_DIM]
            hd = 2 * c + r
            if hd < N_HEADS:
                q_ref[:, hd * HEAD_DIM:(hd + 1) * HEAD_DIM] = (
                    normed(t, inv_r, qn) * QK_SCALE_LOG2).astype(BF16)
            elif hd < N_HEADS + N_KV_HEADS:
                g = hd - N_HEADS
                t = normed(t, inv_r, kn)
                if not rope:
                    for bb in range(tm // SEQ):
                        nk_ref[bb, 0, g] = t[bb * SEQ:(bb + 1) * SEQ]
                k_ref[:, g * HEAD_DIM:(g + 1) * HEAD_DIM] = t.astype(BF16)
            else:
                g = hd - N_HEADS - N_KV_HEADS
                if rope:
                    vt_ref[g] = t.T.astype(BF16)
                else:
                    for bb in range(tm // SEQ):
                        nv_ref[bb, 0, g] = t[bb * SEQ:(bb + 1) * SEQ]
                        vt_ref[bb, g] = t[bb * SEQ:(bb + 1) * SEQ].T.astype(BF16)


def _qkv_layer(x, mods_l, nw, w_qkv, qn, kn, rope_tabs, j):
    tm = 512
    kvd = N_KV_HEADS * HEAD_DIM
    qkv_out = (N_HEADS + 2 * N_KV_HEADS) * HEAD_DIM

    def call(rope, row0, nrows):
        t0 = row0 // tm
        in_specs = [
            pl.BlockSpec((tm, D), lambda i: (t0 + i, 0)),
            pl.BlockSpec((None, N_MOD, D), lambda i: (_mod_group(t0 + i, tm), 0, 0)),
            _const_spec((1, D)),
            _layer_spec((D, qkv_out), j),
            _layer_spec((1, HEAD_DIM), j),
            _layer_spec((1, HEAD_DIM), j),
        ]
        out_shape = [jax.ShapeDtypeStruct((nrows, D), BF16),
                     jax.ShapeDtypeStruct((nrows, kvd), BF16)]
        out_specs = [pl.BlockSpec((tm, D), lambda i: (i, 0)),
                     pl.BlockSpec((tm, kvd), lambda i: (i, 0))]
        args = [x, mods_l, nw, w_qkv, qn, kn]
        if rope:
            tps = DEC_SEQ // tm
            in_specs += [pl.BlockSpec((tm, HEAD_DIM), lambda i: (i % tps, 0)),
                         pl.BlockSpec((tm, HEAD_DIM), lambda i: (i % tps, 0))]
            args += list(rope_tabs)
            out_shape.append(jax.ShapeDtypeStruct((DEC_BATCH, N_KV_HEADS, HEAD_DIM, DEC_SEQ), BF16))
            out_specs.append(pl.BlockSpec((None, N_KV_HEADS, HEAD_DIM, tm),
                                          lambda i: (i // tps, 0, 0, i % tps)))
        else:
            nb = tm // SEQ
            out_shape.append(jax.ShapeDtypeStruct((BATCH, N_KV_HEADS, HEAD_DIM, SEQ), BF16))
            out_specs.append(pl.BlockSpec((nb, N_KV_HEADS, HEAD_DIM, SEQ), lambda i: (i, 0, 0, 0)))
            kv_shape = (BATCH, 1, N_KV_HEADS, SEQ, HEAD_DIM)
            out_shape += [jax.ShapeDtypeStruct(kv_shape, F32)] * 2
            out_specs += [pl.BlockSpec((nb, 1, N_KV_HEADS, SEQ, HEAD_DIM),
                                       lambda i: (i, 0, 0, 0, 0))] * 2
        return pl.pallas_call(
            functools.partial(_qkv_kernel, tm=tm, rope=rope),
            out_shape=out_shape,
            grid=(nrows // tm,),
            in_specs=in_specs,
            out_specs=out_specs,
            scratch_shapes=[pltpu.VMEM((D, qkv_out), BF16)] if rope else [],
            compiler_params=_params(40),
            name="qkv_rope" if rope else "qkv_ctx",
        )(*args)

    return call(False, 0, NP), call(True, NP, NS)


def _attn_kernel(*refs, past, hpb):
    if past:
        (x_ref, bound_ref, q_ref, k_ref, vt_ref, ck_ref, cv_ref, mod_ref, wo_ref,
         o_ref, kall_ref, vext_ref, heads_ref) = refs
    else:
        (x_ref, bound_ref, q_ref, k_ref, vt_ref, mod_ref, wo_ref,
         o_ref, kall_ref, vext_ref, heads_ref) = refs
    nkeys = kall_ref.shape[1]

    @pl.when(pl.program_id(1) == 0)
    def _():
        for g in range(N_KV_HEADS):
            if past:
                kall_ref[g, 0:past] = _pair_halves_layout(ck_ref[0, 0, g]).astype(BF16)
                vext_ref[g, 0:HEAD_DIM, 0:past] = cv_ref[0, 0, g].T.astype(BF16)
            kall_ref[g, past:] = k_ref[:, g * HEAD_DIM:(g + 1) * HEAD_DIM]
            vext_ref[g, 0:HEAD_DIM, past:] = vt_ref[g]
            vext_ref[g, HEAD_DIM:] = jnp.ones((VEXT_ROWS - HEAD_DIM, nkeys), BF16)

    tq = q_ref.shape[0]
    blocks = [list(range(h0, h0 + hpb)) for h0 in range(0, N_HEADS, hpb)]

    def scores(block):
        qb = [q_ref[:, hd * HEAD_DIM:(hd + 1) * HEAD_DIM] for hd in block]
        qb = qb[0] if hpb == 1 else jnp.concatenate(qb, axis=0)
        return _dot_nt(kall_ref[block[0] // Q_PER_KV], qb)

    def attend(shift_of):
        st_next = scores(blocks[0])
        for b, block in enumerate(blocks):
            st = st_next
            if b + 1 < len(blocks):
                st_next = scores(blocks[b + 1])
            pt = jnp.exp2(st - shift_of(st)).astype(BF16)
            ot = _dot(vext_ref[block[0] // Q_PER_KV], pt)
            o = (ot[0:HEAD_DIM] / ot[HEAD_DIM:HEAD_DIM + 1]).T
            for r, hd in enumerate(block):
                heads_ref[:, hd * HEAD_DIM:(hd + 1) * HEAD_DIM] = (
                    o[r * tq:(r + 1) * tq].astype(BF16))

    bound = bound_ref[0]
    use_bound = bound <= MAX_SAFE_SCORE_BOUND

    @pl.when(use_bound)
    def _():
        attend(lambda st: bound)

    @pl.when(jnp.logical_not(use_bound))
    def _():
        attend(lambda st: jnp.max(st, axis=0, keepdims=True))

    gate = mod_ref[...][2:3]
    o_ref[...] = x_ref[...] + gate * _wdot(heads_ref[...], wo_ref[...])


def _attn_layer(x, qkv_p, qkv_s, cache_k, cache_v, mods_l, w_out, qn, kn, j):
    kvd = N_KV_HEADS * HEAD_DIM

    q_len = QK_SCALE_LOG2 * HEAD_DIM ** 0.5 * jnp.max(jnp.abs(qn))
    k_len = HEAD_DIM ** 0.5 * jnp.max(jnp.abs(kn))
    cache_len = jnp.sqrt(jnp.max(jnp.sum(jnp.square(cache_k[:, j]), axis=-1)))
    slack = 1.0 + 2.0 ** -6
    bound_p = (q_len * k_len * slack).reshape(1)
    bound_s = (q_len * jnp.maximum(k_len, cache_len) * slack).reshape(1)

    def call(x, bound, q, k, vt, past, row0, nb, seq, tq, hpb):
        t0 = row0 // tq
        nq = seq // tq
        in_specs = [
            pl.BlockSpec((tq, D), lambda b, i: (t0 + b * nq + i, 0)),
            pl.BlockSpec(memory_space=pltpu.SMEM),
            pl.BlockSpec((tq, D), lambda b, i: (b * nq + i, 0)),
            pl.BlockSpec((seq, kvd), lambda b, i: (b, 0)),
            pl.BlockSpec((None, N_KV_HEADS, HEAD_DIM, seq), lambda b, i: (b, 0, 0, 0)),
        ]
        args = [x, bound, q, k, vt]
        if past:
            cspec = pl.BlockSpec((1, 1, N_KV_HEADS, PAST_LEN, HEAD_DIM),
                                 lambda b, i: (b, j, 0, 0, 0))
            in_specs += [cspec, cspec]
            args += [cache_k, cache_v]
        in_specs += [
            pl.BlockSpec((None, N_MOD, D), lambda b, i: (_mod_group(t0 + b * nq + i, tq), 0, 0)),
            _layer_spec((D, D), j),
        ]
        args += [mods_l, w_out]
        return pl.pallas_call(
            functools.partial(_attn_kernel, past=past, hpb=hpb),
            out_shape=jax.ShapeDtypeStruct((NT, D), F32),
            grid=(nb, nq),
            in_specs=in_specs,
            out_specs=pl.BlockSpec((tq, D), lambda b, i: (t0 + b * nq + i, 0)),
            scratch_shapes=[pltpu.VMEM((N_KV_HEADS, past + seq, HEAD_DIM), BF16),
                            pltpu.VMEM((N_KV_HEADS, VEXT_ROWS, past + seq), BF16),
                            pltpu.VMEM((tq, D), BF16)],
            input_output_aliases={0: 0},
            compiler_params=_params(48, 2),
            name="attn_cached" if past else "attn_ctx",
        )(*args)

    x = call(x, bound_p, *qkv_p, 0, 0, BATCH, SEQ, SEQ, Q_PER_KV)
    return call(x, bound_s, *qkv_s, PAST_LEN, NP, DEC_BATCH, DEC_SEQ, 512, 1)


def _ffn_kernel(*refs, final, tm):
    if final:
        x_ref, mod_ref, nw_ref, win_ref, wout_ref, fw_ref, op_ref, os_ref, h_ref, acc_ref = refs
    else:
        x_ref, mod_ref, nw_ref, win_ref, wout_ref, o_ref, h_ref, acc_ref = refs
    mod = mod_ref[...]
    x = x_ref[...]
    h_ref[...] = _norm_mod(x, nw_ref[...], mod[3:4], mod[4:5]).astype(BF16)
    _swiglu(h_ref[...], win_ref, wout_ref, acc_ref)
    y = x + mod[5:6] * acc_ref[...]
    if not final:
        o_ref[...] = y
        return
    y = y * lax.rsqrt(jnp.mean(y * y, axis=-1, keepdims=True) + EPS) * fw_ref[...]
    is_prompt = pl.program_id(0) < NP // tm

    @pl.when(is_prompt)
    def _():
        op_ref[...] = y

    @pl.when(jnp.logical_not(is_prompt))
    def _():
        os_ref[...] = y


def _ffn_layer(x, mods_l, nw, w_in, w_out, layer, final_w=None):
    tm = 512
    final = final_w is not None
    in_specs = [
        pl.BlockSpec((tm, D), lambda i: (i, 0)),
        pl.BlockSpec((None, N_MOD, D), lambda i: (_mod_group(i, tm), 0, 0)),
        _const_spec((1, D)),
        _layer_spec((D, 2 * D_FF), layer),
        _layer_spec((D_FF, D), layer),
    ]
    args = [x, mods_l, nw, w_in, w_out]
    if final:
        npt = NP // tm
        in_specs.append(_const_spec((1, D)))
        args.append(final_w)
        out_shape = [jax.ShapeDtypeStruct((NP, D), F32), jax.ShapeDtypeStruct((NS, D), F32)]
        out_specs = [pl.BlockSpec((tm, D), lambda i: (jnp.minimum(i, npt - 1), 0)),
                     pl.BlockSpec((tm, D), lambda i: (jnp.maximum(i - npt, 0), 0))]
    else:
        out_shape = jax.ShapeDtypeStruct((NT, D), F32)
        out_specs = pl.BlockSpec((tm, D), lambda i: (i, 0))
    return pl.pallas_call(
        functools.partial(_ffn_kernel, final=final, tm=tm),
        out_shape=out_shape,
        grid=(NT // tm,),
        in_specs=in_specs,
        out_specs=out_specs,
        scratch_shapes=[pltpu.VMEM((tm, D), BF16), pltpu.VMEM((tm, D), F32)],
        compiler_params=_params(56),
        name="ffn_final" if final else "ffn",
    )(*args)


def _rope_tables():
    t = np.arange(DEC_SEQ)
    inv_freq = ROPE_THETA ** (-np.arange(0, ROPE_AXIS_DIM, 2, dtype=np.float64) / ROPE_AXIS_DIM)
    ang_r = (t // GRID_W)[:, None] * inv_freq[None, :]
    ang_c = (t % GRID_W)[:, None] * inv_freq[None, :]
    cos = np.concatenate([np.cos(ang_r), np.cos(ang_c)] * 2, axis=-1)
    sin = np.concatenate([-np.sin(ang_r), -np.sin(ang_c), np.sin(ang_r), np.sin(ang_c)], axis=-1)
    return jnp.asarray(cos, F32), jnp.asarray(sin, F32)


def kernel(x_prompt, x_sample, cache_k, cache_v, c, c_ctx, norm1_w, norm2_w, ada_w, ada_b,
           conv_in_w, conv_w, conv_out_w, pool_w, pool_scale, attn_qkv_w, q_norm_w, k_norm_w,
           attn_out_w, ffn_in_w, ffn_out_w, final_norm_w):
    cvec = jnp.concatenate([c_ctx[None, :], c, jnp.zeros((8 - N_GROUPS, D), F32)], axis=0)
    mods = _adaln(cvec, ada_w, ada_b)
    mods = mods[:, :N_GROUPS].reshape(DEPTH, N_GROUPS, N_MOD, D)

    pool_scale3 = pool_scale[:, None, :]
    qn3, kn3 = q_norm_w[:, None, :], k_norm_w[:, None, :]

    x = (x_prompt.reshape(NP, D), x_sample.reshape(NS, D))
    new_k = new_v = None
    for i in range(DEPTH):
        kind, j = i % 3, i // 3
        nw1 = norm1_w[i][None, :]
        if kind == 0:
            x = _conv_layer(x, mods[i], nw1, conv_in_w, conv_w, conv_out_w, j)
        elif kind == 1:
            x = _pool_ffn_layer(x, mods[i], nw1, pool_w, pool_scale3, j,
                                norm2_w[i][None, :], ffn_in_w, ffn_out_w, i)
            continue
        else:
            (qp, kp, vtp, new_k, new_v), qkv_s = _qkv_layer(
                x, mods[i], nw1, attn_qkv_w, qn3, kn3, _rope_tables(), j)
            x = _attn_layer(x, (qp, kp, vtp), qkv_s, cache_k, cache_v, mods[i], attn_out_w,
                            q_norm_w[j], k_norm_w[j], j)
        x = _ffn_layer(x, mods[i], norm2_w[i][None, :], ffn_in_w, ffn_out_w, i,
                       final_norm_w[None, :] if i == DEPTH - 1 else None)

    y_prompt, y_sample = x
    return (y_prompt.reshape(BATCH, SEQ, D), y_sample.reshape(DEC_BATCH, DEC_SEQ, D), new_k, new_v)
```

```python
import functools

import numpy as np
import jax
import jax.numpy as jnp
from jax import lax
from jax.experimental import pallas as pl
from jax.experimental.pallas import tpu as pltpu

D = 1024
BATCH = 16
SEQ = 256
DEPTH = 4
DEC_BATCH = 2
DEC_SEQ = 4096
PAST_LEN = 256
GRID_W = 64
N_HEADS = 8
N_KV_HEADS = 2
HEAD_DIM = 128
Q_PER_KV = N_HEADS // N_KV_HEADS
ROPE_AXIS_DIM = HEAD_DIM // 2
ROPE_THETA = 10000.0
POOL_WINDOWS = (2, 4, 8, 16)
POOL_GROUP_DIM = D // 4
D_FF = 2816
N_MOD = 6
EPS = 1e-6
ATTN_SCALE = HEAD_DIM ** -0.5

NP = BATCH * SEQ
NS = DEC_BATCH * DEC_SEQ
NT = NP + NS
N_GROUPS = 1 + DEC_BATCH

QK_SCALE_LOG2 = ATTN_SCALE * float(np.log2(np.e))
MAX_SAFE_SCORE_BOUND = 60.0
BF16_SUBLANES = 16
VEXT_ROWS = HEAD_DIM + BF16_SUBLANES
HALO = BF16_SUBLANES
FF_CHUNK = 256
N_FF_CHUNKS = D_FF // FF_CHUNK
BF16 = jnp.bfloat16
F32 = jnp.float32


def _dot(a, b):
    return jnp.dot(a, b, preferred_element_type=F32)


def _wdot(a, w):
    return _dot(a, w.astype(BF16))


def _dot_nt(a, b):
    return lax.dot_general(a, b, (((1,), (1,)), ((), ())), preferred_element_type=F32)


def _norm_mod(x, nw, shift, scale):
    y = x * lax.rsqrt(jnp.mean(x * x, axis=-1, keepdims=True) + EPS)
    return (y * nw) * (1.0 + scale) + shift


def _mod_group(tile, tm):
    return jnp.where(tile < NP // tm, 0, 1 + (tile - NP // tm) // (DEC_SEQ // tm))


def _params(vmem_mb, n_axes=1):
    return pltpu.CompilerParams(
        dimension_semantics=("arbitrary",) * n_axes,
        vmem_limit_bytes=vmem_mb * 1024 * 1024)


def _const_spec(shape):
    return pl.BlockSpec(shape, lambda *_: (0,) * len(shape), pipeline_mode=pl.Buffered(1))


def _layer_spec(shape, j):
    return pl.BlockSpec((None,) + tuple(shape), lambda *_: (j,) + (0,) * len(shape),
                        pipeline_mode=pl.Buffered(1))


def _adaln_kernel(c_ref, w_ref, b_ref, o_ref):
    c = c_ref[...]
    sc = (c * jax.nn.sigmoid(c)).astype(BF16)
    o_ref[...] = _dot(sc, w_ref[...].astype(BF16)) + b_ref[...]


def _adaln(cvec, ada_w, ada_b3, n_layers):
    tn = 1536
    return pl.pallas_call(
        _adaln_kernel,
        out_shape=jax.ShapeDtypeStruct((n_layers, 8, N_MOD * D), F32),
        grid=(n_layers, N_MOD * D // tn),
        in_specs=[
            pl.BlockSpec((8, D), lambda l, j: (0, 0)),
            pl.BlockSpec((None, D, tn), lambda l, j: (l, 0, j)),
            pl.BlockSpec((None, 1, tn), lambda l, j: (l, 0, j)),
        ],
        out_specs=pl.BlockSpec((None, 8, tn), lambda l, j: (l, 0, j)),
        compiler_params=_params(40, 2),
        name="adaln",
    )(cvec, ada_w, ada_b3)


def _conv_kernel(*refs, tm, split):
    i = pl.program_id(0)
    if split:
        (pm_ref, pp_ref, pn_ref, sm_ref, sp_ref, sn_ref,
         mod_ref, nw_ref, win_ref, cw_ref, wout_ref, o_ref, h_ref, acc_ref) = refs
        is_prompt = i < NP // tm
        x = jnp.where(is_prompt, pm_ref[...], sm_ref[...])
        x_prev = jnp.where(is_prompt, pp_ref[...], sp_ref[...])
        x_next = jnp.where(is_prompt, pn_ref[...], sn_ref[...])
    else:
        (x_ref, xp_ref, xn_ref,
         mod_ref, nw_ref, win_ref, cw_ref, wout_ref, o_ref, h_ref, acc_ref) = refs
        x, x_prev, x_next = x_ref[...], xp_ref[...], xn_ref[...]
    mod = mod_ref[...]
    nw = nw_ref[...]
    shift, scale, gate = mod[0:1], mod[1:2], mod[2:3]
    h_ref[0:HALO] = _norm_mod(x_prev, nw, shift, scale).astype(BF16)
    h_ref[HALO:HALO + tm] = _norm_mod(x, nw, shift, scale).astype(BF16)
    h_ref[HALO + tm:] = _norm_mod(x_next, nw, shift, scale).astype(BF16)

    seq_len = jnp.where(i < NP // tm, SEQ, DEC_SEQ)
    pos = (i * tm + lax.broadcasted_iota(jnp.int32, (tm, 1), 0)) & (seq_len - 1)
    has_prev = pos != 0
    has_next = pos != seq_len - 1
    cw = cw_ref[...]
    rows = tm + 2 * HALO

    def in_proj(j):
        c0 = j * 256
        return (_wdot(h_ref[HALO:HALO + tm], win_ref[:, c0:c0 + 256]),
                _wdot(h_ref[...], win_ref[:, D + c0:D + c0 + 256]),
                _wdot(h_ref[...], win_ref[:, 2 * D + c0:2 * D + c0 + 256]))

    nxt = in_proj(0)
    for j in range(D // 256):
        c0 = j * 256
        b, cg, u = nxt
        if j + 1 < D // 256:
            nxt = in_proj(j + 1)
        z = cg * u
        z_prev = pltpu.roll(z, 1, axis=0)[HALO:HALO + tm]
        z_next = pltpu.roll(z, rows - 1, axis=0)[HALO:HALO + tm]
        conv = (jnp.where(has_prev, z_prev, 0.0) * cw[0:1, c0:c0 + 256]
                + z[HALO:HALO + tm] * cw[1:2, c0:c0 + 256]
                + jnp.where(has_next, z_next, 0.0) * cw[2:3, c0:c0 + 256])
        part = _wdot((b * conv).astype(BF16), wout_ref[c0:c0 + 256, :])
        if j == 0:
            acc_ref[...] = part
        else:
            acc_ref[...] += part
    o_ref[...] = x + gate * acc_ref[...]


def _halo_specs(tm, nrows, tile_of):
    nh = tm // HALO
    last = nrows // HALO - 1
    return [
        pl.BlockSpec((tm, D), lambda i: (tile_of(i), 0)),
        pl.BlockSpec((HALO, D), lambda i: (jnp.maximum(tile_of(i) * nh - 1, 0), 0)),
        pl.BlockSpec((HALO, D), lambda i: (jnp.minimum((tile_of(i) + 1) * nh, last), 0)),
    ]


def _conv_layer(xs, mods_l, nw, w_in, cw, w_out, j):
    tm = 512
    split = isinstance(xs, tuple)
    if split:
        npt = NP // tm
        x_specs = (_halo_specs(tm, NP, lambda i: jnp.minimum(i, npt - 1))
                   + _halo_specs(tm, NS, lambda i: jnp.maximum(i - npt, 0)))
        x_args = [xs[0]] * 3 + [xs[1]] * 3
    else:
        x_specs = _halo_specs(tm, NT, lambda i: i)
        x_args = [xs] * 3
    return pl.pallas_call(
        functools.partial(_conv_kernel, tm=tm, split=split),
        out_shape=jax.ShapeDtypeStruct((NT, D), F32),
        grid=(NT // tm,),
        in_specs=x_specs + [
            pl.BlockSpec((None, N_MOD, D), lambda i: (_mod_group(i, tm), 0, 0)),
            _const_spec((1, D)),
            _layer_spec((D, 3 * D), j),
            _layer_spec((3, D), j),
            _layer_spec((D, D), j),
        ],
        out_specs=pl.BlockSpec((tm, D), lambda i: (i, 0)),
        scratch_shapes=[pltpu.VMEM((tm + 2 * HALO, D), BF16), pltpu.VMEM((tm, D), F32)],
        compiler_params=_params(48),
        name="conv_mixer",
    )(*x_args, mods_l, nw, w_in, cw, w_out)


def _swiglu(h, win_ref, wout_ref, acc_ref, side_work=()):
    def in_proj(j):
        c0 = j * FF_CHUNK
        return (_wdot(h, win_ref[:, c0:c0 + FF_CHUNK]),
                _wdot(h, win_ref[:, D_FF + c0:D_FF + c0 + FF_CHUNK]))

    n_chunks = N_FF_CHUNKS
    assert len(side_work) <= n_chunks
    nxt = in_proj(0)
    for j in range(n_chunks):
        c0 = j * FF_CHUNK
        g, u = nxt
        if j + 1 < n_chunks:
            nxt = in_proj(j + 1)
        if j < len(side_work):
            side_work[j]()
        a = (g * jax.nn.sigmoid(g) * u).astype(BF16)
        part = _wdot(a, wout_ref[c0:c0 + FF_CHUNK, :])
        if j == 0:
            acc_ref[...] = part
        else:
            acc_ref[...] += part


def _pool_ffn_kernel(x_ref, xp_ref, xn_ref, modn_ref, modc_ref, nw1_ref, wp_ref, ps_ref, nw2_ref,
                     win_ref, wout_ref, o_ref, hp_ref, st_ref, x1_ref, h2_ref, acc_ref,
                     *, tm, n_tiles):
    s = pl.program_id(0)
    cur = (s + 1) % 2
    new = s % 2

    @pl.when(s == 0)
    def _():
        x1_ref[1] = jnp.zeros((tm, D), F32)
        h2_ref[1] = jnp.zeros((tm, D), BF16)

    t = jnp.minimum(s, n_tiles - 1)
    mod = modn_ref[...]
    nw = nw1_ref[...]
    shift, scale, gate = mod[0:1], mod[1:2], mod[2:3]
    seq_len = jnp.where(t < NP // tm, SEQ, DEC_SEQ)
    pos0 = (t * tm) & (seq_len - 1)
    rows = tm + 2 * HALO

    def row_stats():
        def inv_rms(xv):
            r = lax.rsqrt(jnp.mean(xv * xv, axis=-1, keepdims=True) + EPS)
            return jnp.broadcast_to(r, (xv.shape[0], 128))
        st_ref[0:HALO] = inv_rms(xp_ref[...])
        st_ref[HALO:HALO + tm] = inv_rms(x_ref[...])
        st_ref[HALO + tm:] = inv_rms(xn_ref[...])

    def fill_group(g):
        at_start = pos0 == 0
        at_end = pos0 + tm == seq_len
        for half in range(POOL_GROUP_DIM // 128):
            cs = slice(g * POOL_GROUP_DIM + half * 128, g * POOL_GROUP_DIM + (half + 1) * 128)

            def nm(xv, iv):
                return ((xv * iv) * nw[:, cs]) * (1.0 + scale[:, cs]) + shift[:, cs]

            hp_ref[0:HALO, cs] = jnp.where(at_start, 0.0, nm(xp_ref[:, cs], st_ref[0:HALO]))
            hp_ref[HALO:HALO + tm, cs] = nm(x_ref[:, cs], st_ref[HALO:HALO + tm])
            hp_ref[HALO + tm:, cs] = jnp.where(at_end, 0.0, nm(xn_ref[:, cs], st_ref[HALO + tm:]))

    def pool_group(g):
        w = POOL_WINDOWS[g]
        c0 = g * POOL_GROUP_DIM
        pos = pos0 + lax.broadcasted_iota(jnp.int32, (tm, 1), 0)
        hg = hp_ref[:, c0:c0 + POOL_GROUP_DIM]
        p = hg + pltpu.roll(hg, 1, axis=0)
        step = 1
        while 2 * step < w:
            p = pltpu.roll(p, step, axis=0) + pltpu.roll(p, rows - step, axis=0)
            step *= 2
        cnt = (jnp.minimum(pos + w // 2, seq_len) - jnp.maximum(pos - w // 2, 0)).astype(F32)
        pooled = p[HALO:HALO + tm] / cnt
        diff = (pooled - hg[HALO:HALO + tm]).astype(BF16)
        mixed = _wdot(diff, wp_ref[g]) * ps_ref[...][:, c0:c0 + POOL_GROUP_DIM]
        x1_ref[new, :, c0:c0 + POOL_GROUP_DIM] = (
            x_ref[:, c0:c0 + POOL_GROUP_DIM] + gate[:, c0:c0 + POOL_GROUP_DIM] * mixed)

    def norm_for_swiglu():
        h2_ref[new] = _norm_mod(x1_ref[new], nw2_ref[...], mod[3:4], mod[4:5]).astype(BF16)

    side_work = [row_stats]
    for g in range(len(POOL_WINDOWS)):
        side_work += [functools.partial(fill_group, g), functools.partial(pool_group, g)]
    side_work.append(norm_for_swiglu)

    modc = modc_ref[...]
    _swiglu(h2_ref[cur], win_ref, wout_ref, acc_ref, side_work)
    o_ref[...] = x1_ref[cur] + modc[5:6] * acc_ref[...]


def _pool_ffn_layer(x, mods_l, nw1, wp, ps, j, nw2, w_in, w_out, layer):
    tm = 256
    n = NT // tm
    return pl.pallas_call(
        functools.partial(_pool_ffn_kernel, tm=tm, n_tiles=n),
        out_shape=jax.ShapeDtypeStruct((NT, D), F32),
        grid=(n + 1,),
        in_specs=_halo_specs(tm, NT, lambda s: jnp.minimum(s, n - 1)) + [
            pl.BlockSpec((None, N_MOD, D),
                         lambda s: (_mod_group(jnp.minimum(s, n - 1), tm), 0, 0)),
            pl.BlockSpec((None, N_MOD, D),
                         lambda s: (_mod_group(jnp.maximum(s - 1, 0), tm), 0, 0)),
            _const_spec((1, D)),
            _layer_spec((4, POOL_GROUP_DIM, POOL_GROUP_DIM), j),
            _layer_spec((1, D), j),
            _const_spec((1, D)),
            _layer_spec((D, 2 * D_FF), layer),
            _layer_spec((D_FF, D), layer),
        ],
        out_specs=pl.BlockSpec((tm, D), lambda s: (jnp.maximum(s - 1, 0), 0)),
        scratch_shapes=[pltpu.VMEM((tm + 2 * HALO, D), F32),
                        pltpu.VMEM((tm + 2 * HALO, 128), F32),
                        pltpu.VMEM((2, tm, D), F32),
                        pltpu.VMEM((2, tm, D), BF16),
                        pltpu.VMEM((tm, D), F32)],
        compiler_params=_params(56),
        name="pool_ffn",
    )(x, x, x, mods_l, mods_l, nw1, wp, ps, nw2, w_in, w_out)


def _head_norm(t, w):
    return t * lax.rsqrt(jnp.mean(t * t, axis=-1, keepdims=True) + EPS) * w


def _pair_halves_layout(a):
    q4 = HEAD_DIM // 4
    lane = lax.broadcasted_iota(jnp.int32, (1, HEAD_DIM), 1)
    from_next = pltpu.roll(a, HEAD_DIM - q4, axis=1)
    from_prev = pltpu.roll(a, q4, axis=1)
    return jnp.where((lane >= q4) & (lane < 2 * q4), from_next,
                     jnp.where((lane >= 2 * q4) & (lane < 3 * q4), from_prev, a))


def _qkv_kernel(*refs, tm, rope):
    if rope:
        (x_ref, mod_ref, nw_ref, w_ref, qn_ref, kn_ref, cos_ref, sin_ref,
         q_ref, k_ref, vt_ref, wb_ref) = refs
    else:
        (x_ref, mod_ref, nw_ref, w_ref, qn_ref, kn_ref,
         q_ref, k_ref, vt_ref, nk_ref, nv_ref) = refs
    n_qk = N_HEADS + N_KV_HEADS
    if rope:
        @pl.when(pl.program_id(0) == 0)
        def _():
            for hd in range(n_qk + N_KV_HEADS):
                cols = slice(hd * HEAD_DIM, (hd + 1) * HEAD_DIM)
                w = w_ref[:, cols]
                wb_ref[:, cols] = (_pair_halves_layout(w) if hd < n_qk else w).astype(BF16)

    mod = mod_ref[...]
    h = _norm_mod(x_ref[...], nw_ref[...], mod[0:1], mod[1:2]).astype(BF16)
    qn = qn_ref[...]
    kn = kn_ref[...]
    if rope:
        qn = _pair_halves_layout(qn)
        kn = _pair_halves_layout(kn)
        cos = cos_ref[...]
        sin = sin_ref[...]

    pair = 2 * HEAD_DIM
    n_pairs = (N_HEADS + 2 * N_KV_HEADS) // 2
    same_head = (lax.broadcasted_iota(jnp.int32, (pair, pair), 0) // HEAD_DIM
                 == lax.broadcasted_iota(jnp.int32, (pair, pair), 1) // HEAD_DIM)
    same_head = jnp.where(same_head, 1.0, 0.0).astype(BF16)

    def head_inv_rms(tt):
        sq = tt * tt
        hi = sq.astype(BF16)
        lo = (sq - hi.astype(F32)).astype(BF16)
        ssq = _dot(hi, same_head) + _dot(lo, same_head)
        return lax.rsqrt(ssq * (1.0 / HEAD_DIM) + EPS)

    def normed(t, inv_rms, w):
        if not rope:
            return _head_norm(t, w)
        t = t * inv_rms * w
        return t * cos + pltpu.roll(t, HEAD_DIM // 2, axis=1) * sin

    def proj(c):
        if rope:
            return _dot(h, wb_ref[:, c * pair:(c + 1) * pair])
        return _wdot(h, w_ref[:, c * pair:(c + 1) * pair])

    nxt = proj(0)
    for c in range(n_pairs):
        cur = nxt
        if c + 1 < n_pairs:
            nxt = proj(c + 1)
        inv = head_inv_rms(cur) if rope and 2 * c < n_qk else None
        for r in range(2):
            t = cur[:, r * HEAD_DIM:(r + 1) * HEAD_DIM]
            inv_r = None if inv is None else inv[:, r * HEAD_DIM:(r + 1) * HEAD_DIM]
            hd = 2 * c + r
            if hd < N_HEADS:
                q_ref[:, hd * HEAD_DIM:(hd + 1) * HEAD_DIM] = (
                    normed(t, inv_r, qn) * QK_SCALE_LOG2).astype(BF16)
            elif hd < N_HEADS + N_KV_HEADS:
                g = hd - N_HEADS
                t = normed(t, inv_r, kn)
                if not rope:
                    for bb in range(tm // SEQ):
                        nk_ref[bb, 0, g] = t[bb * SEQ:(bb + 1) * SEQ]
                k_ref[:, g * HEAD_DIM:(g + 1) * HEAD_DIM] = t.astype(BF16)
            else:
                g = hd - N_HEADS - N_KV_HEADS
                if rope:
                    vt_ref[g] = t.T.astype(BF16)
                else:
                    for bb in range(tm // SEQ):
                        nv_ref[bb, 0, g] = t[bb * SEQ:(bb + 1) * SEQ]
                        vt_ref[bb, g] = t[bb * SEQ:(bb + 1) * SEQ].T.astype(BF16)


def _qkv_layer(x, mods_l, nw, w_qkv, qn, kn, rope_tabs, j):
    tm = 512
    kvd = N_KV_HEADS * HEAD_DIM
    qkv_out = (N_HEADS + 2 * N_KV_HEADS) * HEAD_DIM

    def call(rope, row0, nrows):
        t0 = row0 // tm
        in_specs = [
            pl.BlockSpec((tm, D), lambda i: (t0 + i, 0)),
            pl.BlockSpec((None, N_MOD, D), lambda i: (_mod_group(t0 + i, tm), 0, 0)),
            _const_spec((1, D)),
            _layer_spec((D, qkv_out), j),
            _layer_spec((1, HEAD_DIM), j),
            _layer_spec((1, HEAD_DIM), j),
        ]
        out_shape = [jax.ShapeDtypeStruct((nrows, D), BF16),
                     jax.ShapeDtypeStruct((nrows, kvd), BF16)]
        out_specs = [pl.BlockSpec((tm, D), lambda i: (i, 0)),
                     pl.BlockSpec((tm, kvd), lambda i: (i, 0))]
        args = [x, mods_l, nw, w_qkv, qn, kn]
        if rope:
            tps = DEC_SEQ // tm
            in_specs += [pl.BlockSpec((tm, HEAD_DIM), lambda i: (i % tps, 0)),
                         pl.BlockSpec((tm, HEAD_DIM), lambda i: (i % tps, 0))]
            args += list(rope_tabs)
            out_shape.append(jax.ShapeDtypeStruct((DEC_BATCH, N_KV_HEADS, HEAD_DIM, DEC_SEQ), BF16))
            out_specs.append(pl.BlockSpec((None, N_KV_HEADS, HEAD_DIM, tm),
                                          lambda i: (i // tps, 0, 0, i % tps)))
        else:
            nb = tm // SEQ
            out_shape.append(jax.ShapeDtypeStruct((BATCH, N_KV_HEADS, HEAD_DIM, SEQ), BF16))
            out_specs.append(pl.BlockSpec((nb, N_KV_HEADS, HEAD_DIM, SEQ), lambda i: (i, 0, 0, 0)))
            kv_shape = (BATCH, 1, N_KV_HEADS, SEQ, HEAD_DIM)
            out_shape += [jax.ShapeDtypeStruct(kv_shape, F32)] * 2
            out_specs += [pl.BlockSpec((nb, 1, N_KV_HEADS, SEQ, HEAD_DIM),
                                       lambda i: (i, 0, 0, 0, 0))] * 2
        return pl.pallas_call(
            functools.partial(_qkv_kernel, tm=tm, rope=rope),
            out_shape=out_shape,
            grid=(nrows // tm,),
            in_specs=in_specs,
            out_specs=out_specs,
            scratch_shapes=[pltpu.VMEM((D, qkv_out), BF16)] if rope else [],
            compiler_params=_params(40),
            name="qkv_rope" if rope else "qkv_ctx",
        )(*args)

    return call(False, 0, NP), call(True, NP, NS)


def _attn_kernel(*refs, past, hpb):
    if past:
        (x_ref, bound_ref, q_ref, k_ref, vt_ref, ck_ref, cv_ref, mod_ref, wo_ref,
         o_ref, kall_ref, vext_ref, heads_ref) = refs
    else:
        (x_ref, bound_ref, q_ref, k_ref, vt_ref, mod_ref, wo_ref,
         o_ref, kall_ref, vext_ref, heads_ref) = refs
    nkeys = kall_ref.shape[1]
    new_keys = nkeys - past
    nseq = vt_ref.shape[0]
    tq = q_ref.shape[0] // nseq

    @pl.when(pl.program_id(1) == 0)
    def _():
        for sq in range(nseq):
            for g in range(N_KV_HEADS):
                u = sq * N_KV_HEADS + g
                if past:
                    kall_ref[u, 0:past] = _pair_halves_layout(ck_ref[0, 0, g]).astype(BF16)
                    vext_ref[u, 0:HEAD_DIM, 0:past] = cv_ref[0, 0, g].T.astype(BF16)
                kall_ref[u, past:] = k_ref[sq * new_keys:(sq + 1) * new_keys,
                                           g * HEAD_DIM:(g + 1) * HEAD_DIM]
                vext_ref[u, 0:HEAD_DIM, past:] = vt_ref[sq, g]
                vext_ref[u, HEAD_DIM:] = jnp.ones((VEXT_ROWS - HEAD_DIM, nkeys), BF16)

    blocks = [(sq, list(range(h0, h0 + hpb)))
              for sq in range(nseq) for h0 in range(0, N_HEADS, hpb)]

    def scores(block):
        sq, heads = block
        qb = [q_ref[sq * tq:(sq + 1) * tq, hd * HEAD_DIM:(hd + 1) * HEAD_DIM] for hd in heads]
        qb = qb[0] if hpb == 1 else jnp.concatenate(qb, axis=0)
        return _dot_nt(kall_ref[sq * N_KV_HEADS + heads[0] // Q_PER_KV], qb)

    def attend(shift_of):
        st_next = scores(blocks[0])
        for b, (sq, heads) in enumerate(blocks):
            st = st_next
            if b + 1 < len(blocks):
                st_next = scores(blocks[b + 1])
            pt = jnp.exp2(st - shift_of(st)).astype(BF16)
            ot = _dot(vext_ref[sq * N_KV_HEADS + heads[0] // Q_PER_KV], pt)
            o = (ot[0:HEAD_DIM] / ot[HEAD_DIM:HEAD_DIM + 1]).T
            for r, hd in enumerate(heads):
                heads_ref[sq * tq:(sq + 1) * tq, hd * HEAD_DIM:(hd + 1) * HEAD_DIM] = (
                    o[r * tq:(r + 1) * tq].astype(BF16))

    bound = bound_ref[0]
    use_bound = bound <= MAX_SAFE_SCORE_BOUND

    @pl.when(use_bound)
    def _():
        attend(lambda st: bound)

    @pl.when(jnp.logical_not(use_bound))
    def _():
        attend(lambda st: jnp.max(st, axis=0, keepdims=True))

    gate = mod_ref[...][2:3]
    o_ref[...] = x_ref[...] + gate * _wdot(heads_ref[...], wo_ref[...])


def _attn_layer(x, qkv_p, qkv_s, cache_k, cache_v, mods_l, w_out, qn, kn, j):
    kvd = N_KV_HEADS * HEAD_DIM

    q_len = QK_SCALE_LOG2 * HEAD_DIM ** 0.5 * jnp.max(jnp.abs(qn))
    k_len = HEAD_DIM ** 0.5 * jnp.max(jnp.abs(kn))
    cache_len = jnp.sqrt(jnp.max(jnp.sum(jnp.square(cache_k[:, j]), axis=-1)))
    slack = 1.0 + 2.0 ** -6
    bound_p = (q_len * k_len * slack).reshape(1)
    bound_s = (q_len * jnp.maximum(k_len, cache_len) * slack).reshape(1)

    def call(x, bound, q, k, vt, past, row0, nb, seq, tq, hpb, nseq):
        rows = tq * nseq
        t0 = row0 // rows
        nq = seq // tq
        assert nseq == 1 or nq == 1
        in_specs = [
            pl.BlockSpec((rows, D), lambda b, i: (t0 + b * nq + i, 0)),
            pl.BlockSpec(memory_space=pltpu.SMEM),
            pl.BlockSpec((rows, D), lambda b, i: (b * nq + i, 0)),
            pl.BlockSpec((nseq * seq, kvd), lambda b, i: (b, 0)),
            pl.BlockSpec((nseq, N_KV_HEADS, HEAD_DIM, seq), lambda b, i: (b, 0, 0, 0)),
        ]
        args = [x, bound, q, k, vt]
        if past:
            cspec = pl.BlockSpec((1, 1, N_KV_HEADS, PAST_LEN, HEAD_DIM),
                                 lambda b, i: (b, j, 0, 0, 0))
            in_specs += [cspec, cspec]
            args += [cache_k, cache_v]
        in_specs += [
            pl.BlockSpec((None, N_MOD, D),
                         lambda b, i: (_mod_group(t0 + b * nq + i, rows), 0, 0)),
            _layer_spec((D, D), j),
        ]
        args += [mods_l, w_out]
        return pl.pallas_call(
            functools.partial(_attn_kernel, past=past, hpb=hpb),
            out_shape=jax.ShapeDtypeStruct((NT, D), F32),
            grid=(nb // nseq, nq),
            in_specs=in_specs,
            out_specs=pl.BlockSpec((rows, D), lambda b, i: (t0 + b * nq + i, 0)),
            scratch_shapes=[pltpu.VMEM((nseq * N_KV_HEADS, past + seq, HEAD_DIM), BF16),
                            pltpu.VMEM((nseq * N_KV_HEADS, VEXT_ROWS, past + seq), BF16),
                            pltpu.VMEM((rows, D), BF16)],
            input_output_aliases={0: 0},
            compiler_params=_params(48, 2),
            name="attn_cached" if past else "attn_ctx",
        )(*args)

    x = call(x, bound_p, *qkv_p, 0, 0, BATCH, SEQ, SEQ, Q_PER_KV, 2)
    return call(x, bound_s, *qkv_s, PAST_LEN, NP, DEC_BATCH, DEC_SEQ, 512, 1, 1)


def _ffn_kernel(*refs, final, tm, ada):
    side_work = ()
    if final:
        x_ref, mod_ref, nw_ref, win_ref, wout_ref, fw_ref, op_ref, os_ref, h_ref, acc_ref = refs
    elif ada:
        (x_ref, mod_ref, nw_ref, win_ref, wout_ref, c_ref, adaw_ref, adab_ref,
         o_ref, modo_ref, h_ref, acc_ref) = refs

        def later_layers_adaln():
            c = c_ref[...]
            sc = (c * jax.nn.sigmoid(c)).astype(BF16)
            modo_ref[...] = _wdot(sc, adaw_ref[...]) + adab_ref[...]

        side_work = [lambda: None] * (N_FF_CHUNKS // 2) + [later_layers_adaln]
    else:
        x_ref, mod_ref, nw_ref, win_ref, wout_ref, o_ref, h_ref, acc_ref = refs
    mod = mod_ref[...]
    x = x_ref[...]
    h_ref[...] = _norm_mod(x, nw_ref[...], mod[3:4], mod[4:5]).astype(BF16)
    _swiglu(h_ref[...], win_ref, wout_ref, acc_ref, side_work)
    y = x + mod[5:6] * acc_ref[...]
    if not final:
        o_ref[...] = y
        return
    y = y * lax.rsqrt(jnp.mean(y * y, axis=-1, keepdims=True) + EPS) * fw_ref[...]
    is_prompt = pl.program_id(0) < NP // tm

    @pl.when(is_prompt)
    def _():
        op_ref[...] = y

    @pl.when(jnp.logical_not(is_prompt))
    def _():
        os_ref[...] = y


def _ffn_layer(x, mods_l, nw, w_in, w_out, layer, final_w=None, adaln_rest=None):
    tm = 512
    final = final_w is not None
    ada = adaln_rest is not None
    n_steps = NT // tm
    in_specs = [
        pl.BlockSpec((tm, D), lambda i: (i, 0)),
        pl.BlockSpec((None, N_MOD, D), lambda i: (_mod_group(i, tm), 0, 0)),
        _const_spec((1, D)),
        _layer_spec((D, 2 * D_FF), layer),
        _layer_spec((D_FF, D), layer),
    ]
    args = [x, mods_l, nw, w_in, w_out]
    if final:
        npt = NP // tm
        in_specs.append(_const_spec((1, D)))
        args.append(final_w)
        out_shape = [jax.ShapeDtypeStruct((NP, D), F32), jax.ShapeDtypeStruct((NS, D), F32)]
        out_specs = [pl.BlockSpec((tm, D), lambda i: (jnp.minimum(i, npt - 1), 0)),
                     pl.BlockSpec((tm, D), lambda i: (jnp.maximum(i - npt, 0), 0))]
    else:
        out_shape = jax.ShapeDtypeStruct((NT, D), F32)
        out_specs = pl.BlockSpec((tm, D), lambda i: (i, 0))
    if ada:
        rest = DEPTH - 1
        per_layer = n_steps // rest
        tn = N_MOD * D // per_layer
        assert per_layer * rest == n_steps and tn * per_layer == N_MOD * D and tn % 128 == 0
        in_specs += [
            _const_spec((8, D)),
            pl.BlockSpec((None, D, tn), lambda i: (1 + i // per_layer, 0, i % per_layer)),
            pl.BlockSpec((None, 1, tn), lambda i: (1 + i // per_layer, 0, i % per_layer)),
        ]
        args += list(adaln_rest)
        out_shape = [out_shape, jax.ShapeDtypeStruct((rest, 8, N_MOD * D), F32)]
        out_specs = [out_specs,
                     pl.BlockSpec((None, 8, tn), lambda i: (i // per_layer, 0, i % per_layer))]
    return pl.pallas_call(
        functools.partial(_ffn_kernel, final=final, tm=tm, ada=ada),
        out_shape=out_shape,
        grid=(n_steps,),
        in_specs=in_specs,
        out_specs=out_specs,
        scratch_shapes=[pltpu.VMEM((tm, D), BF16), pltpu.VMEM((tm, D), F32)],
        compiler_params=_params(56),
        name="ffn_final" if final else "ffn",
    )(*args)


def _rope_tables():
    t = np.arange(DEC_SEQ)
    inv_freq = ROPE_THETA ** (-np.arange(0, ROPE_AXIS_DIM, 2, dtype=np.float64) / ROPE_AXIS_DIM)
    ang_r = (t // GRID_W)[:, None] * inv_freq[None, :]
    ang_c = (t % GRID_W)[:, None] * inv_freq[None, :]
    cos = np.concatenate([np.cos(ang_r), np.cos(ang_c)] * 2, axis=-1)
    sin = np.concatenate([-np.sin(ang_r), -np.sin(ang_c), np.sin(ang_r), np.sin(ang_c)], axis=-1)
    return jnp.asarray(cos, F32), jnp.asarray(sin, F32)


def kernel(x_prompt, x_sample, cache_k, cache_v, c, c_ctx, norm1_w, norm2_w, ada_w, ada_b,
           conv_in_w, conv_w, conv_out_w, pool_w, pool_scale, attn_qkv_w, q_norm_w, k_norm_w,
           attn_out_w, ffn_in_w, ffn_out_w, final_norm_w):
    cvec = jnp.concatenate([c_ctx[None, :], c, jnp.zeros((8 - N_GROUPS, D), F32)], axis=0)
    ada_b3 = ada_b.reshape(DEPTH, 1, N_MOD * D)

    def mod_rows(m):
        return m[:, :N_GROUPS].reshape(m.shape[0], N_GROUPS, N_MOD, D)

    mods = [mod_rows(_adaln(cvec, ada_w, ada_b3, 1))[0]]

    pool_scale3 = pool_scale[:, None, :]
    qn3, kn3 = q_norm_w[:, None, :], k_norm_w[:, None, :]

    x = (x_prompt.reshape(NP, D), x_sample.reshape(NS, D))
    new_k = new_v = None
    for i in range(DEPTH):
        kind, j = i % 3, i // 3
        nw1 = norm1_w[i][None, :]
        if kind == 0:
            x = _conv_layer(x, mods[i], nw1, conv_in_w, conv_w, conv_out_w, j)
        elif kind == 1:
            x = _pool_ffn_layer(x, mods[i], nw1, pool_w, pool_scale3, j,
                                norm2_w[i][None, :], ffn_in_w, ffn_out_w, i)
            continue
        else:
            (qp, kp, vtp, new_k, new_v), qkv_s = _qkv_layer(
                x, mods[i], nw1, attn_qkv_w, qn3, kn3, _rope_tables(), j)
            x = _attn_layer(x, (qp, kp, vtp), qkv_s, cache_k, cache_v, mods[i], attn_out_w,
                            q_norm_w[j], k_norm_w[j], j)
        if i == 0:
            x, rest = _ffn_layer(x, mods[0], norm2_w[0][None, :], ffn_in_w, ffn_out_w, 0,
                                 adaln_rest=(cvec, ada_w, ada_b3))
            mods += list(mod_rows(rest))
            continue
        x = _ffn_layer(x, mods[i], norm2_w[i][None, :], ffn_in_w, ffn_out_w, i,
                       final_norm_w[None, :] if i == DEPTH - 1 else None)

    y_prompt, y_sample = x
    return (y_prompt.reshape(BATCH, SEQ, D), y_sample.reshape(DEC_BATCH, DEC_SEQ, D), new_k, new_v)
```

```python
import functools

import numpy as np
import jax
import jax.numpy as jnp
from jax import lax
from jax.experimental import pallas as pl
from jax.experimental.pallas import tpu as pltpu

D = 1024
BATCH = 16
SEQ = 256
DEPTH = 4
DEC_BATCH = 2
DEC_SEQ = 4096
PAST_LEN = 256
GRID_W = 64
N_HEADS = 8
N_KV_HEADS = 2
HEAD_DIM = 128
Q_PER_KV = N_HEADS // N_KV_HEADS
ROPE_AXIS_DIM = HEAD_DIM // 2
ROPE_THETA = 10000.0
POOL_WINDOWS = (2, 4, 8, 16)
POOL_GROUP_DIM = D // 4
D_FF = 2816
N_MOD = 6
EPS = 1e-6
ATTN_SCALE = HEAD_DIM ** -0.5

NP = BATCH * SEQ
NS = DEC_BATCH * DEC_SEQ
NT = NP + NS
N_GROUPS = 1 + DEC_BATCH

QK_SCALE_LOG2 = ATTN_SCALE * float(np.log2(np.e))
MAX_SAFE_SCORE_BOUND = 60.0
BF16_SUBLANES = 16
VEXT_ROWS = HEAD_DIM + BF16_SUBLANES
HALO = BF16_SUBLANES
FF_CHUNK = 256
N_FF_CHUNKS = D_FF // FF_CHUNK
BF16 = jnp.bfloat16
F32 = jnp.float32


def _dot(a, b):
    return jnp.dot(a, b, preferred_element_type=F32)


def _wdot(a, w):
    return _dot(a, w.astype(BF16))


def _dot_nt(a, b):
    return lax.dot_general(a, b, (((1,), (1,)), ((), ())), preferred_element_type=F32)


def _norm_mod(x, nw, shift, scale):
    y = x * lax.rsqrt(jnp.mean(x * x, axis=-1, keepdims=True) + EPS)
    return (y * nw) * (1.0 + scale) + shift


def _mod_group(tile, tm):
    return jnp.where(tile < NP // tm, 0, 1 + (tile - NP // tm) // (DEC_SEQ // tm))


def _params(vmem_mb, n_axes=1):
    return pltpu.CompilerParams(
        dimension_semantics=("arbitrary",) * n_axes,
        vmem_limit_bytes=vmem_mb * 1024 * 1024)


def _const_spec(shape):
    return pl.BlockSpec(shape, lambda *_: (0,) * len(shape), pipeline_mode=pl.Buffered(1))


def _layer_spec(shape, j):
    return pl.BlockSpec((None,) + tuple(shape), lambda *_: (j,) + (0,) * len(shape),
                        pipeline_mode=pl.Buffered(1))


def _adaln_kernel(c_ref, w_ref, b_ref, o_ref):
    c = c_ref[...]
    sc = (c * jax.nn.sigmoid(c)).astype(BF16)
    o_ref[...] = _dot(sc, w_ref[...].astype(BF16)) + b_ref[...]


def _adaln(cvec, ada_w, ada_b3, n_layers):
    tn = 1536
    return pl.pallas_call(
        _adaln_kernel,
        out_shape=jax.ShapeDtypeStruct((n_layers, 8, N_MOD * D), F32),
        grid=(n_layers, N_MOD * D // tn),
        in_specs=[
            pl.BlockSpec((8, D), lambda l, j: (0, 0)),
            pl.BlockSpec((None, D, tn), lambda l, j: (l, 0, j)),
            pl.BlockSpec((None, 1, tn), lambda l, j: (l, 0, j)),
        ],
        out_specs=pl.BlockSpec((None, 8, tn), lambda l, j: (l, 0, j)),
        compiler_params=_params(40, 2),
        name="adaln",
    )(cvec, ada_w, ada_b3)


def _conv_kernel(*refs, tm, split):
    i = pl.program_id(0)
    if split:
        (pm_ref, pp_ref, pn_ref, sm_ref, sp_ref, sn_ref,
         mod_ref, nw_ref, win_ref, cw_ref, wout_ref, o_ref, h_ref, acc_ref, winb_ref, woutb_ref) = refs
        is_prompt = i < NP // tm
        x = jnp.where(is_prompt, pm_ref[...], sm_ref[...])
        x_prev = jnp.where(is_prompt, pp_ref[...], sp_ref[...])
        x_next = jnp.where(is_prompt, pn_ref[...], sn_ref[...])
    else:
        (x_ref, xp_ref, xn_ref,
         mod_ref, nw_ref, win_ref, cw_ref, wout_ref, o_ref, h_ref, acc_ref, winb_ref, woutb_ref) = refs
        x, x_prev, x_next = x_ref[...], xp_ref[...], xn_ref[...]
    @pl.when(i == 0)
    def _():
        winb_ref[...] = win_ref[...].astype(BF16)
        woutb_ref[...] = wout_ref[...].astype(BF16)

    mod = mod_ref[...]
    nw = nw_ref[...]
    shift, scale, gate = mod[0:1], mod[1:2], mod[2:3]
    h_ref[0:HALO] = _norm_mod(x_prev, nw, shift, scale).astype(BF16)
    h_ref[HALO:HALO + tm] = _norm_mod(x, nw, shift, scale).astype(BF16)
    h_ref[HALO + tm:] = _norm_mod(x_next, nw, shift, scale).astype(BF16)

    seq_len = jnp.where(i < NP // tm, SEQ, DEC_SEQ)
    pos = (i * tm + lax.broadcasted_iota(jnp.int32, (tm, 1), 0)) & (seq_len - 1)
    has_prev = pos != 0
    has_next = pos != seq_len - 1
    cw = cw_ref[...]
    rows = tm + 2 * HALO

    def in_proj(j):
        c0 = j * 256
        return (_dot(h_ref[HALO:HALO + tm], winb_ref[:, c0:c0 + 256]),
                _dot(h_ref[...], winb_ref[:, D + c0:D + c0 + 256]),
                _dot(h_ref[...], winb_ref[:, 2 * D + c0:2 * D + c0 + 256]))

    nxt = in_proj(0)
    for j in range(D // 256):
        c0 = j * 256
        b, cg, u = nxt
        if j + 1 < D // 256:
            nxt = in_proj(j + 1)
        z = cg * u
        z_prev = pltpu.roll(z, 1, axis=0)[HALO:HALO + tm]
        z_next = pltpu.roll(z, rows - 1, axis=0)[HALO:HALO + tm]
        conv = (jnp.where(has_prev, z_prev, 0.0) * cw[0:1, c0:c0 + 256]
                + z[HALO:HALO + tm] * cw[1:2, c0:c0 + 256]
                + jnp.where(has_next, z_next, 0.0) * cw[2:3, c0:c0 + 256])
        part = _dot((b * conv).astype(BF16), woutb_ref[c0:c0 + 256, :])
        if j == 0:
            acc_ref[...] = part
        else:
            acc_ref[...] += part
    o_ref[...] = x + gate * acc_ref[...]


def _halo_specs(tm, nrows, tile_of):
    nh = tm // HALO
    last = nrows // HALO - 1
    return [
        pl.BlockSpec((tm, D), lambda i: (tile_of(i), 0)),
        pl.BlockSpec((HALO, D), lambda i: (jnp.maximum(tile_of(i) * nh - 1, 0), 0)),
        pl.BlockSpec((HALO, D), lambda i: (jnp.minimum((tile_of(i) + 1) * nh, last), 0)),
    ]


def _conv_layer(xs, mods_l, nw, w_in, cw, w_out, j):
    tm = 512
    split = isinstance(xs, tuple)
    if split:
        npt = NP // tm
        x_specs = (_halo_specs(tm, NP, lambda i: jnp.minimum(i, npt - 1))
                   + _halo_specs(tm, NS, lambda i: jnp.maximum(i - npt, 0)))
        x_args = [xs[0]] * 3 + [xs[1]] * 3
    else:
        x_specs = _halo_specs(tm, NT, lambda i: i)
        x_args = [xs] * 3
    return pl.pallas_call(
        functools.partial(_conv_kernel, tm=tm, split=split),
        out_shape=jax.ShapeDtypeStruct((NT, D), F32),
        grid=(NT // tm,),
        in_specs=x_specs + [
            pl.BlockSpec((None, N_MOD, D), lambda i: (_mod_group(i, tm), 0, 0)),
            _const_spec((1, D)),
            _layer_spec((D, 3 * D), j),
            _layer_spec((3, D), j),
            _layer_spec((D, D), j),
        ],
        out_specs=pl.BlockSpec((tm, D), lambda i: (i, 0)),
        scratch_shapes=[pltpu.VMEM((tm + 2 * HALO, D), BF16), pltpu.VMEM((tm, D), F32),
                        pltpu.VMEM((D, 3 * D), BF16), pltpu.VMEM((D, D), BF16)],
        compiler_params=_params(48),
        name="conv_mixer",
    )(*x_args, mods_l, nw, w_in, cw, w_out)


def _swiglu(h, win_ref, wout_ref, acc_ref, side_work=()):
    def in_proj(j):
        c0 = j * FF_CHUNK
        return (_wdot(h, win_ref[:, c0:c0 + FF_CHUNK]),
                _wdot(h, win_ref[:, D_FF + c0:D_FF + c0 + FF_CHUNK]))

    n_chunks = N_FF_CHUNKS
    assert len(side_work) <= n_chunks
    nxt = in_proj(0)
    for j in range(n_chunks):
        c0 = j * FF_CHUNK
        g, u = nxt
        if j + 1 < n_chunks:
            nxt = in_proj(j + 1)
        if j < len(side_work):
            side_work[j]()
        a = (g * jax.nn.sigmoid(g) * u).astype(BF16)
        part = _wdot(a, wout_ref[c0:c0 + FF_CHUNK, :])
        if j == 0:
            acc_ref[...] = part
        else:
            acc_ref[...] += part


def _pool_ffn_kernel(x_ref, xp_ref, xn_ref, modn_ref, modc_ref, nw1_ref, wp_ref, ps_ref, nw2_ref,
                     win_ref, wout_ref, o_ref, hp_ref, st_ref, x1_ref, h2_ref, acc_ref,
                     *, tm, n_tiles):
    s = pl.program_id(0)
    cur = (s + 1) % 2
    new = s % 2

    @pl.when(s == 0)
    def _():
        x1_ref[1] = jnp.zeros((tm, D), F32)
        h2_ref[1] = jnp.zeros((tm, D), BF16)

    t = jnp.minimum(s, n_tiles - 1)
    mod = modn_ref[...]
    nw = nw1_ref[...]
    shift, scale, gate = mod[0:1], mod[1:2], mod[2:3]
    seq_len = jnp.where(t < NP // tm, SEQ, DEC_SEQ)
    pos0 = (t * tm) & (seq_len - 1)
    rows = tm + 2 * HALO

    def row_stats():
        def inv_rms(xv):
            r = lax.rsqrt(jnp.mean(xv * xv, axis=-1, keepdims=True) + EPS)
            return jnp.broadcast_to(r, (xv.shape[0], 128))
        st_ref[0:HALO] = inv_rms(xp_ref[...])
        st_ref[HALO:HALO + tm] = inv_rms(x_ref[...])
        st_ref[HALO + tm:] = inv_rms(xn_ref[...])

    def fill_group(g):
        at_start = pos0 == 0
        at_end = pos0 + tm == seq_len
        for half in range(POOL_GROUP_DIM // 128):
            cs = slice(g * POOL_GROUP_DIM + half * 128, g * POOL_GROUP_DIM + (half + 1) * 128)

            def nm(xv, iv):
                return ((xv * iv) * nw[:, cs]) * (1.0 + scale[:, cs]) + shift[:, cs]

            hp_ref[0:HALO, cs] = jnp.where(at_start, 0.0, nm(xp_ref[:, cs], st_ref[0:HALO]))
            hp_ref[HALO:HALO + tm, cs] = nm(x_ref[:, cs], st_ref[HALO:HALO + tm])
            hp_ref[HALO + tm:, cs] = jnp.where(at_end, 0.0, nm(xn_ref[:, cs], st_ref[HALO + tm:]))

    def pool_group(g):
        w = POOL_WINDOWS[g]
        c0 = g * POOL_GROUP_DIM
        pos = pos0 + lax.broadcasted_iota(jnp.int32, (tm, 1), 0)
        hg = hp_ref[:, c0:c0 + POOL_GROUP_DIM]
        p = hg
        step = 1
        while step < w:
            p = p + pltpu.roll(p, step, axis=0)
            step *= 2
        if w // 2 > 1:
            p = pltpu.roll(p, rows - (w // 2 - 1), axis=0)
        cnt = (jnp.minimum(pos + w // 2, seq_len) - jnp.maximum(pos - w // 2, 0)).astype(F32)
        pooled = p[HALO:HALO + tm] / cnt
        diff = (pooled - hg[HALO:HALO + tm]).astype(BF16)
        mixed = _wdot(diff, wp_ref[g]) * ps_ref[...][:, c0:c0 + POOL_GROUP_DIM]
        x1_ref[new, :, c0:c0 + POOL_GROUP_DIM] = (
            x_ref[:, c0:c0 + POOL_GROUP_DIM] + gate[:, c0:c0 + POOL_GROUP_DIM] * mixed)

    def norm_for_swiglu():
        h2_ref[new] = _norm_mod(x1_ref[new], nw2_ref[...], mod[3:4], mod[4:5]).astype(BF16)

    side_work = [row_stats]
    for g in range(len(POOL_WINDOWS)):
        side_work += [functools.partial(fill_group, g), functools.partial(pool_group, g)]
    side_work.append(norm_for_swiglu)

    modc = modc_ref[...]
    _swiglu(h2_ref[cur], win_ref, wout_ref, acc_ref, side_work)
    o_ref[...] = x1_ref[cur] + modc[5:6] * acc_ref[...]


def _pool_ffn_layer(x, mods_l, nw1, wp, ps, j, nw2, w_in, w_out, layer):
    tm = 256
    n = NT // tm
    return pl.pallas_call(
        functools.partial(_pool_ffn_kernel, tm=tm, n_tiles=n),
        out_shape=jax.ShapeDtypeStruct((NT, D), F32),
        grid=(n + 1,),
        in_specs=_halo_specs(tm, NT, lambda s: jnp.minimum(s, n - 1)) + [
            pl.BlockSpec((None, N_MOD, D),
                         lambda s: (_mod_group(jnp.minimum(s, n - 1), tm), 0, 0)),
            pl.BlockSpec((None, N_MOD, D),
                         lambda s: (_mod_group(jnp.maximum(s - 1, 0), tm), 0, 0)),
            _const_spec((1, D)),
            _layer_spec((4, POOL_GROUP_DIM, POOL_GROUP_DIM), j),
            _layer_spec((1, D), j),
            _const_spec((1, D)),
            _layer_spec((D, 2 * D_FF), layer),
            _layer_spec((D_FF, D), layer),
        ],
        out_specs=pl.BlockSpec((tm, D), lambda s: (jnp.maximum(s - 1, 0), 0)),
        scratch_shapes=[pltpu.VMEM((tm + 2 * HALO, D), F32),
                        pltpu.VMEM((tm + 2 * HALO, 128), F32),
                        pltpu.VMEM((2, tm, D), F32),
                        pltpu.VMEM((2, tm, D), BF16),
                        pltpu.VMEM((tm, D), F32)],
        compiler_params=_params(56),
        name="pool_ffn",
    )(x, x, x, mods_l, mods_l, nw1, wp, ps, nw2, w_in, w_out)


def _head_norm(t, w):
    return t * lax.rsqrt(jnp.mean(t * t, axis=-1, keepdims=True) + EPS) * w


def _pair_halves_layout(a):
    q4 = HEAD_DIM // 4
    lane = lax.broadcasted_iota(jnp.int32, (1, HEAD_DIM), 1)
    from_next = pltpu.roll(a, HEAD_DIM - q4, axis=1)
    from_prev = pltpu.roll(a, q4, axis=1)
    return jnp.where((lane >= q4) & (lane < 2 * q4), from_next,
                     jnp.where((lane >= 2 * q4) & (lane < 3 * q4), from_prev, a))


def _qkv_kernel(*refs, tm, rope):
    if rope:
        (x_ref, mod_ref, nw_ref, w_ref, qn_ref, kn_ref, cos_ref, sin_ref,
         q_ref, k_ref, vt_ref, wb_ref) = refs
    else:
        (x_ref, mod_ref, nw_ref, w_ref, qn_ref, kn_ref,
         q_ref, k_ref, vt_ref, nk_ref, nv_ref) = refs
    n_qk = N_HEADS + N_KV_HEADS
    if rope:
        @pl.when(pl.program_id(0) == 0)
        def _():
            for hd in range(n_qk + N_KV_HEADS):
                cols = slice(hd * HEAD_DIM, (hd + 1) * HEAD_DIM)
                w = w_ref[:, cols]
                wb_ref[:, cols] = (_pair_halves_layout(w) if hd < n_qk else w).astype(BF16)

    mod = mod_ref[...]
    h = _norm_mod(x_ref[...], nw_ref[...], mod[0:1], mod[1:2]).astype(BF16)
    qn = qn_ref[...]
    kn = kn_ref[...]
    if rope:
        qn = _pair_halves_layout(qn)
        kn = _pair_halves_layout(kn)
        cos = cos_ref[...]
        sin = sin_ref[...]

    pair = 2 * HEAD_DIM
    n_pairs = (N_HEADS + 2 * N_KV_HEADS) // 2
    same_head = (lax.broadcasted_iota(jnp.int32, (pair, pair), 0) // HEAD_DIM
                 == lax.broadcasted_iota(jnp.int32, (pair, pair), 1) // HEAD_DIM)
    same_head = jnp.where(same_head, 1.0, 0.0).astype(BF16)

    def head_inv_rms(tt):
        sq = tt * tt
        hi = sq.astype(BF16)
        lo = (sq - hi.astype(F32)).astype(BF16)
        ssq = _dot(hi, same_head) + _dot(lo, same_head)
        return lax.rsqrt(ssq * (1.0 / HEAD_DIM) + EPS)

    def normed(t, inv_rms, w):
        if not rope:
            return _head_norm(t, w)
        t = t * inv_rms * w
        return t * cos + pltpu.roll(t, HEAD_DIM // 2, axis=1) * sin

    def proj(c):
        if rope:
            return _dot(h, wb_ref[:, c * pair:(c + 1) * pair])
        return _wdot(h, w_ref[:, c * pair:(c + 1) * pair])

    nxt = proj(0)
    for c in range(n_pairs):
        cur = nxt
        if c + 1 < n_pairs:
            nxt = proj(c + 1)
        inv = head_inv_rms(cur) if rope and 2 * c < n_qk else None
        for r in range(2):
            t = cur[:, r * HEAD_DIM:(r + 1) * HEAD_DIM]
            inv_r = None if inv is None else inv[:, r * HEAD_DIM:(r + 1) * HEAD_DIM]
            hd = 2 * c + r
            if hd < N_HEADS:
                q_ref[:, hd * HEAD_DIM:(hd + 1) * HEAD_DIM] = (
                    normed(t, inv_r, qn) * QK_SCALE_LOG2).astype(BF16)
            elif hd < N_HEADS + N_KV_HEADS:
                g = hd - N_HEADS
                t = normed(t, inv_r, kn)
                if not rope:
                    for bb in range(tm // SEQ):
                        nk_ref[bb, 0, g] = t[bb * SEQ:(bb + 1) * SEQ]
                k_ref[:, g * HEAD_DIM:(g + 1) * HEAD_DIM] = t.astype(BF16)
            else:
                g = hd - N_HEADS - N_KV_HEADS
                if rope:
                    vt_ref[g] = t.T.astype(BF16)
                else:
                    for bb in range(tm // SEQ):
                        nv_ref[bb, 0, g] = t[bb * SEQ:(bb + 1) * SEQ]
                        vt_ref[bb, g] = t[bb * SEQ:(bb + 1) * SEQ].T.astype(BF16)


def _qkv_layer(x, mods_l, nw, w_qkv, qn, kn, rope_tabs, j):
    tm = 512
    kvd = N_KV_HEADS * HEAD_DIM
    qkv_out = (N_HEADS + 2 * N_KV_HEADS) * HEAD_DIM

    def call(rope, row0, nrows):
        t0 = row0 // tm
        in_specs = [
            pl.BlockSpec((tm, D), lambda i: (t0 + i, 0)),
            pl.BlockSpec((None, N_MOD, D), lambda i: (_mod_group(t0 + i, tm), 0, 0)),
            _const_spec((1, D)),
            _layer_spec((D, qkv_out), j),
            _layer_spec((1, HEAD_DIM), j),
            _layer_spec((1, HEAD_DIM), j),
        ]
        out_shape = [jax.ShapeDtypeStruct((nrows, D), BF16),
                     jax.ShapeDtypeStruct((nrows, kvd), BF16)]
        out_specs = [pl.BlockSpec((tm, D), lambda i: (i, 0)),
                     pl.BlockSpec((tm, kvd), lambda i: (i, 0))]
        args = [x, mods_l, nw, w_qkv, qn, kn]
        if rope:
            tps = DEC_SEQ // tm
            in_specs += [pl.BlockSpec((tm, HEAD_DIM), lambda i: (i % tps, 0)),
                         pl.BlockSpec((tm, HEAD_DIM), lambda i: (i % tps, 0))]
            args += list(rope_tabs)
            out_shape.append(jax.ShapeDtypeStruct((DEC_BATCH, N_KV_HEADS, HEAD_DIM, DEC_SEQ), BF16))
            out_specs.append(pl.BlockSpec((None, N_KV_HEADS, HEAD_DIM, tm),
                                          lambda i: (i // tps, 0, 0, i % tps)))
        else:
            nb = tm // SEQ
            out_shape.append(jax.ShapeDtypeStruct((BATCH, N_KV_HEADS, HEAD_DIM, SEQ), BF16))
            out_specs.append(pl.BlockSpec((nb, N_KV_HEADS, HEAD_DIM, SEQ), lambda i: (i, 0, 0, 0)))
            kv_shape = (BATCH, 1, N_KV_HEADS, SEQ, HEAD_DIM)
            out_shape += [jax.ShapeDtypeStruct(kv_shape, F32)] * 2
            out_specs += [pl.BlockSpec((nb, 1, N_KV_HEADS, SEQ, HEAD_DIM),
                                       lambda i: (i, 0, 0, 0, 0))] * 2
        return pl.pallas_call(
            functools.partial(_qkv_kernel, tm=tm, rope=rope),
            out_shape=out_shape,
            grid=(nrows // tm,),
            in_specs=in_specs,
            out_specs=out_specs,
            scratch_shapes=[pltpu.VMEM((D, qkv_out), BF16)] if rope else [],
            compiler_params=_params(40),
            name="qkv_rope" if rope else "qkv_ctx",
        )(*args)

    return call(False, 0, NP), call(True, NP, NS)


def _attn_kernel(*refs, past, hpb):
    if past:
        (x_ref, bound_ref, q_ref, k_ref, vt_ref, ck_ref, cv_ref, mod_ref, wo_ref,
         o_ref, kall_ref, vext_ref, heads_ref) = refs
    else:
        (x_ref, bound_ref, q_ref, k_ref, vt_ref, mod_ref, wo_ref,
         o_ref, kall_ref, vext_ref, heads_ref) = refs
    nkeys = kall_ref.shape[1]
    new_keys = nkeys - past
    nseq = vt_ref.shape[0]
    tq = q_ref.shape[0] // nseq

    @pl.when(pl.program_id(1) == 0)
    def _():
        for sq in range(nseq):
            for g in range(N_KV_HEADS):
                u = sq * N_KV_HEADS + g
                if past:
                    kall_ref[u, 0:past] = _pair_halves_layout(ck_ref[0, 0, g]).astype(BF16)
                    vext_ref[u, 0:HEAD_DIM, 0:past] = cv_ref[0, 0, g].T.astype(BF16)
                kall_ref[u, past:] = k_ref[sq * new_keys:(sq + 1) * new_keys,
                                           g * HEAD_DIM:(g + 1) * HEAD_DIM]
                vext_ref[u, 0:HEAD_DIM, past:] = vt_ref[sq, g]
                vext_ref[u, HEAD_DIM:] = jnp.ones((VEXT_ROWS - HEAD_DIM, nkeys), BF16)

    blocks = [(sq, list(range(h0, h0 + hpb)))
              for sq in range(nseq) for h0 in range(0, N_HEADS, hpb)]

    def scores(block):
        sq, heads = block
        qb = [q_ref[sq * tq:(sq + 1) * tq, hd * HEAD_DIM:(hd + 1) * HEAD_DIM] for hd in heads]
        qb = qb[0] if hpb == 1 else jnp.concatenate(qb, axis=0)
        return _dot_nt(kall_ref[sq * N_KV_HEADS + heads[0] // Q_PER_KV], qb)

    def attend(shift_of):
        st_next = scores(blocks[0])
        for b, (sq, heads) in enumerate(blocks):
            st = st_next
            if b + 1 < len(blocks):
                st_next = scores(blocks[b + 1])
            pt = jnp.exp2(st - shift_of(st)).astype(BF16)
            ot = _dot(vext_ref[sq * N_KV_HEADS + heads[0] // Q_PER_KV], pt)
            o = (ot[0:HEAD_DIM] / ot[HEAD_DIM:HEAD_DIM + 1]).T
            for r, hd in enumerate(heads):
                heads_ref[sq * tq:(sq + 1) * tq, hd * HEAD_DIM:(hd + 1) * HEAD_DIM] = (
                    o[r * tq:(r + 1) * tq].astype(BF16))

    bound = bound_ref[0]
    use_bound = bound <= MAX_SAFE_SCORE_BOUND

    @pl.when(use_bound)
    def _():
        attend(lambda st: bound)

    @pl.when(jnp.logical_not(use_bound))
    def _():
        attend(lambda st: jnp.max(st, axis=0, keepdims=True))

    gate = mod_ref[...][2:3]
    o_ref[...] = x_ref[...] + gate * _wdot(heads_ref[...], wo_ref[...])


def _attn_layer(x, qkv_p, qkv_s, cache_k, cache_v, mods_l, w_out, qn, kn, j):
    kvd = N_KV_HEADS * HEAD_DIM

    q_len = QK_SCALE_LOG2 * HEAD_DIM ** 0.5 * jnp.max(jnp.abs(qn))
    k_len = HEAD_DIM ** 0.5 * jnp.max(jnp.abs(kn))
    cache_len = jnp.sqrt(jnp.max(jnp.sum(jnp.square(cache_k[:, j]), axis=-1)))
    slack = 1.0 + 2.0 ** -6
    bound_p = (q_len * k_len * slack).reshape(1)
    bound_s = (q_len * jnp.maximum(k_len, cache_len) * slack).reshape(1)

    def call(x, bound, q, k, vt, past, row0, nb, seq, tq, hpb, nseq):
        rows = tq * nseq
        t0 = row0 // rows
        nq = seq // tq
        assert nseq == 1 or nq == 1
        in_specs = [
            pl.BlockSpec((rows, D), lambda b, i: (t0 + b * nq + i, 0)),
            pl.BlockSpec(memory_space=pltpu.SMEM),
            pl.BlockSpec((rows, D), lambda b, i: (b * nq + i, 0)),
            pl.BlockSpec((nseq * seq, kvd), lambda b, i: (b, 0)),
            pl.BlockSpec((nseq, N_KV_HEADS, HEAD_DIM, seq), lambda b, i: (b, 0, 0, 0)),
        ]
        args = [x, bound, q, k, vt]
        if past:
            cspec = pl.BlockSpec((1, 1, N_KV_HEADS, PAST_LEN, HEAD_DIM),
                                 lambda b, i: (b, j, 0, 0, 0))
            in_specs += [cspec, cspec]
            args += [cache_k, cache_v]
        in_specs += [
            pl.BlockSpec((None, N_MOD, D),
                         lambda b, i: (_mod_group(t0 + b * nq + i, rows), 0, 0)),
            _layer_spec((D, D), j),
        ]
        args += [mods_l, w_out]
        return pl.pallas_call(
            functools.partial(_attn_kernel, past=past, hpb=hpb),
            out_shape=jax.ShapeDtypeStruct((NT, D), F32),
            grid=(nb // nseq, nq),
            in_specs=in_specs,
            out_specs=pl.BlockSpec((rows, D), lambda b, i: (t0 + b * nq + i, 0)),
            scratch_shapes=[pltpu.VMEM((nseq * N_KV_HEADS, past + seq, HEAD_DIM), BF16),
                            pltpu.VMEM((nseq * N_KV_HEADS, VEXT_ROWS, past + seq), BF16),
                            pltpu.VMEM((rows, D), BF16)],
            input_output_aliases={0: 0},
            compiler_params=_params(48, 2),
            name="attn_cached" if past else "attn_ctx",
        )(*args)

    x = call(x, bound_p, *qkv_p, 0, 0, BATCH, SEQ, SEQ, Q_PER_KV, 2)
    return call(x, bound_s, *qkv_s, PAST_LEN, NP, DEC_BATCH, DEC_SEQ, 512, 1, 1)


def _ffn_kernel(*refs, final, tm, ada):
    side_work = ()
    if final:
        x_ref, mod_ref, nw_ref, win_ref, wout_ref, fw_ref, op_ref, os_ref, h_ref, acc_ref = refs
    elif ada:
        (x_ref, mod_ref, nw_ref, win_ref, wout_ref, c_ref, adaw_ref, adab_ref,
         o_ref, modo_ref, h_ref, acc_ref) = refs

        def later_layers_adaln():
            c = c_ref[...]
            sc = (c * jax.nn.sigmoid(c)).astype(BF16)
            modo_ref[...] = _wdot(sc, adaw_ref[...]) + adab_ref[...]

        side_work = [lambda: None] * (N_FF_CHUNKS // 2) + [later_layers_adaln]
    else:
        x_ref, mod_ref, nw_ref, win_ref, wout_ref, o_ref, h_ref, acc_ref = refs
    mod = mod_ref[...]
    x = x_ref[...]
    h_ref[...] = _norm_mod(x, nw_ref[...], mod[3:4], mod[4:5]).astype(BF16)
    _swiglu(h_ref[...], win_ref, wout_ref, acc_ref, side_work)
    y = x + mod[5:6] * acc_ref[...]
    if not final:
        o_ref[...] = y
        return
    y = y * lax.rsqrt(jnp.mean(y * y, axis=-1, keepdims=True) + EPS) * fw_ref[...]
    is_prompt = pl.program_id(0) < NP // tm

    @pl.when(is_prompt)
    def _():
        op_ref[...] = y

    @pl.when(jnp.logical_not(is_prompt))
    def _():
        os_ref[...] = y


def _ffn_layer(x, mods_l, nw, w_in, w_out, layer, final_w=None, adaln_rest=None):
    tm = 512
    final = final_w is not None
    ada = adaln_rest is not None
    n_steps = NT // tm
    in_specs = [
        pl.BlockSpec((tm, D), lambda i: (i, 0)),
        pl.BlockSpec((None, N_MOD, D), lambda i: (_mod_group(i, tm), 0, 0)),
        _const_spec((1, D)),
        _layer_spec((D, 2 * D_FF), layer),
        _layer_spec((D_FF, D), layer),
    ]
    args = [x, mods_l, nw, w_in, w_out]
    if final:
        npt = NP // tm
        in_specs.append(_const_spec((1, D)))
        args.append(final_w)
        out_shape = [jax.ShapeDtypeStruct((NP, D), F32), jax.ShapeDtypeStruct((NS, D), F32)]
        out_specs = [pl.BlockSpec((tm, D), lambda i: (jnp.minimum(i, npt - 1), 0)),
                     pl.BlockSpec((tm, D), lambda i: (jnp.maximum(i - npt, 0), 0))]
    else:
        out_shape = jax.ShapeDtypeStruct((NT, D), F32)
        out_specs = pl.BlockSpec((tm, D), lambda i: (i, 0))
    if ada:
        rest = DEPTH - 1
        per_layer = n_steps // rest
        tn = N_MOD * D // per_layer
        assert per_layer * rest == n_steps and tn * per_layer == N_MOD * D and tn % 128 == 0
        in_specs += [
            _const_spec((8, D)),
            pl.BlockSpec((None, D, tn), lambda i: (1 + i // per_layer, 0, i % per_layer)),
            pl.BlockSpec((None, 1, tn), lambda i: (1 + i // per_layer, 0, i % per_layer)),
        ]
        args += list(adaln_rest)
        out_shape = [out_shape, jax.ShapeDtypeStruct((rest, 8, N_MOD * D), F32)]
        out_specs = [out_specs,
                     pl.BlockSpec((None, 8, tn), lambda i: (i // per_layer, 0, i % per_layer))]
    return pl.pallas_call(
        functools.partial(_ffn_kernel, final=final, tm=tm, ada=ada),
        out_shape=out_shape,
        grid=(n_steps,),
        in_specs=in_specs,
        out_specs=out_specs,
        scratch_shapes=[pltpu.VMEM((tm, D), BF16), pltpu.VMEM((tm, D), F32)],
        compiler_params=_params(56),
        name="ffn_final" if final else "ffn",
    )(*args)


def _rope_tables():
    t = np.arange(DEC_SEQ)
    inv_freq = ROPE_THETA ** (-np.arange(0, ROPE_AXIS_DIM, 2, dtype=np.float64) / ROPE_AXIS_DIM)
    ang_r = (t // GRID_W)[:, None] * inv_freq[None, :]
    ang_c = (t % GRID_W)[:, None] * inv_freq[None, :]
    cos = np.concatenate([np.cos(ang_r), np.cos(ang_c)] * 2, axis=-1)
    sin = np.concatenate([-np.sin(ang_r), -np.sin(ang_c), np.sin(ang_r), np.sin(ang_c)], axis=-1)
    return jnp.asarray(cos, F32), jnp.asarray(sin, F32)


def kernel(x_prompt, x_sample, cache_k, cache_v, c, c_ctx, norm1_w, norm2_w, ada_w, ada_b,
           conv_in_w, conv_w, conv_out_w, pool_w, pool_scale, attn_qkv_w, q_norm_w, k_norm_w,
           attn_out_w, ffn_in_w, ffn_out_w, final_norm_w):
    cvec = jnp.concatenate([c_ctx[None, :], c, jnp.zeros((8 - N_GROUPS, D), F32)], axis=0)
    ada_b3 = ada_b.reshape(DEPTH, 1, N_MOD * D)

    def mod_rows(m):
        return m[:, :N_GROUPS].reshape(m.shape[0], N_GROUPS, N_MOD, D)

    mods = [mod_rows(_adaln(cvec, ada_w, ada_b3, 1))[0]]

    pool_scale3 = pool_scale[:, None, :]
    qn3, kn3 = q_norm_w[:, None, :], k_norm_w[:, None, :]

    x = (x_prompt.reshape(NP, D), x_sample.reshape(NS, D))
    new_k = new_v = None
    for i in range(DEPTH):
        kind, j = i % 3, i // 3
        nw1 = norm1_w[i][None, :]
        if kind == 0:
            x = _conv_layer(x, mods[i], nw1, conv_in_w, conv_w, conv_out_w, j)
        elif kind == 1:
            x = _pool_ffn_layer(x, mods[i], nw1, pool_w, pool_scale3, j,
                                norm2_w[i][None, :], ffn_in_w, ffn_out_w, i)
            continue
        else:
            (qp, kp, vtp, new_k, new_v), qkv_s = _qkv_layer(
                x, mods[i], nw1, attn_qkv_w, qn3, kn3, _rope_tables(), j)
            x = _attn_layer(x, (qp, kp, vtp), qkv_s, cache_k, cache_v, mods[i], attn_out_w,
                            q_norm_w[j], k_norm_w[j], j)
        if i == 0:
            x, rest = _ffn_layer(x, mods[0], norm2_w[0][None, :], ffn_in_w, ffn_out_w, 0,
                                 adaln_rest=(cvec, ada_w, ada_b3))
            mods += list(mod_rows(rest))
            continue
        x = _ffn_layer(x, mods[i], norm2_w[i][None, :], ffn_in_w, ffn_out_w, i,
                       final_norm_w[None, :] if i == DEPTH - 1 else None)

    y_prompt, y_sample = x
    return (y_prompt.reshape(BATCH, SEQ, D), y_sample.reshape(DEC_BATCH, DEC_SEQ, D), new_k, new_v)
```

```python
import functools

import numpy as np
import jax
import jax.numpy as jnp
from jax import lax
from jax.experimental import pallas as pl
from jax.experimental.pallas import tpu as pltpu

D = 1024
BATCH = 16
SEQ = 256
DEPTH = 4
DEC_BATCH = 2
DEC_SEQ = 4096
PAST_LEN = 256
GRID_W = 64
N_HEADS = 8
N_KV_HEADS = 2
HEAD_DIM = 128
Q_PER_KV = N_HEADS // N_KV_HEADS
ROPE_AXIS_DIM = HEAD_DIM // 2
ROPE_THETA = 10000.0
POOL_WINDOWS = (2, 4, 8, 16)
POOL_GROUP_DIM = D // 4
D_FF = 2816
N_MOD = 6
EPS = 1e-6
ATTN_SCALE = HEAD_DIM ** -0.5

NP = BATCH * SEQ
NS = DEC_BATCH * DEC_SEQ
NT = NP + NS
N_GROUPS = 1 + DEC_BATCH

QK_SCALE_LOG2 = ATTN_SCALE * float(np.log2(np.e))
MAX_SAFE_SCORE_BOUND = 60.0
BF16_SUBLANES = 16
VEXT_ROWS = HEAD_DIM + BF16_SUBLANES
HALO = BF16_SUBLANES
FF_CHUNK = 256
N_FF_CHUNKS = D_FF // FF_CHUNK
CONV_CHUNK = 256
LANES = 128
BF16 = jnp.bfloat16
F32 = jnp.float32


def _dot(a, b):
    return jnp.dot(a, b, preferred_element_type=F32)


def _wdot(a, w):
    return _dot(a, w.astype(BF16))


def _dot_nt(a, b):
    return lax.dot_general(a, b, (((1,), (1,)), ((), ())), preferred_element_type=F32)


def _norm_mod(x, nw, shift, scale):
    y = x * lax.rsqrt(jnp.mean(x * x, axis=-1, keepdims=True) + EPS)
    return (y * nw) * (1.0 + scale) + shift


def _mod_group(tile, tm):
    return jnp.where(tile < NP // tm, 0, 1 + (tile - NP // tm) // (DEC_SEQ // tm))


def _params(vmem_mb, n_axes=1):
    return pltpu.CompilerParams(
        dimension_semantics=("arbitrary",) * n_axes,
        vmem_limit_bytes=vmem_mb * 1024 * 1024)


def _const_spec(shape):
    return pl.BlockSpec(shape, lambda *_: (0,) * len(shape), pipeline_mode=pl.Buffered(1))


def _layer_spec(shape, j):
    return pl.BlockSpec((None,) + tuple(shape), lambda *_: (j,) + (0,) * len(shape),
                        pipeline_mode=pl.Buffered(1))


def _adaln_kernel(c_ref, w_ref, b_ref, o_ref):
    c = c_ref[...]
    sc = (c * jax.nn.sigmoid(c)).astype(BF16)
    o_ref[...] = _dot(sc, w_ref[...].astype(BF16)) + b_ref[...]


def _adaln(cvec, ada_w, ada_b3, n_layers):
    tn = 1536
    return pl.pallas_call(
        _adaln_kernel,
        out_shape=jax.ShapeDtypeStruct((n_layers, 8, N_MOD * D), F32),
        grid=(n_layers, N_MOD * D // tn),
        in_specs=[
            pl.BlockSpec((8, D), lambda l, j: (0, 0)),
            pl.BlockSpec((None, D, tn), lambda l, j: (l, 0, j)),
            pl.BlockSpec((None, 1, tn), lambda l, j: (l, 0, j)),
        ],
        out_specs=pl.BlockSpec((None, 8, tn), lambda l, j: (l, 0, j)),
        compiler_params=_params(40, 2),
        name="adaln",
    )(cvec, ada_w, ada_b3)


def _conv_kernel(*refs, tm, split):
    i = pl.program_id(0)
    if split:
        (pm_ref, pp_ref, pn_ref, sm_ref, sp_ref, sn_ref,
         mod_ref, nw_ref, win_ref, cw_ref, wout_ref, o_ref, h_ref, acc_ref) = refs
        is_prompt = i < NP // tm
        x = jnp.where(is_prompt, pm_ref[...], sm_ref[...])
        x_prev = jnp.where(is_prompt, pp_ref[...], sp_ref[...])
        x_next = jnp.where(is_prompt, pn_ref[...], sn_ref[...])
    else:
        (x_ref, xp_ref, xn_ref,
         mod_ref, nw_ref, win_ref, cw_ref, wout_ref, o_ref, h_ref, acc_ref) = refs
        x, x_prev, x_next = x_ref[...], xp_ref[...], xn_ref[...]
    mod = mod_ref[...]
    nw = nw_ref[...]
    shift, scale, gate = mod[0:1], mod[1:2], mod[2:3]
    half = HALO // 2
    h_ref[0:tm] = _norm_mod(x, nw, shift, scale).astype(BF16)
    h_ref[tm:] = jnp.concatenate([_norm_mod(x_next, nw, shift, scale)[0:half],
                                  _norm_mod(x_prev, nw, shift, scale)[half:]], axis=0).astype(BF16)

    seq_len = jnp.where(i < NP // tm, SEQ, DEC_SEQ)
    pos = (i * tm + lax.broadcasted_iota(jnp.int32, (tm, 1), 0)) & (seq_len - 1)
    has_prev = pos != 0
    has_next = pos != seq_len - 1
    cw = cw_ref[...]
    rows = tm + HALO

    def in_proj(j):
        c0 = j * CONV_CHUNK
        return (_wdot(h_ref[0:tm], win_ref[:, c0:c0 + CONV_CHUNK]),
                _wdot(h_ref[...], win_ref[:, D + c0:D + c0 + CONV_CHUNK]),
                _wdot(h_ref[...], win_ref[:, 2 * D + c0:2 * D + c0 + CONV_CHUNK]))

    nxt = in_proj(0)
    for j in range(D // CONV_CHUNK):
        c0 = j * CONV_CHUNK
        b, cg, u = nxt
        if j + 1 < D // CONV_CHUNK:
            nxt = in_proj(j + 1)
        z = cg * u
        z_prev = pltpu.roll(z, 1, axis=0)[0:tm]
        z_next = pltpu.roll(z, rows - 1, axis=0)[0:tm]
        conv = (jnp.where(has_prev, z_prev, 0.0) * cw[0:1, c0:c0 + CONV_CHUNK]
                + z[0:tm] * cw[1:2, c0:c0 + CONV_CHUNK]
                + jnp.where(has_next, z_next, 0.0) * cw[2:3, c0:c0 + CONV_CHUNK])
        part = _wdot((b * conv).astype(BF16), wout_ref[c0:c0 + CONV_CHUNK, :])
        if j == 0:
            acc_ref[...] = part
        else:
            acc_ref[...] += part
    o_ref[...] = x + gate * acc_ref[...]


def _halo_specs(tm, nrows, tile_of):
    nh = tm // HALO
    last = nrows // HALO - 1
    return [
        pl.BlockSpec((tm, D), lambda i: (tile_of(i), 0)),
        pl.BlockSpec((HALO, D), lambda i: (jnp.maximum(tile_of(i) * nh - 1, 0), 0)),
        pl.BlockSpec((HALO, D), lambda i: (jnp.minimum((tile_of(i) + 1) * nh, last), 0)),
    ]


def _conv_layer(xs, mods_l, nw, w_in, cw, w_out, j):
    tm = 512
    split = isinstance(xs, tuple)
    if split:
        npt = NP // tm
        x_specs = (_halo_specs(tm, NP, lambda i: jnp.minimum(i, npt - 1))
                   + _halo_specs(tm, NS, lambda i: jnp.maximum(i - npt, 0)))
        x_args = [xs[0]] * 3 + [xs[1]] * 3
    else:
        x_specs = _halo_specs(tm, NT, lambda i: i)
        x_args = [xs] * 3
    return pl.pallas_call(
        functools.partial(_conv_kernel, tm=tm, split=split),
        out_shape=jax.ShapeDtypeStruct((NT, D), F32),
        grid=(NT // tm,),
        in_specs=x_specs + [
            pl.BlockSpec((None, N_MOD, D), lambda i: (_mod_group(i, tm), 0, 0)),
            _const_spec((1, D)),
            _layer_spec((D, 3 * D), j),
            _layer_spec((3, D), j),
            _layer_spec((D, D), j),
        ],
        out_specs=pl.BlockSpec((tm, D), lambda i: (i, 0)),
        scratch_shapes=[pltpu.VMEM((tm + HALO, D), BF16), pltpu.VMEM((tm, D), F32)],
        compiler_params=_params(48),
        name="conv_mixer",
    )(*x_args, mods_l, nw, w_in, cw, w_out)


def _swiglu(h, win_ref, wout_ref, acc_ref, side_work=()):
    def in_proj(j):
        c0 = j * FF_CHUNK
        return (_wdot(h, win_ref[:, c0:c0 + FF_CHUNK]),
                _wdot(h, win_ref[:, D_FF + c0:D_FF + c0 + FF_CHUNK]))

    n_chunks = N_FF_CHUNKS
    assert len(side_work) <= n_chunks
    nxt = in_proj(0)
    for j in range(n_chunks):
        c0 = j * FF_CHUNK
        g, u = nxt
        if j + 1 < n_chunks:
            nxt = in_proj(j + 1)
        if j < len(side_work):
            side_work[j]()
        a = (g * jax.nn.sigmoid(g) * u).astype(BF16)
        part = _wdot(a, wout_ref[c0:c0 + FF_CHUNK, :])
        if j == 0:
            acc_ref[...] = part
        else:
            acc_ref[...] += part


def _pool_ffn_kernel(x_ref, xp_ref, xn_ref, modn_ref, modc_ref, nw1_ref, wp_ref, ps_ref, nw2_ref,
                     win_ref, wout_ref, o_ref, hp_ref, st_ref, x1_ref, h2_ref, acc_ref,
                     *, tm, n_tiles):
    s = pl.program_id(0)
    cur = (s + 1) % 2
    new = s % 2

    t = jnp.minimum(s, n_tiles - 1)
    mod = modn_ref[...]
    nw = nw1_ref[...]
    shift, scale, gate = mod[0:1], mod[1:2], mod[2:3]
    seq_len = jnp.where(t < NP // tm, SEQ, DEC_SEQ)
    pos0 = (t * tm) & (seq_len - 1)
    rows = tm + 2 * HALO

    def row_stats():
        def inv_rms(xv):
            r = lax.rsqrt(jnp.mean(xv * xv, axis=-1, keepdims=True) + EPS)
            return jnp.broadcast_to(r, (xv.shape[0], LANES))
        st_ref[0:HALO] = inv_rms(xp_ref[...])
        st_ref[HALO:HALO + tm] = inv_rms(x_ref[...])
        st_ref[HALO + tm:] = inv_rms(xn_ref[...])

    def fill_group(g):
        at_start = pos0 == 0
        at_end = pos0 + tm == seq_len
        for half in range(POOL_GROUP_DIM // LANES):
            cs = slice(g * POOL_GROUP_DIM + half * LANES, g * POOL_GROUP_DIM + (half + 1) * LANES)

            def nm(xv, iv):
                return ((xv * iv) * nw[:, cs]) * (1.0 + scale[:, cs]) + shift[:, cs]

            hp_ref[0:HALO, cs] = jnp.where(at_start, 0.0, nm(xp_ref[:, cs], st_ref[0:HALO]))
            hp_ref[HALO:HALO + tm, cs] = nm(x_ref[:, cs], st_ref[HALO:HALO + tm])
            hp_ref[HALO + tm:, cs] = jnp.where(at_end, 0.0, nm(xn_ref[:, cs], st_ref[HALO + tm:]))

    def pool_group(g):
        w = POOL_WINDOWS[g]
        c0 = g * POOL_GROUP_DIM
        pos = pos0 + lax.broadcasted_iota(jnp.int32, (tm, 1), 0)
        hg = hp_ref[:, c0:c0 + POOL_GROUP_DIM]
        p = hg + pltpu.roll(hg, 1, axis=0)
        step = 1
        while 2 * step < w:
            p = pltpu.roll(p, step, axis=0) + pltpu.roll(p, rows - step, axis=0)
            step *= 2
        cnt = (jnp.minimum(pos + w // 2, seq_len) - jnp.maximum(pos - w // 2, 0)).astype(F32)
        pooled = p[HALO:HALO + tm] / cnt
        diff = (pooled - hg[HALO:HALO + tm]).astype(BF16)
        mixed = _wdot(diff, wp_ref[g]) * ps_ref[...][:, c0:c0 + POOL_GROUP_DIM]
        x1_ref[new, :, c0:c0 + POOL_GROUP_DIM] = (
            x_ref[:, c0:c0 + POOL_GROUP_DIM] + gate[:, c0:c0 + POOL_GROUP_DIM] * mixed)

    def norm_for_swiglu():
        h2_ref[new] = _norm_mod(x1_ref[new], nw2_ref[...], mod[3:4], mod[4:5]).astype(BF16)

    side_work = [row_stats]
    for g in range(len(POOL_WINDOWS)):
        side_work += [functools.partial(fill_group, g), functools.partial(pool_group, g)]
    side_work.append(norm_for_swiglu)

    @pl.when(s == 0)
    def _():
        for piece in side_work:
            piece()
        o_ref[...] = jnp.zeros((tm, D), F32)

    @pl.when(s != 0)
    def _():
        modc = modc_ref[...]
        _swiglu(h2_ref[cur], win_ref, wout_ref, acc_ref, side_work)
        o_ref[...] = x1_ref[cur] + modc[5:6] * acc_ref[...]


def _pool_ffn_layer(x, mods_l, nw1, wp, ps, j, nw2, w_in, w_out, layer):
    tm = 256
    n = NT // tm
    return pl.pallas_call(
        functools.partial(_pool_ffn_kernel, tm=tm, n_tiles=n),
        out_shape=jax.ShapeDtypeStruct((NT, D), F32),
        grid=(n + 1,),
        in_specs=_halo_specs(tm, NT, lambda s: jnp.minimum(s, n - 1)) + [
            pl.BlockSpec((None, N_MOD, D),
                         lambda s: (_mod_group(jnp.minimum(s, n - 1), tm), 0, 0)),
            pl.BlockSpec((None, N_MOD, D),
                         lambda s: (_mod_group(jnp.maximum(s - 1, 0), tm), 0, 0)),
            _const_spec((1, D)),
            _layer_spec((4, POOL_GROUP_DIM, POOL_GROUP_DIM), j),
            _layer_spec((1, D), j),
            _const_spec((1, D)),
            _layer_spec((D, 2 * D_FF), layer),
            _layer_spec((D_FF, D), layer),
        ],
        out_specs=pl.BlockSpec((tm, D), lambda s: (jnp.maximum(s - 1, 0), 0)),
        scratch_shapes=[pltpu.VMEM((tm + 2 * HALO, D), F32),
                        pltpu.VMEM((tm + 2 * HALO, LANES), F32),
                        pltpu.VMEM((2, tm, D), F32),
                        pltpu.VMEM((2, tm, D), BF16),
                        pltpu.VMEM((tm, D), F32)],
        compiler_params=_params(56),
        name="pool_ffn",
    )(x, x, x, mods_l, mods_l, nw1, wp, ps, nw2, w_in, w_out)


def _head_norm(t, w):
    return t * lax.rsqrt(jnp.mean(t * t, axis=-1, keepdims=True) + EPS) * w


def _pair_halves_layout(a):
    q4 = HEAD_DIM // 4
    lane = lax.broadcasted_iota(jnp.int32, (1, HEAD_DIM), 1)
    from_next = pltpu.roll(a, HEAD_DIM - q4, axis=1)
    from_prev = pltpu.roll(a, q4, axis=1)
    return jnp.where((lane >= q4) & (lane < 2 * q4), from_next,
                     jnp.where((lane >= 2 * q4) & (lane < 3 * q4), from_prev, a))


def _qkv_kernel(*refs, tm, rope):
    if rope:
        (x_ref, mod_ref, nw_ref, w_ref, qn_ref, kn_ref, cos_ref, sin_ref,
         q_ref, k_ref, vt_ref, wb_ref) = refs
    else:
        (x_ref, mod_ref, nw_ref, w_ref, qn_ref, kn_ref,
         q_ref, k_ref, vt_ref, nk_ref, nv_ref) = refs
    n_qk = N_HEADS + N_KV_HEADS
    if rope:
        @pl.when(pl.program_id(0) == 0)
        def _():
            for hd in range(n_qk + N_KV_HEADS):
                cols = slice(hd * HEAD_DIM, (hd + 1) * HEAD_DIM)
                w = w_ref[:, cols]
                wb_ref[:, cols] = (_pair_halves_layout(w) if hd < n_qk else w).astype(BF16)

    mod = mod_ref[...]
    h = _norm_mod(x_ref[...], nw_ref[...], mod[0:1], mod[1:2]).astype(BF16)
    qn = qn_ref[...]
    kn = kn_ref[...]
    if rope:
        qn = _pair_halves_layout(qn)
        kn = _pair_halves_layout(kn)
        cos = cos_ref[...]
        sin = sin_ref[...]

    pair = 2 * HEAD_DIM
    n_pairs = (N_HEADS + 2 * N_KV_HEADS) // 2
    same_head = (lax.broadcasted_iota(jnp.int32, (pair, pair), 0) // HEAD_DIM
                 == lax.broadcasted_iota(jnp.int32, (pair, pair), 1) // HEAD_DIM)
    same_head = jnp.where(same_head, 1.0, 0.0).astype(BF16)

    def head_inv_rms(tt):
        sq = tt * tt
        hi = sq.astype(BF16)
        lo = (sq - hi.astype(F32)).astype(BF16)
        ssq = _dot(hi, same_head) + _dot(lo, same_head)
        return lax.rsqrt(ssq * (1.0 / HEAD_DIM) + EPS)

    def normed(t, inv_rms, w):
        if not rope:
            return _head_norm(t, w)
        t = t * inv_rms * w
        return t * cos + pltpu.roll(t, HEAD_DIM // 2, axis=1) * sin

    def proj(c):
        if rope:
            return _dot(h, wb_ref[:, c * pair:(c + 1) * pair])
        return _wdot(h, w_ref[:, c * pair:(c + 1) * pair])

    nxt = proj(0)
    for c in range(n_pairs):
        cur = nxt
        if c + 1 < n_pairs:
            nxt = proj(c + 1)
        inv = head_inv_rms(cur) if rope and 2 * c < n_qk else None
        for r in range(2):
            t = cur[:, r * HEAD_DIM:(r + 1) * HEAD_DIM]
            inv_r = None if inv is None else inv[:, r * HEAD_DIM:(r + 1) * HEAD_DIM]
            hd = 2 * c + r
            if hd < N_HEADS:
                q_ref[:, hd * HEAD_DIM:(hd + 1) * HEAD_DIM] = (
                    normed(t, inv_r, qn) * QK_SCALE_LOG2).astype(BF16)
            elif hd < N_HEADS + N_KV_HEADS:
                g = hd - N_HEADS
                t = normed(t, inv_r, kn)
                if not rope:
                    for bb in range(tm // SEQ):
                        nk_ref[bb, 0, g] = t[bb * SEQ:(bb + 1) * SEQ]
                k_ref[:, g * HEAD_DIM:(g + 1) * HEAD_DIM] = t.astype(BF16)
            else:
                g = hd - N_HEADS - N_KV_HEADS
                if rope:
                    vt_ref[g] = t.T.astype(BF16)
                else:
                    for bb in range(tm // SEQ):
                        nv_ref[bb, 0, g] = t[bb * SEQ:(bb + 1) * SEQ]
                        vt_ref[bb, g] = t[bb * SEQ:(bb + 1) * SEQ].T.astype(BF16)


def _qkv_layer(x, mods_l, nw, w_qkv, qn, kn, rope_tabs, j):
    tm = 512
    kvd = N_KV_HEADS * HEAD_DIM
    qkv_out = (N_HEADS + 2 * N_KV_HEADS) * HEAD_DIM

    def call(rope, row0, nrows):
        t0 = row0 // tm
        in_specs = [
            pl.BlockSpec((tm, D), lambda i: (t0 + i, 0)),
            pl.BlockSpec((None, N_MOD, D), lambda i: (_mod_group(t0 + i, tm), 0, 0)),
            _const_spec((1, D)),
            _layer_spec((D, qkv_out), j),
            _layer_spec((1, HEAD_DIM), j),
            _layer_spec((1, HEAD_DIM), j),
        ]
        out_shape = [jax.ShapeDtypeStruct((nrows, D), BF16),
                     jax.ShapeDtypeStruct((nrows, kvd), BF16)]
        out_specs = [pl.BlockSpec((tm, D), lambda i: (i, 0)),
                     pl.BlockSpec((tm, kvd), lambda i: (i, 0))]
        args = [x, mods_l, nw, w_qkv, qn, kn]
        if rope:
            tps = DEC_SEQ // tm
            in_specs += [pl.BlockSpec((tm, HEAD_DIM), lambda i: (i % tps, 0)),
                         pl.BlockSpec((tm, HEAD_DIM), lambda i: (i % tps, 0))]
            args += list(rope_tabs)
            out_shape.append(jax.ShapeDtypeStruct((DEC_BATCH, N_KV_HEADS, HEAD_DIM, DEC_SEQ), BF16))
            out_specs.append(pl.BlockSpec((None, N_KV_HEADS, HEAD_DIM, tm),
                                          lambda i: (i // tps, 0, 0, i % tps)))
        else:
            nb = tm // SEQ
            out_shape.append(jax.ShapeDtypeStruct((BATCH, N_KV_HEADS, HEAD_DIM, SEQ), BF16))
            out_specs.append(pl.BlockSpec((nb, N_KV_HEADS, HEAD_DIM, SEQ), lambda i: (i, 0, 0, 0)))
            kv_shape = (BATCH, 1, N_KV_HEADS, SEQ, HEAD_DIM)
            out_shape += [jax.ShapeDtypeStruct(kv_shape, F32)] * 2
            out_specs += [pl.BlockSpec((nb, 1, N_KV_HEADS, SEQ, HEAD_DIM),
                                       lambda i: (i, 0, 0, 0, 0))] * 2
        return pl.pallas_call(
            functools.partial(_qkv_kernel, tm=tm, rope=rope),
            out_shape=out_shape,
            grid=(nrows // tm,),
            in_specs=in_specs,
            out_specs=out_specs,
            scratch_shapes=[pltpu.VMEM((D, qkv_out), BF16)] if rope else [],
            compiler_params=_params(40),
            name="qkv_rope" if rope else "qkv_ctx",
        )(*args)

    return call(False, 0, NP), call(True, NP, NS)


def _attn_kernel(*refs, past, hpb):
    if past:
        (x_ref, bound_ref, q_ref, k_ref, vt_ref, ck_ref, cv_ref, mod_ref, wo_ref,
         o_ref, kall_ref, vext_ref, heads_ref) = refs
    else:
        (x_ref, bound_ref, q_ref, k_ref, vt_ref, mod_ref, wo_ref,
         o_ref, kall_ref, vext_ref, heads_ref) = refs
    nkeys = kall_ref.shape[1]
    new_keys = nkeys - past
    nseq = vt_ref.shape[0]
    tq = q_ref.shape[0] // nseq

    @pl.when(pl.program_id(1) == 0)
    def _():
        for sq in range(nseq):
            for g in range(N_KV_HEADS):
                u = sq * N_KV_HEADS + g
                if past:
                    kall_ref[u, 0:past] = _pair_halves_layout(ck_ref[0, 0, g]).astype(BF16)
                    vext_ref[u, 0:HEAD_DIM, 0:past] = cv_ref[0, 0, g].T.astype(BF16)
                kall_ref[u, past:] = k_ref[sq * new_keys:(sq + 1) * new_keys,
                                           g * HEAD_DIM:(g + 1) * HEAD_DIM]
                vext_ref[u, 0:HEAD_DIM, past:] = vt_ref[sq, g]
                vext_ref[u, HEAD_DIM:] = jnp.ones((VEXT_ROWS - HEAD_DIM, nkeys), BF16)

    blocks = [(sq, list(range(h0, h0 + hpb)))
              for sq in range(nseq) for h0 in range(0, N_HEADS, hpb)]

    def scores(block):
        sq, heads = block
        qb = [q_ref[sq * tq:(sq + 1) * tq, hd * HEAD_DIM:(hd + 1) * HEAD_DIM] for hd in heads]
        qb = qb[0] if hpb == 1 else jnp.concatenate(qb, axis=0)
        return _dot_nt(kall_ref[sq * N_KV_HEADS + heads[0] // Q_PER_KV], qb)

    def attend(shift_of):
        st_next = scores(blocks[0])
        for b, (sq, heads) in enumerate(blocks):
            st = st_next
            if b + 1 < len(blocks):
                st_next = scores(blocks[b + 1])
            pt = jnp.exp2(st - shift_of(st)).astype(BF16)
            ot = _dot(vext_ref[sq * N_KV_HEADS + heads[0] // Q_PER_KV], pt)
            o = (ot[0:HEAD_DIM] / ot[HEAD_DIM:HEAD_DIM + 1]).T
            for r, hd in enumerate(heads):
                heads_ref[sq * tq:(sq + 1) * tq, hd * HEAD_DIM:(hd + 1) * HEAD_DIM] = (
                    o[r * tq:(r + 1) * tq].astype(BF16))
        gate = mod_ref[...][2:3]
        o_ref[...] = x_ref[...] + gate * _wdot(heads_ref[...], wo_ref[...])

    bound = bound_ref[0]
    use_bound = bound <= MAX_SAFE_SCORE_BOUND

    @pl.when(use_bound)
    def _():
        attend(lambda st: bound)

    @pl.when(jnp.logical_not(use_bound))
    def _():
        attend(lambda st: jnp.max(st, axis=0, keepdims=True))


def _attn_layer(x, qkv_p, qkv_s, cache_k, cache_v, mods_l, w_out, qn, kn, j):
    kvd = N_KV_HEADS * HEAD_DIM

    q_len = QK_SCALE_LOG2 * HEAD_DIM ** 0.5 * jnp.max(jnp.abs(qn))
    k_len = HEAD_DIM ** 0.5 * jnp.max(jnp.abs(kn))
    cache_len = jnp.sqrt(jnp.max(jnp.sum(jnp.square(cache_k[:, j]), axis=-1)))
    slack = 1.0 + 2.0 ** -6
    bound_p = (q_len * k_len * slack).reshape(1)
    bound_s = (q_len * jnp.maximum(k_len, cache_len) * slack).reshape(1)

    def call(x, bound, q, k, vt, past, row0, nb, seq, tq, hpb, nseq):
        rows = tq * nseq
        t0 = row0 // rows
        nq = seq // tq
        assert nseq == 1 or nq == 1
        in_specs = [
            pl.BlockSpec((rows, D), lambda b, i: (t0 + b * nq + i, 0)),
            pl.BlockSpec(memory_space=pltpu.SMEM),
            pl.BlockSpec((rows, D), lambda b, i: (b * nq + i, 0)),
            pl.BlockSpec((nseq * seq, kvd), lambda b, i: (b, 0)),
            pl.BlockSpec((nseq, N_KV_HEADS, HEAD_DIM, seq), lambda b, i: (b, 0, 0, 0)),
        ]
        args = [x, bound, q, k, vt]
        if past:
            cspec = pl.BlockSpec((1, 1, N_KV_HEADS, PAST_LEN, HEAD_DIM),
                                 lambda b, i: (b, j, 0, 0, 0))
            in_specs += [cspec, cspec]
            args += [cache_k, cache_v]
        in_specs += [
            pl.BlockSpec((None, N_MOD, D),
                         lambda b, i: (_mod_group(t0 + b * nq + i, rows), 0, 0)),
            _layer_spec((D, D), j),
        ]
        args += [mods_l, w_out]
        return pl.pallas_call(
            functools.partial(_attn_kernel, past=past, hpb=hpb),
            out_shape=jax.ShapeDtypeStruct((NT, D), F32),
            grid=(nb // nseq, nq),
            in_specs=in_specs,
            out_specs=pl.BlockSpec((rows, D), lambda b, i: (t0 + b * nq + i, 0)),
            scratch_shapes=[pltpu.VMEM((nseq * N_KV_HEADS, past + seq, HEAD_DIM), BF16),
                            pltpu.VMEM((nseq * N_KV_HEADS, VEXT_ROWS, past + seq), BF16),
                            pltpu.VMEM((rows, D), BF16)],
            input_output_aliases={0: 0},
            compiler_params=_params(48, 2),
            name="attn_cached" if past else "attn_ctx",
        )(*args)

    x = call(x, bound_p, *qkv_p, 0, 0, BATCH, SEQ, SEQ, Q_PER_KV, 2)
    return call(x, bound_s, *qkv_s, PAST_LEN, NP, DEC_BATCH, DEC_SEQ, 512, 1, 1)


def _ffn_kernel(*refs, final, tm, ada):
    side_work = ()
    if final:
        x_ref, mod_ref, nw_ref, win_ref, wout_ref, fw_ref, op_ref, os_ref, h_ref, acc_ref = refs
    elif ada:
        (x_ref, mod_ref, nw_ref, win_ref, wout_ref, c_ref, adaw_ref, adab_ref,
         o_ref, modo_ref, h_ref, acc_ref) = refs

        def later_layers_adaln():
            c = c_ref[...]
            sc = (c * jax.nn.sigmoid(c)).astype(BF16)
            modo_ref[...] = _wdot(sc, adaw_ref[...]) + adab_ref[...]

        side_work = [lambda: None] * (N_FF_CHUNKS // 2) + [later_layers_adaln]
    else:
        x_ref, mod_ref, nw_ref, win_ref, wout_ref, o_ref, h_ref, acc_ref = refs
    mod = mod_ref[...]
    x = x_ref[...]
    h_ref[...] = _norm_mod(x, nw_ref[...], mod[3:4], mod[4:5]).astype(BF16)
    _swiglu(h_ref[...], win_ref, wout_ref, acc_ref, side_work)
    y = x + mod[5:6] * acc_ref[...]
    if not final:
        o_ref[...] = y
        return
    y = y * lax.rsqrt(jnp.mean(y * y, axis=-1, keepdims=True) + EPS) * fw_ref[...]
    is_prompt = pl.program_id(0) < NP // tm

    @pl.when(is_prompt)
    def _():
        op_ref[...] = y

    @pl.when(jnp.logical_not(is_prompt))
    def _():
        os_ref[...] = y


def _ffn_layer(x, mods_l, nw, w_in, w_out, layer, final_w=None, adaln_rest=None):
    tm = 512
    final = final_w is not None
    ada = adaln_rest is not None
    n_steps = NT // tm
    in_specs = [
        pl.BlockSpec((tm, D), lambda i: (i, 0)),
        pl.BlockSpec((None, N_MOD, D), lambda i: (_mod_group(i, tm), 0, 0)),
        _const_spec((1, D)),
        _layer_spec((D, 2 * D_FF), layer),
        _layer_spec((D_FF, D), layer),
    ]
    args = [x, mods_l, nw, w_in, w_out]
    if final:
        npt = NP // tm
        in_specs.append(_const_spec((1, D)))
        args.append(final_w)
        out_shape = [jax.ShapeDtypeStruct((NP, D), F32), jax.ShapeDtypeStruct((NS, D), F32)]
        out_specs = [pl.BlockSpec((tm, D), lambda i: (jnp.minimum(i, npt - 1), 0)),
                     pl.BlockSpec((tm, D), lambda i: (jnp.maximum(i - npt, 0), 0))]
    else:
        out_shape = jax.ShapeDtypeStruct((NT, D), F32)
        out_specs = pl.BlockSpec((tm, D), lambda i: (i, 0))
    if ada:
        rest = DEPTH - 1
        per_layer = n_steps // rest
        tn = N_MOD * D // per_layer
        assert per_layer * rest == n_steps and tn * per_layer == N_MOD * D and tn % LANES == 0
        in_specs += [
            _const_spec((8, D)),
            pl.BlockSpec((None, D, tn), lambda i: (1 + i // per_layer, 0, i % per_layer)),
            pl.BlockSpec((None, 1, tn), lambda i: (1 + i // per_layer, 0, i % per_layer)),
        ]
        args += list(adaln_rest)
        out_shape = [out_shape, jax.ShapeDtypeStruct((rest, 8, N_MOD * D), F32)]
        out_specs = [out_specs,
                     pl.BlockSpec((None, 8, tn), lambda i: (i // per_layer, 0, i % per_layer))]
    return pl.pallas_call(
        functools.partial(_ffn_kernel, final=final, tm=tm, ada=ada),
        out_shape=out_shape,
        grid=(n_steps,),
        in_specs=in_specs,
        out_specs=out_specs,
        scratch_shapes=[pltpu.VMEM((tm, D), BF16), pltpu.VMEM((tm, D), F32)],
        compiler_params=_params(56),
        name="ffn_final" if final else "ffn",
    )(*args)


def _rope_tables():
    t = np.arange(DEC_SEQ)
    inv_freq = ROPE_THETA ** (-np.arange(0, ROPE_AXIS_DIM, 2, dtype=np.float64) / ROPE_AXIS_DIM)
    ang_r = (t // GRID_W)[:, None] * inv_freq[None, :]
    ang_c = (t % GRID_W)[:, None] * inv_freq[None, :]
    cos = np.concatenate([np.cos(ang_r), np.cos(ang_c)] * 2, axis=-1)
    sin = np.concatenate([-np.sin(ang_r), -np.sin(ang_c), np.sin(ang_r), np.sin(ang_c)], axis=-1)
    return jnp.asarray(cos, F32), jnp.asarray(sin, F32)


def kernel(x_prompt, x_sample, cache_k, cache_v, c, c_ctx, norm1_w, norm2_w, ada_w, ada_b,
           conv_in_w, conv_w, conv_out_w, pool_w, pool_scale, attn_qkv_w, q_norm_w, k_norm_w,
           attn_out_w, ffn_in_w, ffn_out_w, final_norm_w):
    cvec = jnp.concatenate([c_ctx[None, :], c, jnp.zeros((8 - N_GROUPS, D), F32)], axis=0)
    ada_b3 = ada_b.reshape(DEPTH, 1, N_MOD * D)

    def mod_rows(m):
        return m[:, :N_GROUPS].reshape(m.shape[0], N_GROUPS, N_MOD, D)

    mods = [mod_rows(_adaln(cvec, ada_w, ada_b3, 1))[0]]

    pool_scale3 = pool_scale[:, None, :]
    qn3, kn3 = q_norm_w[:, None, :], k_norm_w[:, None, :]

    x = (x_prompt.reshape(NP, D), x_sample.reshape(NS, D))
    new_k = new_v = None
    for i in range(DEPTH):
        kind, j = i % 3, i // 3
        nw1 = norm1_w[i][None, :]
        if kind == 0:
            x = _conv_layer(x, mods[i], nw1, conv_in_w, conv_w, conv_out_w, j)
        elif kind == 1:
            x = _pool_ffn_layer(x, mods[i], nw1, pool_w, pool_scale3, j,
                                norm2_w[i][None, :], ffn_in_w, ffn_out_w, i)
            continue
        else:
            (qp, kp, vtp, new_k, new_v), qkv_s = _qkv_layer(
                x, mods[i], nw1, attn_qkv_w, qn3, kn3, _rope_tables(), j)
            x = _attn_layer(x, (qp, kp, vtp), qkv_s, cache_k, cache_v, mods[i], attn_out_w,
                            q_norm_w[j], k_norm_w[j], j)
        if i == 0:
            x, rest = _ffn_layer(x, mods[0], norm2_w[0][None, :], ffn_in_w, ffn_out_w, 0,
                                 adaln_rest=(cvec, ada_w, ada_b3))
            mods += list(mod_rows(rest))
            continue
        x = _ffn_layer(x, mods[i], norm2_w[i][None, :], ffn_in_w, ffn_out_w, i,
                       final_norm_w[None, :] if i == DEPTH - 1 else None)

    y_prompt, y_sample = x
    return (y_prompt.reshape(BATCH, SEQ, D), y_sample.reshape(DEC_BATCH, DEC_SEQ, D), new_k, new_v)
```

```python
import functools

import numpy as np
import jax
import jax.numpy as jnp
from jax import lax
from jax.experimental import pallas as pl
from jax.experimental.pallas import tpu as pltpu

D = 1024
BATCH = 16
SEQ = 256
DEPTH = 4
DEC_BATCH = 2
DEC_SEQ = 4096
PAST_LEN = 256
GRID_W = 64
N_HEADS = 8
N_KV_HEADS = 2
HEAD_DIM = 128
Q_PER_KV = N_HEADS // N_KV_HEADS
ROPE_AXIS_DIM = HEAD_DIM // 2
ROPE_THETA = 10000.0
POOL_WINDOWS = (2, 4, 8, 16)
POOL_GROUP_DIM = D // 4
D_FF = 2816
N_MOD = 6
EPS = 1e-6
ATTN_SCALE = HEAD_DIM ** -0.5

NP = BATCH * SEQ
NS = DEC_BATCH * DEC_SEQ
NT = NP + NS
N_GROUPS = 1 + DEC_BATCH

QK_SCALE_LOG2 = ATTN_SCALE * float(np.log2(np.e))
MAX_SAFE_SCORE_BOUND = 60.0
BF16_SUBLANES = 16
VEXT_ROWS = HEAD_DIM + BF16_SUBLANES
HALO = BF16_SUBLANES
FF_CHUNK = 256
N_FF_CHUNKS = D_FF // FF_CHUNK
CONV_CHUNK = 256
LANES = 128
BF16 = jnp.bfloat16
F32 = jnp.float32


def _dot(a, b):
    return jnp.dot(a, b, preferred_element_type=F32)


def _wdot(a, w):
    return _dot(a, w.astype(BF16))


def _dot_nt(a, b):
    return lax.dot_general(a, b, (((1,), (1,)), ((), ())), preferred_element_type=F32)


def _norm_mod(x, nw, shift, scale):
    y = x * lax.rsqrt(jnp.mean(x * x, axis=-1, keepdims=True) + EPS)
    return (y * nw) * (1.0 + scale) + shift


def _mod_group(tile, tm):
    return jnp.where(tile < NP // tm, 0, 1 + (tile - NP // tm) // (DEC_SEQ // tm))


def _params(vmem_mb, n_axes=1):
    return pltpu.CompilerParams(
        dimension_semantics=("arbitrary",) * n_axes,
        vmem_limit_bytes=vmem_mb * 1024 * 1024)


def _const_spec(shape):
    return pl.BlockSpec(shape, lambda *_: (0,) * len(shape), pipeline_mode=pl.Buffered(1))


def _layer_spec(shape, j):
    return pl.BlockSpec((None,) + tuple(shape), lambda *_: (j,) + (0,) * len(shape),
                        pipeline_mode=pl.Buffered(1))


def _adaln_kernel(c_ref, w_ref, b_ref, o_ref):
    c = c_ref[...]
    sc = (c * jax.nn.sigmoid(c)).astype(BF16)
    o_ref[...] = _dot(sc, w_ref[...].astype(BF16)) + b_ref[...]


def _adaln(cvec, ada_w, ada_b3, n_layers):
    tn = 1536
    return pl.pallas_call(
        _adaln_kernel,
        out_shape=jax.ShapeDtypeStruct((n_layers, 8, N_MOD * D), F32),
        grid=(n_layers, N_MOD * D // tn),
        in_specs=[
            pl.BlockSpec((8, D), lambda l, j: (0, 0)),
            pl.BlockSpec((None, D, tn), lambda l, j: (l, 0, j)),
            pl.BlockSpec((None, 1, tn), lambda l, j: (l, 0, j)),
        ],
        out_specs=pl.BlockSpec((None, 8, tn), lambda l, j: (l, 0, j)),
        compiler_params=_params(40, 2),
        name="adaln",
    )(cvec, ada_w, ada_b3)


def _conv_kernel(*refs, tm, split):
    i = pl.program_id(0)
    if split:
        (pm_ref, pp_ref, pn_ref, sm_ref, sp_ref, sn_ref,
         mod_ref, nw_ref, win_ref, cw_ref, wout_ref, o_ref, h_ref, acc_ref) = refs
        is_prompt = i < NP // tm
        x = jnp.where(is_prompt, pm_ref[...], sm_ref[...])
        x_prev = jnp.where(is_prompt, pp_ref[...], sp_ref[...])
        x_next = jnp.where(is_prompt, pn_ref[...], sn_ref[...])
    else:
        (x_ref, xp_ref, xn_ref,
         mod_ref, nw_ref, win_ref, cw_ref, wout_ref, o_ref, h_ref, acc_ref) = refs
        x, x_prev, x_next = x_ref[...], xp_ref[...], xn_ref[...]
    mod = mod_ref[...]
    nw = nw_ref[...]
    shift, scale, gate = mod[0:1], mod[1:2], mod[2:3]
    half = HALO // 2
    h_ref[0:tm] = _norm_mod(x, nw, shift, scale).astype(BF16)
    h_ref[tm:] = jnp.concatenate([_norm_mod(x_next, nw, shift, scale)[0:half],
                                  _norm_mod(x_prev, nw, shift, scale)[half:]], axis=0).astype(BF16)

    seq_len = jnp.where(i < NP // tm, SEQ, DEC_SEQ)
    pos = (i * tm + lax.broadcasted_iota(jnp.int32, (tm, 1), 0)) & (seq_len - 1)
    has_prev = pos != 0
    has_next = pos != seq_len - 1
    cw = cw_ref[...]
    rows = tm + HALO

    def in_proj(j):
        c0 = j * CONV_CHUNK
        return (_wdot(h_ref[0:tm], win_ref[:, c0:c0 + CONV_CHUNK]),
                _wdot(h_ref[...], win_ref[:, D + c0:D + c0 + CONV_CHUNK]),
                _wdot(h_ref[...], win_ref[:, 2 * D + c0:2 * D + c0 + CONV_CHUNK]))

    nxt = in_proj(0)
    for j in range(D // CONV_CHUNK):
        c0 = j * CONV_CHUNK
        b, cg, u = nxt
        if j + 1 < D // CONV_CHUNK:
            nxt = in_proj(j + 1)
        z = cg * u
        z_prev = pltpu.roll(z, 1, axis=0)[0:tm]
        z_next = pltpu.roll(z, rows - 1, axis=0)[0:tm]
        conv = (jnp.where(has_prev, z_prev, 0.0) * cw[0:1, c0:c0 + CONV_CHUNK]
                + z[0:tm] * cw[1:2, c0:c0 + CONV_CHUNK]
                + jnp.where(has_next, z_next, 0.0) * cw[2:3, c0:c0 + CONV_CHUNK])
        part = _wdot((b * conv).astype(BF16), wout_ref[c0:c0 + CONV_CHUNK, :])
        if j == 0:
            acc_ref[...] = part
        else:
            acc_ref[...] += part
    o_ref[...] = x + gate * acc_ref[...]


def _halo_specs(tm, nrows, tile_of):
    nh = tm // HALO
    last = nrows // HALO - 1
    return [
        pl.BlockSpec((tm, D), lambda i: (tile_of(i), 0)),
        pl.BlockSpec((HALO, D), lambda i: (jnp.maximum(tile_of(i) * nh - 1, 0), 0)),
        pl.BlockSpec((HALO, D), lambda i: (jnp.minimum((tile_of(i) + 1) * nh, last), 0)),
    ]


def _conv_layer(xs, mods_l, nw, w_in, cw, w_out, j):
    tm = 512
    split = isinstance(xs, tuple)
    if split:
        npt = NP // tm
        x_specs = (_halo_specs(tm, NP, lambda i: jnp.minimum(i, npt - 1))
                   + _halo_specs(tm, NS, lambda i: jnp.maximum(i - npt, 0)))
        x_args = [xs[0]] * 3 + [xs[1]] * 3
    else:
        x_specs = _halo_specs(tm, NT, lambda i: i)
        x_args = [xs] * 3
    return pl.pallas_call(
        functools.partial(_conv_kernel, tm=tm, split=split),
        out_shape=jax.ShapeDtypeStruct((NT, D), F32),
        grid=(NT // tm,),
        in_specs=x_specs + [
            pl.BlockSpec((None, N_MOD, D), lambda i: (_mod_group(i, tm), 0, 0)),
            _const_spec((1, D)),
            _layer_spec((D, 3 * D), j),
            _layer_spec((3, D), j),
            _layer_spec((D, D), j),
        ],
        out_specs=pl.BlockSpec((tm, D), lambda i: (i, 0)),
        scratch_shapes=[pltpu.VMEM((tm + HALO, D), BF16), pltpu.VMEM((tm, D), F32)],
        compiler_params=_params(48),
        name="conv_mixer",
    )(*x_args, mods_l, nw, w_in, cw, w_out)


def _swiglu(h, win_ref, wout_ref, acc_ref, side_work=()):
    def in_proj(j):
        c0 = j * FF_CHUNK
        return (_wdot(h, win_ref[:, c0:c0 + FF_CHUNK]),
                _wdot(h, win_ref[:, D_FF + c0:D_FF + c0 + FF_CHUNK]))

    n_chunks = N_FF_CHUNKS
    assert len(side_work) <= n_chunks
    nxt = in_proj(0)
    for j in range(n_chunks):
        c0 = j * FF_CHUNK
        g, u = nxt
        if j + 1 < n_chunks:
            nxt = in_proj(j + 1)
        if j < len(side_work):
            side_work[j]()
        a = (g * jax.nn.sigmoid(g) * u).astype(BF16)
        part = _wdot(a, wout_ref[c0:c0 + FF_CHUNK, :])
        if j == 0:
            acc_ref[...] = part
        else:
            acc_ref[...] += part


def _pool_ffn_kernel(x_ref, xp_ref, xn_ref, modn_ref, modc_ref, nw1_ref, wp_ref, ps_ref, nw2_ref,
                     win_ref, wout_ref, o_ref, hp_ref, st_ref, x1_ref, h2_ref, acc_ref,
                     *, tm, n_tiles):
    s = pl.program_id(0)
    cur = (s + 1) % 2
    new = s % 2

    @pl.when(s == 0)
    def _():
        x1_ref[1] = jnp.zeros((tm, D), F32)
        h2_ref[1] = jnp.zeros((tm, D), BF16)

    t = jnp.minimum(s, n_tiles - 1)
    mod = modn_ref[...]
    nw = nw1_ref[...]
    shift, scale, gate = mod[0:1], mod[1:2], mod[2:3]
    seq_len = jnp.where(t < NP // tm, SEQ, DEC_SEQ)
    pos0 = (t * tm) & (seq_len - 1)
    rows = tm + 2 * HALO

    def row_stats():
        def inv_rms(xv):
            r = lax.rsqrt(jnp.mean(xv * xv, axis=-1, keepdims=True) + EPS)
            return jnp.broadcast_to(r, (xv.shape[0], LANES))
        st_ref[0:HALO] = inv_rms(xp_ref[...])
        st_ref[HALO:HALO + tm] = inv_rms(x_ref[...])
        st_ref[HALO + tm:] = inv_rms(xn_ref[...])

    def fill_group(g):
        at_start = pos0 == 0
        at_end = pos0 + tm == seq_len
        for half in range(POOL_GROUP_DIM // LANES):
            cs = slice(g * POOL_GROUP_DIM + half * LANES, g * POOL_GROUP_DIM + (half + 1) * LANES)

            def nm(xv, iv):
                return ((xv * iv) * nw[:, cs]) * (1.0 + scale[:, cs]) + shift[:, cs]

            hp_ref[0:HALO, cs] = jnp.where(at_start, 0.0, nm(xp_ref[:, cs], st_ref[0:HALO]))
            hp_ref[HALO:HALO + tm, cs] = nm(x_ref[:, cs], st_ref[HALO:HALO + tm])
            hp_ref[HALO + tm:, cs] = jnp.where(at_end, 0.0, nm(xn_ref[:, cs], st_ref[HALO + tm:]))

    def pool_group(g):
        w = POOL_WINDOWS[g]
        c0 = g * POOL_GROUP_DIM
        pos = pos0 + lax.broadcasted_iota(jnp.int32, (tm, 1), 0)
        hg = hp_ref[:, c0:c0 + POOL_GROUP_DIM]
        p = hg + pltpu.roll(hg, 1, axis=0)
        step = 1
        while 2 * step < w:
            p = pltpu.roll(p, step, axis=0) + pltpu.roll(p, rows - step, axis=0)
            step *= 2
        cnt = (jnp.minimum(pos + w // 2, seq_len) - jnp.maximum(pos - w // 2, 0)).astype(F32)
        pooled = p[HALO:HALO + tm] / cnt
        diff = (pooled - hg[HALO:HALO + tm]).astype(BF16)
        mixed = _wdot(diff, wp_ref[g]) * ps_ref[...][:, c0:c0 + POOL_GROUP_DIM]
        x1_ref[new, :, c0:c0 + POOL_GROUP_DIM] = (
            x_ref[:, c0:c0 + POOL_GROUP_DIM] + gate[:, c0:c0 + POOL_GROUP_DIM] * mixed)

    def norm_for_swiglu():
        h2_ref[new] = _norm_mod(x1_ref[new], nw2_ref[...], mod[3:4], mod[4:5]).astype(BF16)

    side_work = [row_stats]
    for g in range(len(POOL_WINDOWS)):
        side_work += [functools.partial(fill_group, g), functools.partial(pool_group, g)]
    side_work.append(norm_for_swiglu)

    modc = modc_ref[...]
    _swiglu(h2_ref[cur], win_ref, wout_ref, acc_ref, side_work)
    o_ref[...] = x1_ref[cur] + modc[5:6] * acc_ref[...]


def _pool_ffn_layer(x, mods_l, nw1, wp, ps, j, nw2, w_in, w_out, layer):
    tm = 256
    n = NT // tm
    return pl.pallas_call(
        functools.partial(_pool_ffn_kernel, tm=tm, n_tiles=n),
        out_shape=jax.ShapeDtypeStruct((NT, D), F32),
        grid=(n + 1,),
        in_specs=_halo_specs(tm, NT, lambda s: jnp.minimum(s, n - 1)) + [
            pl.BlockSpec((None, N_MOD, D),
                         lambda s: (_mod_group(jnp.minimum(s, n - 1), tm), 0, 0)),
            pl.BlockSpec((None, N_MOD, D),
                         lambda s: (_mod_group(jnp.maximum(s - 1, 0), tm), 0, 0)),
            _const_spec((1, D)),
            _layer_spec((4, POOL_GROUP_DIM, POOL_GROUP_DIM), j),
            _layer_spec((1, D), j),
            _const_spec((1, D)),
            _layer_spec((D, 2 * D_FF), layer),
            _layer_spec((D_FF, D), layer),
        ],
        out_specs=pl.BlockSpec((tm, D), lambda s: (jnp.maximum(s - 1, 0), 0)),
        scratch_shapes=[pltpu.VMEM((tm + 2 * HALO, D), F32),
                        pltpu.VMEM((tm + 2 * HALO, LANES), F32),
                        pltpu.VMEM((2, tm, D), F32),
                        pltpu.VMEM((2, tm, D), BF16),
                        pltpu.VMEM((tm, D), F32)],
        compiler_params=_params(56),
        name="pool_ffn",
    )(x, x, x, mods_l, mods_l, nw1, wp, ps, nw2, w_in, w_out)


def _head_norm(t, w):
    return t * lax.rsqrt(jnp.mean(t * t, axis=-1, keepdims=True) + EPS) * w


def _pair_halves_layout(a):
    q4 = HEAD_DIM // 4
    lane = lax.broadcasted_iota(jnp.int32, (1, HEAD_DIM), 1)
    from_next = pltpu.roll(a, HEAD_DIM - q4, axis=1)
    from_prev = pltpu.roll(a, q4, axis=1)
    return jnp.where((lane >= q4) & (lane < 2 * q4), from_next,
                     jnp.where((lane >= 2 * q4) & (lane < 3 * q4), from_prev, a))


def _qkv_kernel(*refs, tm, rope):
    if rope:
        (x_ref, mod_ref, nw_ref, w_ref, qn_ref, kn_ref, cos_ref, sin_ref,
         q_ref, k_ref, vt_ref, wb_ref) = refs
    else:
        (x_ref, mod_ref, nw_ref, w_ref, qn_ref, kn_ref,
         q_ref, k_ref, vt_ref, nk_ref, nv_ref) = refs
    n_qk = N_HEADS + N_KV_HEADS
    if rope:
        @pl.when(pl.program_id(0) == 0)
        def _():
            for hd in range(n_qk + N_KV_HEADS):
                cols = slice(hd * HEAD_DIM, (hd + 1) * HEAD_DIM)
                w = w_ref[:, cols]
                wb_ref[:, cols] = (_pair_halves_layout(w) if hd < n_qk else w).astype(BF16)

    mod = mod_ref[...]
    h = _norm_mod(x_ref[...], nw_ref[...], mod[0:1], mod[1:2]).astype(BF16)
    qn = qn_ref[...]
    kn = kn_ref[...]
    if rope:
        qn = _pair_halves_layout(qn)
        kn = _pair_halves_layout(kn)
        cos = cos_ref[...]
        sin = sin_ref[...]

    pair = 2 * HEAD_DIM
    n_pairs = (N_HEADS + 2 * N_KV_HEADS) // 2
    same_head = (lax.broadcasted_iota(jnp.int32, (pair, pair), 0) // HEAD_DIM
                 == lax.broadcasted_iota(jnp.int32, (pair, pair), 1) // HEAD_DIM)
    same_head = jnp.where(same_head, 1.0, 0.0).astype(BF16)

    def head_inv_rms(tt):
        sq = tt * tt
        hi = sq.astype(BF16)
        lo = (sq - hi.astype(F32)).astype(BF16)
        ssq = _dot(hi, same_head) + _dot(lo, same_head)
        return lax.rsqrt(ssq * (1.0 / HEAD_DIM) + EPS)

    def normed(t, inv_rms, w):
        if not rope:
            return _head_norm(t, w)
        t = t * inv_rms * w
        return t * cos + pltpu.roll(t, HEAD_DIM // 2, axis=1) * sin

    def proj(c):
        if rope:
            return _dot(h, wb_ref[:, c * pair:(c + 1) * pair])
        return _wdot(h, w_ref[:, c * pair:(c + 1) * pair])

    nxt = proj(0)
    for c in range(n_pairs):
        cur = nxt
        if c + 1 < n_pairs:
            nxt = proj(c + 1)
        inv = head_inv_rms(cur) if rope and 2 * c < n_qk else None
        for r in range(2):
            t = cur[:, r * HEAD_DIM:(r + 1) * HEAD_DIM]
            inv_r = None if inv is None else inv[:, r * HEAD_DIM:(r + 1) * HEAD_DIM]
            hd = 2 * c + r
            if hd < N_HEADS:
                q_ref[:, hd * HEAD_DIM:(hd + 1) * HEAD_DIM] = (
                    normed(t, inv_r, qn) * QK_SCALE_LOG2).astype(BF16)
            elif hd < N_HEADS + N_KV_HEADS:
                g = hd - N_HEADS
                t = normed(t, inv_r, kn)
                if not rope:
                    for bb in range(tm // SEQ):
                        nk_ref[bb, 0, g] = t[bb * SEQ:(bb + 1) * SEQ]
                k_ref[:, g * HEAD_DIM:(g + 1) * HEAD_DIM] = t.astype(BF16)
            else:
                g = hd - N_HEADS - N_KV_HEADS
                if rope:
                    vt_ref[g] = t.T.astype(BF16)
                else:
                    for bb in range(tm // SEQ):
                        nv_ref[bb, 0, g] = t[bb * SEQ:(bb + 1) * SEQ]
                        vt_ref[bb, g] = t[bb * SEQ:(bb + 1) * SEQ].T.astype(BF16)


def _qkv_layer(x, mods_l, nw, w_qkv, qn, kn, rope_tabs, j):
    tm = 512
    kvd = N_KV_HEADS * HEAD_DIM
    qkv_out = (N_HEADS + 2 * N_KV_HEADS) * HEAD_DIM

    def call(rope, row0, nrows):
        t0 = row0 // tm
        in_specs = [
            pl.BlockSpec((tm, D), lambda i: (t0 + i, 0)),
            pl.BlockSpec((None, N_MOD, D), lambda i: (_mod_group(t0 + i, tm), 0, 0)),
            _const_spec((1, D)),
            _layer_spec((D, qkv_out), j),
            _layer_spec((1, HEAD_DIM), j),
            _layer_spec((1, HEAD_DIM), j),
        ]
        out_shape = [jax.ShapeDtypeStruct((nrows, D), BF16),
                     jax.ShapeDtypeStruct((nrows, kvd), BF16)]
        out_specs = [pl.BlockSpec((tm, D), lambda i: (i, 0)),
                     pl.BlockSpec((tm, kvd), lambda i: (i, 0))]
        args = [x, mods_l, nw, w_qkv, qn, kn]
        if rope:
            tps = DEC_SEQ // tm
            in_specs += [pl.BlockSpec((tm, HEAD_DIM), lambda i: (i % tps, 0)),
                         pl.BlockSpec((tm, HEAD_DIM), lambda i: (i % tps, 0))]
            args += list(rope_tabs)
            out_shape.append(jax.ShapeDtypeStruct((DEC_BATCH, N_KV_HEADS, HEAD_DIM, DEC_SEQ), BF16))
            out_specs.append(pl.BlockSpec((None, N_KV_HEADS, HEAD_DIM, tm),
                                          lambda i: (i // tps, 0, 0, i % tps)))
        else:
            nb = tm // SEQ
            out_shape.append(jax.ShapeDtypeStruct((BATCH, N_KV_HEADS, HEAD_DIM, SEQ), BF16))
            out_specs.append(pl.BlockSpec((nb, N_KV_HEADS, HEAD_DIM, SEQ), lambda i: (i, 0, 0, 0)))
            kv_shape = (BATCH, 1, N_KV_HEADS, SEQ, HEAD_DIM)
            out_shape += [jax.ShapeDtypeStruct(kv_shape, F32)] * 2
            out_specs += [pl.BlockSpec((nb, 1, N_KV_HEADS, SEQ, HEAD_DIM),
                                       lambda i: (i, 0, 0, 0, 0))] * 2
        return pl.pallas_call(
            functools.partial(_qkv_kernel, tm=tm, rope=rope),
            out_shape=out_shape,
            grid=(nrows // tm,),
            in_specs=in_specs,
            out_specs=out_specs,
            scratch_shapes=[pltpu.VMEM((D, qkv_out), BF16)] if rope else [],
            compiler_params=_params(40),
            name="qkv_rope" if rope else "qkv_ctx",
        )(*args)

    return call(False, 0, NP), call(True, NP, NS)


def _attn_kernel(*refs, past, hpb):
    if past:
        (x_ref, bound_ref, q_ref, k_ref, vt_ref, ck_ref, cv_ref, mod_ref, wo_ref,
         o_ref, kall_ref, vext_ref, heads_ref) = refs
    else:
        (x_ref, bound_ref, q_ref, k_ref, vt_ref, mod_ref, wo_ref,
         o_ref, kall_ref, vext_ref, heads_ref) = refs
    nkeys = kall_ref.shape[1]
    new_keys = nkeys - past
    nseq = vt_ref.shape[0]
    tq = q_ref.shape[0] // nseq

    @pl.when(pl.program_id(1) == 0)
    def _():
        for sq in range(nseq):
            for g in range(N_KV_HEADS):
                u = sq * N_KV_HEADS + g
                if past:
                    kall_ref[u, 0:past] = _pair_halves_layout(ck_ref[0, 0, g]).astype(BF16)
                    vext_ref[u, 0:HEAD_DIM, 0:past] = cv_ref[0, 0, g].T.astype(BF16)
                kall_ref[u, past:] = k_ref[sq * new_keys:(sq + 1) * new_keys,
                                           g * HEAD_DIM:(g + 1) * HEAD_DIM]
                vext_ref[u, 0:HEAD_DIM, past:] = vt_ref[sq, g]
                vext_ref[u, HEAD_DIM:] = jnp.ones((VEXT_ROWS - HEAD_DIM, nkeys), BF16)

    blocks = [(sq, list(range(h0, h0 + hpb)))
              for sq in range(nseq) for h0 in range(0, N_HEADS, hpb)]

    def scores(block):
        sq, heads = block
        qb = [q_ref[sq * tq:(sq + 1) * tq, hd * HEAD_DIM:(hd + 1) * HEAD_DIM] for hd in heads]
        qb = qb[0] if hpb == 1 else jnp.concatenate(qb, axis=0)
        return _dot_nt(kall_ref[sq * N_KV_HEADS + heads[0] // Q_PER_KV], qb)

    def attend(shift_of):
        st_next = scores(blocks[0])
        for b, (sq, heads) in enumerate(blocks):
            st = st_next
            if b + 1 < len(blocks):
                st_next = scores(blocks[b + 1])
            pt = jnp.exp2(st - shift_of(st)).astype(BF16)
            ot = _dot(vext_ref[sq * N_KV_HEADS + heads[0] // Q_PER_KV], pt)
            o = (ot[0:HEAD_DIM] / ot[HEAD_DIM:HEAD_DIM + 1]).T
            for r, hd in enumerate(heads):
                heads_ref[sq * tq:(sq + 1) * tq, hd * HEAD_DIM:(hd + 1) * HEAD_DIM] = (
                    o[r * tq:(r + 1) * tq].astype(BF16))
        gate = mod_ref[...][2:3]
        o_ref[...] = x_ref[...] + gate * _wdot(heads_ref[...], wo_ref[...])

    bound = bound_ref[0]
    use_bound = bound <= MAX_SAFE_SCORE_BOUND

    @pl.when(use_bound)
    def _():
        attend(lambda st: bound)

    @pl.when(jnp.logical_not(use_bound))
    def _():
        attend(lambda st: jnp.max(st, axis=0, keepdims=True))


def _attn_layer(x, qkv_p, qkv_s, cache_k, cache_v, mods_l, w_out, qn, kn, j):
    kvd = N_KV_HEADS * HEAD_DIM

    q_len = QK_SCALE_LOG2 * HEAD_DIM ** 0.5 * jnp.max(jnp.abs(qn))
    k_len = HEAD_DIM ** 0.5 * jnp.max(jnp.abs(kn))
    cache_len = jnp.sqrt(jnp.max(jnp.sum(jnp.square(cache_k[:, j]), axis=-1)))
    slack = 1.0 + 2.0 ** -6
    bound_p = (q_len * k_len * slack).reshape(1)
    bound_s = (q_len * jnp.maximum(k_len, cache_len) * slack).reshape(1)

    def call(x, bound, q, k, vt, past, row0, nb, seq, tq, hpb, nseq):
        rows = tq * nseq
        t0 = row0 // rows
        nq = seq // tq
        assert nseq == 1 or nq == 1
        in_specs = [
            pl.BlockSpec((rows, D), lambda b, i: (t0 + b * nq + i, 0)),
            pl.BlockSpec(memory_space=pltpu.SMEM),
            pl.BlockSpec((rows, D), lambda b, i: (b * nq + i, 0)),
            pl.BlockSpec((nseq * seq, kvd), lambda b, i: (b, 0)),
            pl.BlockSpec((nseq, N_KV_HEADS, HEAD_DIM, seq), lambda b, i: (b, 0, 0, 0)),
        ]
        args = [x, bound, q, k, vt]
        if past:
            cspec = pl.BlockSpec((1, 1, N_KV_HEADS, PAST_LEN, HEAD_DIM),
                                 lambda b, i: (b, j, 0, 0, 0))
            in_specs += [cspec, cspec]
            args += [cache_k, cache_v]
        in_specs += [
            pl.BlockSpec((None, N_MOD, D),
                         lambda b, i: (_mod_group(t0 + b * nq + i, rows), 0, 0)),
            _layer_spec((D, D), j),
        ]
        args += [mods_l, w_out]
        return pl.pallas_call(
            functools.partial(_attn_kernel, past=past, hpb=hpb),
            out_shape=jax.ShapeDtypeStruct((NT, D), F32),
            grid=(nb // nseq, nq),
            in_specs=in_specs,
            out_specs=pl.BlockSpec((rows, D), lambda b, i: (t0 + b * nq + i, 0)),
            scratch_shapes=[pltpu.VMEM((nseq * N_KV_HEADS, past + seq, HEAD_DIM), BF16),
                            pltpu.VMEM((nseq * N_KV_HEADS, VEXT_ROWS, past + seq), BF16),
                            pltpu.VMEM((rows, D), BF16)],
            input_output_aliases={0: 0},
            compiler_params=_params(48, 2),
            name="attn_cached" if past else "attn_ctx",
        )(*args)

    x = call(x, bound_p, *qkv_p, 0, 0, BATCH, SEQ, SEQ, Q_PER_KV, 2)
    return call(x, bound_s, *qkv_s, PAST_LEN, NP, DEC_BATCH, DEC_SEQ, 512, 1, 1)


def _ffn_kernel(*refs, final, tm, ada):
    side_work = ()
    if final:
        x_ref, mod_ref, nw_ref, win_ref, wout_ref, fw_ref, op_ref, os_ref, h_ref, acc_ref = refs
    elif ada:
        (x_ref, mod_ref, nw_ref, win_ref, wout_ref, c_ref, adaw_ref, adab_ref,
         o_ref, modo_ref, h_ref, acc_ref) = refs

        def later_layers_adaln():
            c = c_ref[...]
            sc = (c * jax.nn.sigmoid(c)).astype(BF16)
            modo_ref[...] = _wdot(sc, adaw_ref[...]) + adab_ref[...]

        side_work = [lambda: None] * (N_FF_CHUNKS // 2) + [later_layers_adaln]
    else:
        x_ref, mod_ref, nw_ref, win_ref, wout_ref, o_ref, h_ref, acc_ref = refs
    mod = mod_ref[...]
    x = x_ref[...]
    h_ref[...] = _norm_mod(x, nw_ref[...], mod[3:4], mod[4:5]).astype(BF16)
    _swiglu(h_ref[...], win_ref, wout_ref, acc_ref, side_work)
    y = x + mod[5:6] * acc_ref[...]
    if not final:
        o_ref[...] = y
        return
    y = y * lax.rsqrt(jnp.mean(y * y, axis=-1, keepdims=True) + EPS) * fw_ref[...]
    is_prompt = pl.program_id(0) < NP // tm

    @pl.when(is_prompt)
    def _():
        op_ref[...] = y

    @pl.when(jnp.logical_not(is_prompt))
    def _():
        os_ref[...] = y


def _ffn_layer(x, mods_l, nw, w_in, w_out, layer, final_w=None, adaln_rest=None):
    tm = 512
    final = final_w is not None
    ada = adaln_rest is not None
    n_steps = NT // tm
    in_specs = [
        pl.BlockSpec((tm, D), lambda i: (i, 0)),
        pl.BlockSpec((None, N_MOD, D), lambda i: (_mod_group(i, tm), 0, 0)),
        _const_spec((1, D)),
        _layer_spec((D, 2 * D_FF), layer),
        _layer_spec((D_FF, D), layer),
    ]
    args = [x, mods_l, nw, w_in, w_out]
    if final:
        npt = NP // tm
        in_specs.append(_const_spec((1, D)))
        args.append(final_w)
        out_shape = [jax.ShapeDtypeStruct((NP, D), F32), jax.ShapeDtypeStruct((NS, D), F32)]
        out_specs = [pl.BlockSpec((tm, D), lambda i: (jnp.minimum(i, npt - 1), 0)),
                     pl.BlockSpec((tm, D), lambda i: (jnp.maximum(i - npt, 0), 0))]
    else:
        out_shape = jax.ShapeDtypeStruct((NT, D), F32)
        out_specs = pl.BlockSpec((tm, D), lambda i: (i, 0))
    if ada:
        rest = DEPTH - 1
        per_layer = n_steps // rest
        tn = N_MOD * D // per_layer
        assert per_layer * rest == n_steps and tn * per_layer == N_MOD * D and tn % LANES == 0
        in_specs += [
            _const_spec((8, D)),
            pl.BlockSpec((None, D, tn), lambda i: (1 + i // per_layer, 0, i % per_layer)),
            pl.BlockSpec((None, 1, tn), lambda i: (1 + i // per_layer, 0, i % per_layer)),
        ]
        args += list(adaln_rest)
        out_shape = [out_shape, jax.ShapeDtypeStruct((rest, 8, N_MOD * D), F32)]
        out_specs = [out_specs,
                     pl.BlockSpec((None, 8, tn), lambda i: (i // per_layer, 0, i % per_layer))]
    return pl.pallas_call(
        functools.partial(_ffn_kernel, final=final, tm=tm, ada=ada),
        out_shape=out_shape,
        grid=(n_steps,),
        in_specs=in_specs,
        out_specs=out_specs,
        scratch_shapes=[pltpu.VMEM((tm, D), BF16), pltpu.VMEM((tm, D), F32)],
        compiler_params=_params(56),
        name="ffn_final" if final else "ffn",
    )(*args)


def _rope_tables():
    t = np.arange(DEC_SEQ)
    inv_freq = ROPE_THETA ** (-np.arange(0, ROPE_AXIS_DIM, 2, dtype=np.float64) / ROPE_AXIS_DIM)
    ang_r = (t // GRID_W)[:, None] * inv_freq[None, :]
    ang_c = (t % GRID_W)[:, None] * inv_freq[None, :]
    cos = np.concatenate([np.cos(ang_r), np.cos(ang_c)] * 2, axis=-1)
    sin = np.concatenate([-np.sin(ang_r), -np.sin(ang_c), np.sin(ang_r), np.sin(ang_c)], axis=-1)
    return jnp.asarray(cos, F32), jnp.asarray(sin, F32)


def kernel(x_prompt, x_sample, cache_k, cache_v, c, c_ctx, norm1_w, norm2_w, ada_w, ada_b,
           conv_in_w, conv_w, conv_out_w, pool_w, pool_scale, attn_qkv_w, q_norm_w, k_norm_w,
           attn_out_w, ffn_in_w, ffn_out_w, final_norm_w):
    cvec = jnp.concatenate([c_ctx[None, :], c, jnp.zeros((8 - N_GROUPS, D), F32)], axis=0)
    ada_b3 = ada_b.reshape(DEPTH, 1, N_MOD * D)

    def mod_rows(m):
        return m[:, :N_GROUPS].reshape(m.shape[0], N_GROUPS, N_MOD, D)

    mods = [mod_rows(_adaln(cvec, ada_w, ada_b3, 1))[0]]

    pool_scale3 = pool_scale[:, None, :]
    qn3, kn3 = q_norm_w[:, None, :], k_norm_w[:, None, :]

    x = (x_prompt.reshape(NP, D), x_sample.reshape(NS, D))
    new_k = new_v = None
    for i in range(DEPTH):
        kind, j = i % 3, i // 3
        nw1 = norm1_w[i][None, :]
        if kind == 0:
            x = _conv_layer(x, mods[i], nw1, conv_in_w, conv_w, conv_out_w, j)
        elif kind == 1:
            x = _pool_ffn_layer(x, mods[i], nw1, pool_w, pool_scale3, j,
                                norm2_w[i][None, :], ffn_in_w, ffn_out_w, i)
            continue
        else:
            (qp, kp, vtp, new_k, new_v), qkv_s = _qkv_layer(
                x, mods[i], nw1, attn_qkv_w, qn3, kn3, _rope_tables(), j)
            x = _attn_layer(x, (qp, kp, vtp), qkv_s, cache_k, cache_v, mods[i], attn_out_w,
                            q_norm_w[j], k_norm_w[j], j)
        if i == 0:
            x, rest = _ffn_layer(x, mods[0], norm2_w[0][None, :], ffn_in_w, ffn_out_w, 0,
                                 adaln_rest=(cvec, ada_w, ada_b3))
            mods += list(mod_rows(rest))
            continue
        x = _ffn_layer(x, mods[i], norm2_w[i][None, :], ffn_in_w, ffn_out_w, i,
                       final_norm_w[None, :] if i == DEPTH - 1 else None)

    y_prompt, y_sample = x
    return (y_prompt.reshape(BATCH, SEQ, D), y_sample.reshape(DEC_BATCH, DEC_SEQ, D), new_k, new_v)
```

```python
import functools

import numpy as np
import jax
import jax.numpy as jnp
from jax import lax
from jax.experimental import pallas as pl
from jax.experimental.pallas import tpu as pltpu

D = 1024
BATCH = 16
SEQ = 256
DEPTH = 4
DEC_BATCH = 2
DEC_SEQ = 4096
PAST_LEN = 256
GRID_W = 64
N_HEADS = 8
N_KV_HEADS = 2
HEAD_DIM = 128
Q_PER_KV = N_HEADS // N_KV_HEADS
ROPE_AXIS_DIM = HEAD_DIM // 2
ROPE_THETA = 10000.0
POOL_WINDOWS = (2, 4, 8, 16)
POOL_GROUP_DIM = D // 4
D_FF = 2816
N_MOD = 6
EPS = 1e-6
ATTN_SCALE = HEAD_DIM ** -0.5

NP = BATCH * SEQ
NS = DEC_BATCH * DEC_SEQ
NT = NP + NS
N_GROUPS = 1 + DEC_BATCH

QK_SCALE_LOG2 = ATTN_SCALE * float(np.log2(np.e))
MAX_SAFE_SCORE_BOUND = 60.0
BF16_SUBLANES = 16
VEXT_ROWS = HEAD_DIM + BF16_SUBLANES
HALO = BF16_SUBLANES
FF_CHUNK = 256
N_FF_CHUNKS = D_FF // FF_CHUNK
CONV_CHUNK = 256
LANES = 128
BF16 = jnp.bfloat16
F32 = jnp.float32


def _dot(a, b):
    return jnp.dot(a, b, preferred_element_type=F32)


def _wdot(a, w):
    return _dot(a, w.astype(BF16))


def _dot_nt(a, b):
    return lax.dot_general(a, b, (((1,), (1,)), ((), ())), preferred_element_type=F32)


def _norm_mod(x, nw, shift, scale):
    y = x * lax.rsqrt(jnp.mean(x * x, axis=-1, keepdims=True) + EPS)
    return (y * nw) * (1.0 + scale) + shift


def _mod_group(tile, tm):
    return jnp.where(tile < NP // tm, 0, 1 + (tile - NP // tm) // (DEC_SEQ // tm))


def _params(vmem_mb, n_axes=1):
    return pltpu.CompilerParams(
        dimension_semantics=("arbitrary",) * n_axes,
        vmem_limit_bytes=vmem_mb * 1024 * 1024)


def _const_spec(shape):
    return pl.BlockSpec(shape, lambda *_: (0,) * len(shape), pipeline_mode=pl.Buffered(1))


def _layer_spec(shape, j):
    return pl.BlockSpec((None,) + tuple(shape), lambda *_: (j,) + (0,) * len(shape),
                        pipeline_mode=pl.Buffered(1))


def _adaln_kernel(c_ref, w_ref, b_ref, o_ref):
    c = c_ref[...]
    sc = (c * jax.nn.sigmoid(c)).astype(BF16)
    o_ref[...] = _dot(sc, w_ref[...].astype(BF16)) + b_ref[...]


def _adaln(cvec, ada_w, ada_b3, n_layers):
    tn = 1536
    return pl.pallas_call(
        _adaln_kernel,
        out_shape=jax.ShapeDtypeStruct((n_layers, 8, N_MOD * D), F32),
        grid=(n_layers, N_MOD * D // tn),
        in_specs=[
            pl.BlockSpec((8, D), lambda l, j: (0, 0)),
            pl.BlockSpec((None, D, tn), lambda l, j: (l, 0, j)),
            pl.BlockSpec((None, 1, tn), lambda l, j: (l, 0, j)),
        ],
        out_specs=pl.BlockSpec((None, 8, tn), lambda l, j: (l, 0, j)),
        compiler_params=_params(40, 2),
        name="adaln",
    )(cvec, ada_w, ada_b3)


def _conv_kernel(*refs, tm, split):
    i = pl.program_id(0)
    if split:
        (pm_ref, pp_ref, pn_ref, sm_ref, sp_ref, sn_ref,
         mod_ref, nw_ref, win_ref, cw_ref, wout_ref, o_ref, h_ref, acc_ref) = refs
        is_prompt = i < NP // tm
        x = jnp.where(is_prompt, pm_ref[...], sm_ref[...])
        x_prev = jnp.where(is_prompt, pp_ref[...], sp_ref[...])
        x_next = jnp.where(is_prompt, pn_ref[...], sn_ref[...])
    else:
        (x_ref, xp_ref, xn_ref,
         mod_ref, nw_ref, win_ref, cw_ref, wout_ref, o_ref, h_ref, acc_ref) = refs
        x, x_prev, x_next = x_ref[...], xp_ref[...], xn_ref[...]
    mod = mod_ref[...]
    nw = nw_ref[...]
    shift, scale, gate = mod[0:1], mod[1:2], mod[2:3]
    half = HALO // 2
    h_ref[0:tm] = _norm_mod(x, nw, shift, scale).astype(BF16)
    h_ref[tm:] = jnp.concatenate([_norm_mod(x_next, nw, shift, scale)[0:half],
                                  _norm_mod(x_prev, nw, shift, scale)[half:]], axis=0).astype(BF16)

    seq_len = jnp.where(i < NP // tm, SEQ, DEC_SEQ)
    pos = (i * tm + lax.broadcasted_iota(jnp.int32, (tm, 1), 0)) & (seq_len - 1)
    has_prev = pos != 0
    has_next = pos != seq_len - 1
    cw = cw_ref[...]
    rows = tm + HALO

    def in_proj(j):
        c0 = j * CONV_CHUNK
        return (_wdot(h_ref[0:tm], win_ref[:, c0:c0 + CONV_CHUNK]),
                _wdot(h_ref[...], win_ref[:, D + c0:D + c0 + CONV_CHUNK]),
                _wdot(h_ref[...], win_ref[:, 2 * D + c0:2 * D + c0 + CONV_CHUNK]))

    nxt = in_proj(0)
    for j in range(D // CONV_CHUNK):
        c0 = j * CONV_CHUNK
        b, cg, u = nxt
        if j + 1 < D // CONV_CHUNK:
            nxt = in_proj(j + 1)
        z = cg * u
        z_prev = pltpu.roll(z, 1, axis=0)[0:tm]
        z_next = pltpu.roll(z, rows - 1, axis=0)[0:tm]
        conv = (jnp.where(has_prev, z_prev, 0.0) * cw[0:1, c0:c0 + CONV_CHUNK]
                + z[0:tm] * cw[1:2, c0:c0 + CONV_CHUNK]
                + jnp.where(has_next, z_next, 0.0) * cw[2:3, c0:c0 + CONV_CHUNK])
        part = _wdot((b * conv).astype(BF16), wout_ref[c0:c0 + CONV_CHUNK, :])
        if j == 0:
            acc_ref[...] = part
        else:
            acc_ref[...] += part
    o_ref[...] = x + gate * acc_ref[...]


def _halo_specs(tm, nrows, tile_of):
    nh = tm // HALO
    last = nrows // HALO - 1
    return [
        pl.BlockSpec((tm, D), lambda i: (tile_of(i), 0)),
        pl.BlockSpec((HALO, D), lambda i: (jnp.maximum(tile_of(i) * nh - 1, 0), 0)),
        pl.BlockSpec((HALO, D), lambda i: (jnp.minimum((tile_of(i) + 1) * nh, last), 0)),
    ]


def _conv_layer(xs, mods_l, nw, w_in, cw, w_out, j):
    tm = 512
    split = isinstance(xs, tuple)
    if split:
        npt = NP // tm
        x_specs = (_halo_specs(tm, NP, lambda i: jnp.minimum(i, npt - 1))
                   + _halo_specs(tm, NS, lambda i: jnp.maximum(i - npt, 0)))
        x_args = [xs[0]] * 3 + [xs[1]] * 3
    else:
        x_specs = _halo_specs(tm, NT, lambda i: i)
        x_args = [xs] * 3
    return pl.pallas_call(
        functools.partial(_conv_kernel, tm=tm, split=split),
        out_shape=jax.ShapeDtypeStruct((NT, D), F32),
        grid=(NT // tm,),
        in_specs=x_specs + [
            pl.BlockSpec((None, N_MOD, D), lambda i: (_mod_group(i, tm), 0, 0)),
            _const_spec((1, D)),
            _layer_spec((D, 3 * D), j),
            _layer_spec((3, D), j),
            _layer_spec((D, D), j),
        ],
        out_specs=pl.BlockSpec((tm, D), lambda i: (i, 0)),
        scratch_shapes=[pltpu.VMEM((tm + HALO, D), BF16), pltpu.VMEM((tm, D), F32)],
        compiler_params=_params(48),
        name="conv_mixer",
    )(*x_args, mods_l, nw, w_in, cw, w_out)


def _swiglu(h, win_ref, wout_ref, acc_ref, side_work=()):
    def in_proj(j):
        c0 = j * FF_CHUNK
        return (_wdot(h, win_ref[:, c0:c0 + FF_CHUNK]),
                _wdot(h, win_ref[:, D_FF + c0:D_FF + c0 + FF_CHUNK]))

    n_chunks = N_FF_CHUNKS
    assert len(side_work) <= n_chunks
    nxt = in_proj(0)
    for j in range(n_chunks):
        c0 = j * FF_CHUNK
        g, u = nxt
        if j + 1 < n_chunks:
            nxt = in_proj(j + 1)
        if j < len(side_work):
            side_work[j]()
        a = (g * jax.nn.sigmoid(g) * u).astype(BF16)
        part = _wdot(a, wout_ref[c0:c0 + FF_CHUNK, :])
        if j == 0:
            acc_ref[...] = part
        else:
            acc_ref[...] += part


def _pool_ffn_kernel(x_ref, xp_ref, xn_ref, modn_ref, modc_ref, nw1_ref, wp_ref, ps_ref, nw2_ref,
                     win_ref, wout_ref, o_ref, hp_ref, st_ref, x1_ref, h2_ref, acc_ref,
                     *, tm, n_tiles):
    s = pl.program_id(0)
    cur = (s + 1) % 2
    new = s % 2

    @pl.when(s == 0)
    def _():
        x1_ref[1] = jnp.zeros((tm, D), F32)
        h2_ref[1] = jnp.zeros((tm, D), BF16)

    t = jnp.minimum(s, n_tiles - 1)
    mod = modn_ref[...]
    nw = nw1_ref[...]
    shift, scale, gate = mod[0:1], mod[1:2], mod[2:3]
    seq_len = jnp.where(t < NP // tm, SEQ, DEC_SEQ)
    pos0 = (t * tm) & (seq_len - 1)
    rows = tm + 2 * HALO

    def row_stats():
        def inv_rms(xv):
            r = lax.rsqrt(jnp.mean(xv * xv, axis=-1, keepdims=True) + EPS)
            return jnp.broadcast_to(r, (xv.shape[0], LANES))
        st_ref[0:HALO] = inv_rms(xp_ref[...])
        st_ref[HALO:HALO + tm] = inv_rms(x_ref[...])
        st_ref[HALO + tm:] = inv_rms(xn_ref[...])

    def fill_group(g):
        at_start = pos0 == 0
        at_end = pos0 + tm == seq_len
        for half in range(POOL_GROUP_DIM // LANES):
            cs = slice(g * POOL_GROUP_DIM + half * LANES, g * POOL_GROUP_DIM + (half + 1) * LANES)

            def nm(xv, iv):
                return ((xv * iv) * nw[:, cs]) * (1.0 + scale[:, cs]) + shift[:, cs]

            hp_ref[0:HALO, cs] = jnp.where(at_start, 0.0, nm(xp_ref[:, cs], st_ref[0:HALO]))
            hp_ref[HALO:HALO + tm, cs] = nm(x_ref[:, cs], st_ref[HALO:HALO + tm])
            hp_ref[HALO + tm:, cs] = jnp.where(at_end, 0.0, nm(xn_ref[:, cs], st_ref[HALO + tm:]))

    def pool_group(g):
        w = POOL_WINDOWS[g]
        c0 = g * POOL_GROUP_DIM
        pos = pos0 + lax.broadcasted_iota(jnp.int32, (tm, 1), 0)
        hg = hp_ref[:, c0:c0 + POOL_GROUP_DIM]
        p = hg + pltpu.roll(hg, 1, axis=0)
        step = 1
        while 2 * step < w:
            p = pltpu.roll(p, step, axis=0) + pltpu.roll(p, rows - step, axis=0)
            step *= 2
        cnt = (jnp.minimum(pos + w // 2, seq_len) - jnp.maximum(pos - w // 2, 0)).astype(F32)
        pooled = p[HALO:HALO + tm] / cnt
        diff = (pooled - hg[HALO:HALO + tm]).astype(BF16)
        mixed = _wdot(diff, wp_ref[g]) * ps_ref[...][:, c0:c0 + POOL_GROUP_DIM]
        x1_ref[new, :, c0:c0 + POOL_GROUP_DIM] = (
            x_ref[:, c0:c0 + POOL_GROUP_DIM] + gate[:, c0:c0 + POOL_GROUP_DIM] * mixed)

    def norm_for_swiglu():
        h2_ref[new] = _norm_mod(x1_ref[new], nw2_ref[...], mod[3:4], mod[4:5]).astype(BF16)

    side_work = [row_stats]
    for g in range(len(POOL_WINDOWS)):
        side_work += [functools.partial(fill_group, g), functools.partial(pool_group, g)]
    side_work.append(norm_for_swiglu)

    modc = modc_ref[...]
    _swiglu(h2_ref[cur], win_ref, wout_ref, acc_ref, side_work)
    o_ref[...] = x1_ref[cur] + modc[5:6] * acc_ref[...]


def _pool_ffn_layer(x, mods_l, nw1, wp, ps, j, nw2, w_in, w_out, layer):
    tm = 256
    n = NT // tm
    return pl.pallas_call(
        functools.partial(_pool_ffn_kernel, tm=tm, n_tiles=n),
        out_shape=jax.ShapeDtypeStruct((NT, D), F32),
        grid=(n + 1,),
        in_specs=_halo_specs(tm, NT, lambda s: jnp.minimum(s, n - 1)) + [
            pl.BlockSpec((None, N_MOD, D),
                         lambda s: (_mod_group(jnp.minimum(s, n - 1), tm), 0, 0)),
            pl.BlockSpec((None, N_MOD, D),
                         lambda s: (_mod_group(jnp.maximum(s - 1, 0), tm), 0, 0)),
            _const_spec((1, D)),
            _layer_spec((4, POOL_GROUP_DIM, POOL_GROUP_DIM), j),
            _layer_spec((1, D), j),
            _const_spec((1, D)),
            _layer_spec((D, 2 * D_FF), layer),
            _layer_spec((D_FF, D), layer),
        ],
        out_specs=pl.BlockSpec((tm, D), lambda s: (jnp.maximum(s - 1, 0), 0)),
        scratch_shapes=[pltpu.VMEM((tm + 2 * HALO, D), F32),
                        pltpu.VMEM((tm + 2 * HALO, LANES), F32),
                        pltpu.VMEM((2, tm, D), F32),
                        pltpu.VMEM((2, tm, D), BF16),
                        pltpu.VMEM((tm, D), F32)],
        compiler_params=_params(56),
        name="pool_ffn",
    )(x, x, x, mods_l, mods_l, nw1, wp, ps, nw2, w_in, w_out)


def _head_norm(t, w):
    return t * lax.rsqrt(jnp.mean(t * t, axis=-1, keepdims=True) + EPS) * w


def _pair_halves_layout(a):
    q4 = HEAD_DIM // 4
    lane = lax.broadcasted_iota(jnp.int32, (1, HEAD_DIM), 1)
    from_next = pltpu.roll(a, HEAD_DIM - q4, axis=1)
    from_prev = pltpu.roll(a, q4, axis=1)
    return jnp.where((lane >= q4) & (lane < 2 * q4), from_next,
                     jnp.where((lane >= 2 * q4) & (lane < 3 * q4), from_prev, a))


def _qkv_kernel(*refs, tm, rope):
    if rope:
        (x_ref, mod_ref, nw_ref, w_ref, qn_ref, kn_ref, cos_ref, sin_ref,
         q_ref, k_ref, vt_ref, wb_ref) = refs
    else:
        (x_ref, mod_ref, nw_ref, w_ref, qn_ref, kn_ref,
         q_ref, k_ref, vt_ref, nk_ref, nv_ref) = refs
    n_qk = N_HEADS + N_KV_HEADS
    if rope:
        @pl.when(pl.program_id(0) == 0)
        def _():
            for hd in range(n_qk + N_KV_HEADS):
                cols = slice(hd * HEAD_DIM, (hd + 1) * HEAD_DIM)
                w = w_ref[:, cols]
                wb_ref[:, cols] = (_pair_halves_layout(w) if hd < n_qk else w).astype(BF16)

    mod = mod_ref[...]
    h = _norm_mod(x_ref[...], nw_ref[...], mod[0:1], mod[1:2]).astype(BF16)
    qn = qn_ref[...]
    kn = kn_ref[...]
    if rope:
        qn = _pair_halves_layout(qn)
        kn = _pair_halves_layout(kn)
        cos = cos_ref[...]
        sin = sin_ref[...]

    pair = 2 * HEAD_DIM
    n_pairs = (N_HEADS + 2 * N_KV_HEADS) // 2
    same_head = (lax.broadcasted_iota(jnp.int32, (pair, pair), 0) // HEAD_DIM
                 == lax.broadcasted_iota(jnp.int32, (pair, pair), 1) // HEAD_DIM)
    same_head = jnp.where(same_head, 1.0, 0.0).astype(BF16)

    def head_inv_rms(tt):
        sq = tt * tt
        hi = sq.astype(BF16)
        lo = (sq - hi.astype(F32)).astype(BF16)
        ssq = _dot(hi, same_head) + _dot(lo, same_head)
        return lax.rsqrt(ssq * (1.0 / HEAD_DIM) + EPS)

    def normed(t, inv_rms, w):
        if not rope:
            return _head_norm(t, w)
        t = t * inv_rms * w
        return t * cos + pltpu.roll(t, HEAD_DIM // 2, axis=1) * sin

    def proj(c):
        if rope:
            return _dot(h, wb_ref[:, c * pair:(c + 1) * pair])
        return _wdot(h, w_ref[:, c * pair:(c + 1) * pair])

    nxt = proj(0)
    for c in range(n_pairs):
        cur = nxt
        if c + 1 < n_pairs:
            nxt = proj(c + 1)
        inv = head_inv_rms(cur) if rope and 2 * c < n_qk else None
        for r in range(2):
            t = cur[:, r * HEAD_DIM:(r + 1) * HEAD_DIM]
            inv_r = None if inv is None else inv[:, r * HEAD_DIM:(r + 1) * HEAD_DIM]
            hd = 2 * c + r
            if hd < N_HEADS:
                q_ref[:, hd * HEAD_DIM:(hd + 1) * HEAD_DIM] = (
                    normed(t, inv_r, qn) * QK_SCALE_LOG2).astype(BF16)
            elif hd < N_HEADS + N_KV_HEADS:
                g = hd - N_HEADS
                t = normed(t, inv_r, kn)
                if not rope:
                    for bb in range(tm // SEQ):
                        nk_ref[bb, 0, g] = t[bb * SEQ:(bb + 1) * SEQ]
                k_ref[:, g * HEAD_DIM:(g + 1) * HEAD_DIM] = t.astype(BF16)
            else:
                g = hd - N_HEADS - N_KV_HEADS
                if rope:
                    vt_ref[g] = t.T.astype(BF16)
                else:
                    for bb in range(tm // SEQ):
                        nv_ref[bb, 0, g] = t[bb * SEQ:(bb + 1) * SEQ]
                        vt_ref[bb, g] = t[bb * SEQ:(bb + 1) * SEQ].T.astype(BF16)


def _qkv_layer(x, mods_l, nw, w_qkv, qn, kn, rope_tabs, j):
    tm = 512
    kvd = N_KV_HEADS * HEAD_DIM
    qkv_out = (N_HEADS + 2 * N_KV_HEADS) * HEAD_DIM

    def call(rope, row0, nrows):
        t0 = row0 // tm
        in_specs = [
            pl.BlockSpec((tm, D), lambda i: (t0 + i, 0)),
            pl.BlockSpec((None, N_MOD, D), lambda i: (_mod_group(t0 + i, tm), 0, 0)),
            _const_spec((1, D)),
            _layer_spec((D, qkv_out), j),
            _layer_spec((1, HEAD_DIM), j),
            _layer_spec((1, HEAD_DIM), j),
        ]
        out_shape = [jax.ShapeDtypeStruct((nrows, D), BF16),
                     jax.ShapeDtypeStruct((nrows, kvd), BF16)]
        out_specs = [pl.BlockSpec((tm, D), lambda i: (i, 0)),
                     pl.BlockSpec((tm, kvd), lambda i: (i, 0))]
        args = [x, mods_l, nw, w_qkv, qn, kn]
        if rope:
            tps = DEC_SEQ // tm
            in_specs += [pl.BlockSpec((tm, HEAD_DIM), lambda i: (i % tps, 0)),
                         pl.BlockSpec((tm, HEAD_DIM), lambda i: (i % tps, 0))]
            args += list(rope_tabs)
            out_shape.append(jax.ShapeDtypeStruct((DEC_BATCH, N_KV_HEADS, HEAD_DIM, DEC_SEQ), BF16))
            out_specs.append(pl.BlockSpec((None, N_KV_HEADS, HEAD_DIM, tm),
                                          lambda i: (i // tps, 0, 0, i % tps)))
        else:
            nb = tm // SEQ
            out_shape.append(jax.ShapeDtypeStruct((BATCH, N_KV_HEADS, HEAD_DIM, SEQ), BF16))
            out_specs.append(pl.BlockSpec((nb, N_KV_HEADS, HEAD_DIM, SEQ), lambda i: (i, 0, 0, 0)))
            kv_shape = (BATCH, 1, N_KV_HEADS, SEQ, HEAD_DIM)
            out_shape += [jax.ShapeDtypeStruct(kv_shape, F32)] * 2
            out_specs += [pl.BlockSpec((nb, 1, N_KV_HEADS, SEQ, HEAD_DIM),
                                       lambda i: (i, 0, 0, 0, 0))] * 2
        return pl.pallas_call(
            functools.partial(_qkv_kernel, tm=tm, rope=rope),
            out_shape=out_shape,
            grid=(nrows // tm,),
            in_specs=in_specs,
            out_specs=out_specs,
            scratch_shapes=[pltpu.VMEM((D, qkv_out), BF16)] if rope else [],
            compiler_params=_params(40),
            name="qkv_rope" if rope else "qkv_ctx",
        )(*args)

    return call(False, 0, NP), call(True, NP, NS)


def _attn_kernel(*refs, past, hpb):
    if past:
        (x_ref, bound_ref, q_ref, k_ref, vt_ref, ck_ref, cv_ref, mod_ref, wo_ref,
         o_ref, kall_ref, vext_ref, heads_ref) = refs
    else:
        (x_ref, bound_ref, q_ref, k_ref, vt_ref, mod_ref, wo_ref,
         o_ref, kall_ref, vext_ref, heads_ref) = refs
    nkeys = kall_ref.shape[1]
    new_keys = nkeys - past
    nseq = vt_ref.shape[0]
    tq = q_ref.shape[0] // nseq

    @pl.when(pl.program_id(1) == 0)
    def _():
        for sq in range(nseq):
            for g in range(N_KV_HEADS):
                u = sq * N_KV_HEADS + g
                if past:
                    kall_ref[u, 0:past] = _pair_halves_layout(ck_ref[0, 0, g]).astype(BF16)
                    vext_ref[u, 0:HEAD_DIM, 0:past] = cv_ref[0, 0, g].T.astype(BF16)
                kall_ref[u, past:] = k_ref[sq * new_keys:(sq + 1) * new_keys,
                                           g * HEAD_DIM:(g + 1) * HEAD_DIM]
                vext_ref[u, 0:HEAD_DIM, past:] = vt_ref[sq, g]
                vext_ref[u, HEAD_DIM:] = jnp.ones((VEXT_ROWS - HEAD_DIM, nkeys), BF16)

    blocks = [(sq, list(range(h0, h0 + hpb)))
              for sq in range(nseq) for h0 in range(0, N_HEADS, hpb)]

    def scores(block):
        sq, heads = block
        qb = [q_ref[sq * tq:(sq + 1) * tq, hd * HEAD_DIM:(hd + 1) * HEAD_DIM] for hd in heads]
        qb = qb[0] if hpb == 1 else jnp.concatenate(qb, axis=0)
        return _dot_nt(kall_ref[sq * N_KV_HEADS + heads[0] // Q_PER_KV], qb)

    def attend(shift_of):
        st_next = scores(blocks[0])
        for b, (sq, heads) in enumerate(blocks):
            st = st_next
            if b + 1 < len(blocks):
                st_next = scores(blocks[b + 1])
            pt = jnp.exp2(st - shift_of(st)).astype(BF16)
            ot = _dot(vext_ref[sq * N_KV_HEADS + heads[0] // Q_PER_KV], pt)
            o = (ot[0:HEAD_DIM] / ot[HEAD_DIM:HEAD_DIM + 1]).T
            for r, hd in enumerate(heads):
                heads_ref[sq * tq:(sq + 1) * tq, hd * HEAD_DIM:(hd + 1) * HEAD_DIM] = (
                    o[r * tq:(r + 1) * tq].astype(BF16))
        gate = mod_ref[...][2:3]
        o_ref[...] = x_ref[...] + gate * _wdot(heads_ref[...], wo_ref[...])

    bound = bound_ref[0]
    use_bound = bound <= MAX_SAFE_SCORE_BOUND

    @pl.when(use_bound)
    def _():
        attend(lambda st: bound)

    @pl.when(jnp.logical_not(use_bound))
    def _():
        attend(lambda st: jnp.max(st, axis=0, keepdims=True))


def _attn_layer(x, qkv_p, qkv_s, cache_k, cache_v, mods_l, w_out, qn, kn, j):
    kvd = N_KV_HEADS * HEAD_DIM

    q_len = QK_SCALE_LOG2 * HEAD_DIM ** 0.5 * jnp.max(jnp.abs(qn))
    k_len = HEAD_DIM ** 0.5 * jnp.max(jnp.abs(kn))
    cache_len = jnp.sqrt(jnp.max(jnp.sum(jnp.square(cache_k[:, j]), axis=-1)))
    slack = 1.0 + 2.0 ** -6
    bound_p = (q_len * k_len * slack).reshape(1)
    bound_s = (q_len * jnp.maximum(k_len, cache_len) * slack).reshape(1)

    def call(x, bound, q, k, vt, past, row0, nb, seq, tq, hpb, nseq):
        rows = tq * nseq
        t0 = row0 // rows
        nq = seq // tq
        assert nseq == 1 or nq == 1
        in_specs = [
            pl.BlockSpec((rows, D), lambda b, i: (t0 + b * nq + i, 0)),
            pl.BlockSpec(memory_space=pltpu.SMEM),
            pl.BlockSpec((rows, D), lambda b, i: (b * nq + i, 0)),
            pl.BlockSpec((nseq * seq, kvd), lambda b, i: (b, 0)),
            pl.BlockSpec((nseq, N_KV_HEADS, HEAD_DIM, seq), lambda b, i: (b, 0, 0, 0)),
        ]
        args = [x, bound, q, k, vt]
        if past:
            cspec = pl.BlockSpec((1, 1, N_KV_HEADS, PAST_LEN, HEAD_DIM),
                                 lambda b, i: (b, j, 0, 0, 0))
            in_specs += [cspec, cspec]
            args += [cache_k, cache_v]
        in_specs += [
            pl.BlockSpec((None, N_MOD, D),
                         lambda b, i: (_mod_group(t0 + b * nq + i, rows), 0, 0)),
            _layer_spec((D, D), j),
        ]
        args += [mods_l, w_out]
        return pl.pallas_call(
            functools.partial(_attn_kernel, past=past, hpb=hpb),
            out_shape=jax.ShapeDtypeStruct((NT, D), F32),
            grid=(nb // nseq, nq),
            in_specs=in_specs,
            out_specs=pl.BlockSpec((rows, D), lambda b, i: (t0 + b * nq + i, 0)),
            scratch_shapes=[pltpu.VMEM((nseq * N_KV_HEADS, past + seq, HEAD_DIM), BF16),
                            pltpu.VMEM((nseq * N_KV_HEADS, VEXT_ROWS, past + seq), BF16),
                            pltpu.VMEM((rows, D), BF16)],
            input_output_aliases={0: 0},
            compiler_params=_params(48, 2),
            name="attn_cached" if past else "attn_ctx",
        )(*args)

    x = call(x, bound_p, *qkv_p, 0, 0, BATCH, SEQ, SEQ, Q_PER_KV, 2)
    return call(x, bound_s, *qkv_s, PAST_LEN, NP, DEC_BATCH, DEC_SEQ, 512, 1, 1)


def _ffn_kernel(*refs, final, tm, ada):
    side_work = ()
    if final:
        x_ref, mod_ref, nw_ref, win_ref, wout_ref, fw_ref, op_ref, os_ref, h_ref, acc_ref = refs
    elif ada:
        (x_ref, mod_ref, nw_ref, win_ref, wout_ref, c_ref, adaw_ref, adab_ref,
         o_ref, modo_ref, h_ref, acc_ref) = refs

        def later_layers_adaln():
            c = c_ref[...]
            sc = (c * jax.nn.sigmoid(c)).astype(BF16)
            modo_ref[...] = _wdot(sc, adaw_ref[...]) + adab_ref[...]

        side_work = [lambda: None] * (N_FF_CHUNKS // 2) + [later_layers_adaln]
    else:
        x_ref, mod_ref, nw_ref, win_ref, wout_ref, o_ref, h_ref, acc_ref = refs
    mod = mod_ref[...]
    x = x_ref[...]
    h_ref[...] = _norm_mod(x, nw_ref[...], mod[3:4], mod[4:5]).astype(BF16)
    _swiglu(h_ref[...], win_ref, wout_ref, acc_ref, side_work)
    y = x + mod[5:6] * acc_ref[...]
    if not final:
        o_ref[...] = y
        return
    y = y * lax.rsqrt(jnp.mean(y * y, axis=-1, keepdims=True) + EPS) * fw_ref[...]
    @pl.when(pl.program_id(0) < NP // tm)
    def _():
        op_ref[...] = y

    os_ref[...] = y


def _ffn_layer(x, mods_l, nw, w_in, w_out, layer, final_w=None, adaln_rest=None):
    tm = 512
    final = final_w is not None
    ada = adaln_rest is not None
    n_steps = NT // tm
    in_specs = [
        pl.BlockSpec((tm, D), lambda i: (i, 0)),
        pl.BlockSpec((None, N_MOD, D), lambda i: (_mod_group(i, tm), 0, 0)),
        _const_spec((1, D)),
        _layer_spec((D, 2 * D_FF), layer),
        _layer_spec((D_FF, D), layer),
    ]
    args = [x, mods_l, nw, w_in, w_out]
    if final:
        npt = NP // tm
        in_specs.append(_const_spec((1, D)))
        args.append(final_w)
        out_shape = [jax.ShapeDtypeStruct((NP, D), F32), jax.ShapeDtypeStruct((NS, D), F32)]
        out_specs = [pl.BlockSpec((tm, D), lambda i: (jnp.minimum(i, npt - 1), 0)),
                     pl.BlockSpec((tm, D), lambda i: (jnp.maximum(i - npt, 0), 0))]
    else:
        out_shape = jax.ShapeDtypeStruct((NT, D), F32)
        out_specs = pl.BlockSpec((tm, D), lambda i: (i, 0))
    if ada:
        rest = DEPTH - 1
        per_layer = n_steps // rest
        tn = N_MOD * D // per_layer
        assert per_layer * rest == n_steps and tn * per_layer == N_MOD * D and tn % LANES == 0
        in_specs += [
            _const_spec((8, D)),
            pl.BlockSpec((None, D, tn), lambda i: (1 + i // per_layer, 0, i % per_layer)),
            pl.BlockSpec((None, 1, tn), lambda i: (1 + i // per_layer, 0, i % per_layer)),
        ]
        args += list(adaln_rest)
        out_shape = [out_shape, jax.ShapeDtypeStruct((rest, 8, N_MOD * D), F32)]
        out_specs = [out_specs,
                     pl.BlockSpec((None, 8, tn), lambda i: (i // per_layer, 0, i % per_layer))]
    return pl.pallas_call(
        functools.partial(_ffn_kernel, final=final, tm=tm, ada=ada),
        out_shape=out_shape,
        grid=(n_steps,),
        in_specs=in_specs,
        out_specs=out_specs,
        scratch_shapes=[pltpu.VMEM((tm, D), BF16), pltpu.VMEM((tm, D), F32)],
        compiler_params=_params(56),
        name="ffn_final" if final else "ffn",
    )(*args)


def _rope_tables():
    t = np.arange(DEC_SEQ)
    inv_freq = ROPE_THETA ** (-np.arange(0, ROPE_AXIS_DIM, 2, dtype=np.float64) / ROPE_AXIS_DIM)
    ang_r = (t // GRID_W)[:, None] * inv_freq[None, :]
    ang_c = (t % GRID_W)[:, None] * inv_freq[None, :]
    cos = np.concatenate([np.cos(ang_r), np.cos(ang_c)] * 2, axis=-1)
    sin = np.concatenate([-np.sin(ang_r), -np.sin(ang_c), np.sin(ang_r), np.sin(ang_c)], axis=-1)
    return jnp.asarray(cos, F32), jnp.asarray(sin, F32)


def kernel(x_prompt, x_sample, cache_k, cache_v, c, c_ctx, norm1_w, norm2_w, ada_w, ada_b,
           conv_in_w, conv_w, conv_out_w, pool_w, pool_scale, attn_qkv_w, q_norm_w, k_norm_w,
           attn_out_w, ffn_in_w, ffn_out_w, final_norm_w):
    cvec = jnp.concatenate([c_ctx[None, :], c, jnp.zeros((8 - N_GROUPS, D), F32)], axis=0)
    ada_b3 = ada_b.reshape(DEPTH, 1, N_MOD * D)

    def mod_rows(m):
        return m[:, :N_GROUPS].reshape(m.shape[0], N_GROUPS, N_MOD, D)

    mods = [mod_rows(_adaln(cvec, ada_w, ada_b3, 1))[0]]

    pool_scale3 = pool_scale[:, None, :]
    qn3, kn3 = q_norm_w[:, None, :], k_norm_w[:, None, :]

    x = (x_prompt.reshape(NP, D), x_sample.reshape(NS, D))
    new_k = new_v = None
    for i in range(DEPTH):
        kind, j = i % 3, i // 3
        nw1 = norm1_w[i][None, :]
        if kind == 0:
            x = _conv_layer(x, mods[i], nw1, conv_in_w, conv_w, conv_out_w, j)
        elif kind == 1:
            x = _pool_ffn_layer(x, mods[i], nw1, pool_w, pool_scale3, j,
                                norm2_w[i][None, :], ffn_in_w, ffn_out_w, i)
            continue
        else:
            (qp, kp, vtp, new_k, new_v), qkv_s = _qkv_layer(
                x, mods[i], nw1, attn_qkv_w, qn3, kn3, _rope_tables(), j)
            x = _attn_layer(x, (qp, kp, vtp), qkv_s, cache_k, cache_v, mods[i], attn_out_w,
                            q_norm_w[j], k_norm_w[j], j)
        if i == 0:
            x, rest = _ffn_layer(x, mods[0], norm2_w[0][None, :], ffn_in_w, ffn_out_w, 0,
                                 adaln_rest=(cvec, ada_w, ada_b3))
            mods += list(mod_rows(rest))
            continue
        x = _ffn_layer(x, mods[i], norm2_w[i][None, :], ffn_in_w, ffn_out_w, i,
                       final_norm_w[None, :] if i == DEPTH - 1 else None)

    y_prompt, y_sample = x
    return (y_prompt.reshape(BATCH, SEQ, D), y_sample.reshape(DEC_BATCH, DEC_SEQ, D), new_k, new_v)
```

```python
import functools

import numpy as np
import jax
import jax.numpy as jnp
from jax import lax
from jax.experimental import pallas as pl
from jax.experimental.pallas import tpu as pltpu

D = 1024
BATCH = 16
SEQ = 256
DEPTH = 4
DEC_BATCH = 2
DEC_SEQ = 4096
PAST_LEN = 256
GRID_W = 64
N_HEADS = 8
N_KV_HEADS = 2
HEAD_DIM = 128
Q_PER_KV = N_HEADS // N_KV_HEADS
ROPE_AXIS_DIM = HEAD_DIM // 2
ROPE_THETA = 10000.0
POOL_WINDOWS = (2, 4, 8, 16)
POOL_GROUP_DIM = D // 4
D_FF = 2816
N_MOD = 6
EPS = 1e-6
ATTN_SCALE = HEAD_DIM ** -0.5

NP = BATCH * SEQ
NS = DEC_BATCH * DEC_SEQ
NT = NP + NS
N_GROUPS = 1 + DEC_BATCH

QK_SCALE_LOG2 = ATTN_SCALE * float(np.log2(np.e))
MAX_SAFE_SCORE_BOUND = 60.0
BF16_SUBLANES = 16
VEXT_ROWS = HEAD_DIM + BF16_SUBLANES
HALO = BF16_SUBLANES
FF_CHUNK = 256
N_FF_CHUNKS = D_FF // FF_CHUNK
CONV_CHUNK = 256
LANES = 128
BF16 = jnp.bfloat16
F32 = jnp.float32


def _dot(a, b):
    return jnp.dot(a, b, preferred_element_type=F32)


def _wdot(a, w):
    return _dot(a, w.astype(BF16))


def _dot_nt(a, b):
    return lax.dot_general(a, b, (((1,), (1,)), ((), ())), preferred_element_type=F32)


def _norm_mod(x, nw, shift, scale):
    y = x * lax.rsqrt(jnp.mean(x * x, axis=-1, keepdims=True) + EPS)
    return (y * nw) * (1.0 + scale) + shift


def _mod_group(tile, tm):
    return jnp.where(tile < NP // tm, 0, 1 + (tile - NP // tm) // (DEC_SEQ // tm))


def _params(vmem_mb, n_axes=1):
    return pltpu.CompilerParams(
        dimension_semantics=("arbitrary",) * n_axes,
        vmem_limit_bytes=vmem_mb * 1024 * 1024)


def _const_spec(shape):
    return pl.BlockSpec(shape, lambda *_: (0,) * len(shape), pipeline_mode=pl.Buffered(1))


def _layer_spec(shape, j):
    return pl.BlockSpec((None,) + tuple(shape), lambda *_: (j,) + (0,) * len(shape),
                        pipeline_mode=pl.Buffered(1))


def _adaln_kernel(c_ref, w_ref, b_ref, o_ref):
    c = c_ref[...]
    sc = (c * jax.nn.sigmoid(c)).astype(BF16)
    o_ref[...] = _dot(sc, w_ref[...].astype(BF16)) + b_ref[...]


def _adaln(cvec, ada_w, ada_b3, n_layers):
    tn = 1536
    return pl.pallas_call(
        _adaln_kernel,
        out_shape=jax.ShapeDtypeStruct((n_layers, 8, N_MOD * D), F32),
        grid=(n_layers, N_MOD * D // tn),
        in_specs=[
            pl.BlockSpec((8, D), lambda l, j: (0, 0)),
            pl.BlockSpec((None, D, tn), lambda l, j: (l, 0, j)),
            pl.BlockSpec((None, 1, tn), lambda l, j: (l, 0, j)),
        ],
        out_specs=pl.BlockSpec((None, 8, tn), lambda l, j: (l, 0, j)),
        compiler_params=_params(40, 2),
        name="adaln",
    )(cvec, ada_w, ada_b3)


def _conv_kernel(*refs, tm, split):
    i = pl.program_id(0)
    if split:
        (pm_ref, pp_ref, pn_ref, sm_ref, sp_ref, sn_ref,
         mod_ref, nw_ref, win_ref, cw_ref, wout_ref, o_ref, h_ref, acc_ref) = refs
        is_prompt = i < NP // tm
        x = jnp.where(is_prompt, pm_ref[...], sm_ref[...])
        x_prev = jnp.where(is_prompt, pp_ref[...], sp_ref[...])
        x_next = jnp.where(is_prompt, pn_ref[...], sn_ref[...])
    else:
        (x_ref, xp_ref, xn_ref,
         mod_ref, nw_ref, win_ref, cw_ref, wout_ref, o_ref, h_ref, acc_ref) = refs
        x, x_prev, x_next = x_ref[...], xp_ref[...], xn_ref[...]
    mod = mod_ref[...]
    nw = nw_ref[...]
    shift, scale, gate = mod[0:1], mod[1:2], mod[2:3]
    half = HALO // 2
    h_ref[0:tm] = _norm_mod(x, nw, shift, scale).astype(BF16)
    h_ref[tm:] = jnp.concatenate([_norm_mod(x_next, nw, shift, scale)[0:half],
                                  _norm_mod(x_prev, nw, shift, scale)[half:]], axis=0).astype(BF16)

    seq_len = jnp.where(i < NP // tm, SEQ, DEC_SEQ)
    pos = (i * tm + lax.broadcasted_iota(jnp.int32, (tm, 1), 0)) & (seq_len - 1)
    has_prev = pos != 0
    has_next = pos != seq_len - 1
    cw = cw_ref[...]
    rows = tm + HALO

    def in_proj(j):
        c0 = j * CONV_CHUNK
        return (_wdot(h_ref[0:tm], win_ref[:, c0:c0 + CONV_CHUNK]),
                _wdot(h_ref[...], win_ref[:, D + c0:D + c0 + CONV_CHUNK]),
                _wdot(h_ref[...], win_ref[:, 2 * D + c0:2 * D + c0 + CONV_CHUNK]))

    nxt = in_proj(0)
    for j in range(D // CONV_CHUNK):
        c0 = j * CONV_CHUNK
        b, cg, u = nxt
        if j + 1 < D // CONV_CHUNK:
            nxt = in_proj(j + 1)
        z = cg * u
        z_prev = pltpu.roll(z, 1, axis=0)[0:tm]
        z_next = pltpu.roll(z, rows - 1, axis=0)[0:tm]
        conv = (jnp.where(has_prev, z_prev, 0.0) * cw[0:1, c0:c0 + CONV_CHUNK]
                + z[0:tm] * cw[1:2, c0:c0 + CONV_CHUNK]
                + jnp.where(has_next, z_next, 0.0) * cw[2:3, c0:c0 + CONV_CHUNK])
        part = _wdot((b * conv).astype(BF16), wout_ref[c0:c0 + CONV_CHUNK, :])
        if j == 0:
            acc_ref[...] = part
        else:
            acc_ref[...] += part
    o_ref[...] = x + gate * acc_ref[...]


def _halo_specs(tm, nrows, tile_of):
    nh = tm // HALO
    last = nrows // HALO - 1
    return [
        pl.BlockSpec((tm, D), lambda i: (tile_of(i), 0)),
        pl.BlockSpec((HALO, D), lambda i: (jnp.maximum(tile_of(i) * nh - 1, 0), 0)),
        pl.BlockSpec((HALO, D), lambda i: (jnp.minimum((tile_of(i) + 1) * nh, last), 0)),
    ]


def _conv_layer(xs, mods_l, nw, w_in, cw, w_out, j):
    tm = 512
    split = isinstance(xs, tuple)
    if split:
        npt = NP // tm
        x_specs = (_halo_specs(tm, NP, lambda i: jnp.minimum(i, npt - 1))
                   + _halo_specs(tm, NS, lambda i: jnp.maximum(i - npt, 0)))
        x_args = [xs[0]] * 3 + [xs[1]] * 3
    else:
        x_specs = _halo_specs(tm, NT, lambda i: i)
        x_args = [xs] * 3
    return pl.pallas_call(
        functools.partial(_conv_kernel, tm=tm, split=split),
        out_shape=jax.ShapeDtypeStruct((NT, D), F32),
        grid=(NT // tm,),
        in_specs=x_specs + [
            pl.BlockSpec((None, N_MOD, D), lambda i: (_mod_group(i, tm), 0, 0)),
            _const_spec((1, D)),
            _layer_spec((D, 3 * D), j),
            _layer_spec((3, D), j),
            _layer_spec((D, D), j),
        ],
        out_specs=pl.BlockSpec((tm, D), lambda i: (i, 0)),
        scratch_shapes=[pltpu.VMEM((tm + HALO, D), BF16), pltpu.VMEM((tm, D), F32)],
        compiler_params=_params(48),
        name="conv_mixer",
    )(*x_args, mods_l, nw, w_in, cw, w_out)


def _swiglu(h, win_ref, wout_ref, acc_ref, side_work=()):
    def in_proj(j):
        c0 = j * FF_CHUNK
        return (_wdot(h, win_ref[:, c0:c0 + FF_CHUNK]),
                _wdot(h, win_ref[:, D_FF + c0:D_FF + c0 + FF_CHUNK]))

    n_chunks = N_FF_CHUNKS
    assert len(side_work) <= n_chunks
    nxt = in_proj(0)
    for j in range(n_chunks):
        c0 = j * FF_CHUNK
        g, u = nxt
        if j + 1 < n_chunks:
            nxt = in_proj(j + 1)
        if j < len(side_work):
            side_work[j]()
        a = (g * jax.nn.sigmoid(g) * u).astype(BF16)
        part = _wdot(a, wout_ref[c0:c0 + FF_CHUNK, :])
        if j == 0:
            acc_ref[...] = part
        else:
            acc_ref[...] += part


def _pool_ffn_kernel(x_ref, xp_ref, xn_ref, modn_ref, modc_ref, nw1_ref, wp_ref, ps_ref, nw2_ref,
                     win_ref, wout_ref, o_ref, hp_ref, st_ref, x1_ref, h2_ref, acc_ref,
                     *, tm, n_tiles):
    s = pl.program_id(0)
    cur = (s + 1) % 2
    new = s % 2

    @pl.when(s == 0)
    def _():
        x1_ref[1] = jnp.zeros((tm, D), F32)
        h2_ref[1] = jnp.zeros((tm, D), BF16)

    t = jnp.minimum(s, n_tiles - 1)
    mod = modn_ref[...]
    nw = nw1_ref[...]
    shift, scale, gate = mod[0:1], mod[1:2], mod[2:3]
    seq_len = jnp.where(t < NP // tm, SEQ, DEC_SEQ)
    pos0 = (t * tm) & (seq_len - 1)
    rows = tm + 2 * HALO

    def row_stats():
        def inv_rms(xv):
            r = lax.rsqrt(jnp.mean(xv * xv, axis=-1, keepdims=True) + EPS)
            return jnp.broadcast_to(r, (xv.shape[0], LANES))
        st_ref[0:HALO] = inv_rms(xp_ref[...])
        st_ref[HALO:HALO + tm] = inv_rms(x_ref[...])
        st_ref[HALO + tm:] = inv_rms(xn_ref[...])

    def fill_group(g):
        at_start = pos0 == 0
        at_end = pos0 + tm == seq_len
        for half in range(POOL_GROUP_DIM // LANES):
            cs = slice(g * POOL_GROUP_DIM + half * LANES, g * POOL_GROUP_DIM + (half + 1) * LANES)

            def nm(xv, iv):
                return ((xv * iv) * nw[:, cs]) * (1.0 + scale[:, cs]) + shift[:, cs]

            hp_ref[0:HALO, cs] = jnp.where(at_start, 0.0, nm(xp_ref[:, cs], st_ref[0:HALO]))
            hp_ref[HALO:HALO + tm, cs] = nm(x_ref[:, cs], st_ref[HALO:HALO + tm])
            hp_ref[HALO + tm:, cs] = jnp.where(at_end, 0.0, nm(xn_ref[:, cs], st_ref[HALO + tm:]))

    def pool_group(g):
        w = POOL_WINDOWS[g]
        c0 = g * POOL_GROUP_DIM
        pos = pos0 + lax.broadcasted_iota(jnp.int32, (tm, 1), 0)
        hg = hp_ref[:, c0:c0 + POOL_GROUP_DIM]
        p = hg + pltpu.roll(hg, 1, axis=0)
        step = 1
        while 2 * step < w:
            p = pltpu.roll(p, step, axis=0) + pltpu.roll(p, rows - step, axis=0)
            step *= 2
        cnt = (jnp.minimum(pos + w // 2, seq_len) - jnp.maximum(pos - w // 2, 0)).astype(F32)
        pooled = p[HALO:HALO + tm] / cnt
        diff = (pooled - hg[HALO:HALO + tm]).astype(BF16)
        mixed = _wdot(diff, wp_ref[g]) * ps_ref[...][:, c0:c0 + POOL_GROUP_DIM]
        x1_ref[new, :, c0:c0 + POOL_GROUP_DIM] = (
            x_ref[:, c0:c0 + POOL_GROUP_DIM] + gate[:, c0:c0 + POOL_GROUP_DIM] * mixed)

    def norm_for_swiglu():
        h2_ref[new] = _norm_mod(x1_ref[new], nw2_ref[...], mod[3:4], mod[4:5]).astype(BF16)

    side_work = [row_stats]
    for g in range(len(POOL_WINDOWS)):
        side_work += [functools.partial(fill_group, g), functools.partial(pool_group, g)]
    side_work.append(norm_for_swiglu)

    modc = modc_ref[...]
    _swiglu(h2_ref[cur], win_ref, wout_ref, acc_ref, side_work)
    o_ref[...] = x1_ref[cur] + modc[5:6] * acc_ref[...]


def _pool_ffn_layer(x, mods_l, nw1, wp, ps, j, nw2, w_in, w_out, layer):
    tm = 256
    n = NT // tm
    return pl.pallas_call(
        functools.partial(_pool_ffn_kernel, tm=tm, n_tiles=n),
        out_shape=jax.ShapeDtypeStruct((NT, D), F32),
        grid=(n + 1,),
        in_specs=_halo_specs(tm, NT, lambda s: jnp.minimum(s, n - 1)) + [
            pl.BlockSpec((None, N_MOD, D),
                         lambda s: (_mod_group(jnp.minimum(s, n - 1), tm), 0, 0)),
            pl.BlockSpec((None, N_MOD, D),
                         lambda s: (_mod_group(jnp.maximum(s - 1, 0), tm), 0, 0)),
            _const_spec((1, D)),
            _layer_spec((4, POOL_GROUP_DIM, POOL_GROUP_DIM), j),
            _layer_spec((1, D), j),
            _const_spec((1, D)),
            _layer_spec((D, 2 * D_FF), layer),
            _layer_spec((D_FF, D), layer),
        ],
        out_specs=pl.BlockSpec((tm, D), lambda s: (jnp.maximum(s - 1, 0), 0)),
        scratch_shapes=[pltpu.VMEM((tm + 2 * HALO, D), F32),
                        pltpu.VMEM((tm + 2 * HALO, LANES), F32),
                        pltpu.VMEM((2, tm, D), F32),
                        pltpu.VMEM((2, tm, D), BF16),
                        pltpu.VMEM((tm, D), F32)],
        compiler_params=_params(56),
        name="pool_ffn",
    )(x, x, x, mods_l, mods_l, nw1, wp, ps, nw2, w_in, w_out)


def _head_norm(t, w):
    return t * lax.rsqrt(jnp.mean(t * t, axis=-1, keepdims=True) + EPS) * w


def _pair_halves_layout(a):
    q4 = HEAD_DIM // 4
    lane = lax.broadcasted_iota(jnp.int32, (1, HEAD_DIM), 1)
    from_next = pltpu.roll(a, HEAD_DIM - q4, axis=1)
    from_prev = pltpu.roll(a, q4, axis=1)
    return jnp.where((lane >= q4) & (lane < 2 * q4), from_next,
                     jnp.where((lane >= 2 * q4) & (lane < 3 * q4), from_prev, a))


def _qkv_kernel(*refs, tm, rope):
    if rope:
        (x_ref, mod_ref, nw_ref, w_ref, qn_ref, kn_ref, cos_ref, sin_ref,
         q_ref, k_ref, vt_ref, wb_ref) = refs
    else:
        (x_ref, mod_ref, nw_ref, w_ref, qn_ref, kn_ref,
         q_ref, k_ref, vt_ref, nk_ref, nv_ref) = refs
    n_qk = N_HEADS + N_KV_HEADS
    if rope:
        @pl.when(pl.program_id(0) == 0)
        def _():
            for hd in range(n_qk + N_KV_HEADS):
                cols = slice(hd * HEAD_DIM, (hd + 1) * HEAD_DIM)
                w = w_ref[:, cols]
                wb_ref[:, cols] = (_pair_halves_layout(w) if hd < n_qk else w).astype(BF16)

    mod = mod_ref[...]
    h = _norm_mod(x_ref[...], nw_ref[...], mod[0:1], mod[1:2]).astype(BF16)
    qn = qn_ref[...]
    kn = kn_ref[...]
    if rope:
        qn = _pair_halves_layout(qn)
        kn = _pair_halves_layout(kn)
        cos = cos_ref[...]
        sin = sin_ref[...]

    pair = 2 * HEAD_DIM
    n_pairs = (N_HEADS + 2 * N_KV_HEADS) // 2
    same_head = (lax.broadcasted_iota(jnp.int32, (pair, pair), 0) // HEAD_DIM
                 == lax.broadcasted_iota(jnp.int32, (pair, pair), 1) // HEAD_DIM)
    same_head = jnp.where(same_head, 1.0, 0.0).astype(BF16)

    def head_inv_rms(tt):
        sq = tt * tt
        hi = sq.astype(BF16)
        lo = (sq - hi.astype(F32)).astype(BF16)
        ssq = _dot(hi, same_head) + _dot(lo, same_head)
        return lax.rsqrt(ssq * (1.0 / HEAD_DIM) + EPS)

    def normed(t, inv_rms, w):
        if not rope:
            return _head_norm(t, w)
        t = t * inv_rms * w
        return t * cos + pltpu.roll(t, HEAD_DIM // 2, axis=1) * sin

    def proj(c):
        if rope:
            return _dot(h, wb_ref[:, c * pair:(c + 1) * pair])
        return _wdot(h, w_ref[:, c * pair:(c + 1) * pair])

    nxt = proj(0)
    for c in range(n_pairs):
        cur = nxt
        if c + 1 < n_pairs:
            nxt = proj(c + 1)
        inv = head_inv_rms(cur) if rope and 2 * c < n_qk else None
        for r in range(2):
            t = cur[:, r * HEAD_DIM:(r + 1) * HEAD_DIM]
            inv_r = None if inv is None else inv[:, r * HEAD_DIM:(r + 1) * HEAD_DIM]
            hd = 2 * c + r
            if hd < N_HEADS:
                q_ref[:, hd * HEAD_DIM:(hd + 1) * HEAD_DIM] = (
                    normed(t, inv_r, qn) * QK_SCALE_LOG2).astype(BF16)
            elif hd < N_HEADS + N_KV_HEADS:
                g = hd - N_HEADS
                t = normed(t, inv_r, kn)
                if not rope:
                    for bb in range(tm // SEQ):
                        nk_ref[bb, 0, g] = t[bb * SEQ:(bb + 1) * SEQ]
                k_ref[:, g * HEAD_DIM:(g + 1) * HEAD_DIM] = t.astype(BF16)
            else:
                g = hd - N_HEADS - N_KV_HEADS
                if rope:
                    vt_ref[g] = t.T.astype(BF16)
                else:
                    for bb in range(tm // SEQ):
                        nv_ref[bb, 0, g] = t[bb * SEQ:(bb + 1) * SEQ]
                        vt_ref[bb, g] = t[bb * SEQ:(bb + 1) * SEQ].T.astype(BF16)


def _qkv_layer(x, mods_l, nw, w_qkv, qn, kn, rope_tabs, j):
    tm = 512
    kvd = N_KV_HEADS * HEAD_DIM
    qkv_out = (N_HEADS + 2 * N_KV_HEADS) * HEAD_DIM

    def call(rope, row0, nrows):
        t0 = row0 // tm
        in_specs = [
            pl.BlockSpec((tm, D), lambda i: (t0 + i, 0)),
            pl.BlockSpec((None, N_MOD, D), lambda i: (_mod_group(t0 + i, tm), 0, 0)),
            _const_spec((1, D)),
            _layer_spec((D, qkv_out), j),
            _layer_spec((1, HEAD_DIM), j),
            _layer_spec((1, HEAD_DIM), j),
        ]
        out_shape = [jax.ShapeDtypeStruct((nrows, D), BF16),
                     jax.ShapeDtypeStruct((nrows, kvd), BF16)]
        out_specs = [pl.BlockSpec((tm, D), lambda i: (i, 0)),
                     pl.BlockSpec((tm, kvd), lambda i: (i, 0))]
        args = [x, mods_l, nw, w_qkv, qn, kn]
        if rope:
            tps = DEC_SEQ // tm
            in_specs += [pl.BlockSpec((tm, HEAD_DIM), lambda i: (i % tps, 0)),
                         pl.BlockSpec((tm, HEAD_DIM), lambda i: (i % tps, 0))]
            args += list(rope_tabs)
            out_shape.append(jax.ShapeDtypeStruct((DEC_BATCH, N_KV_HEADS, HEAD_DIM, DEC_SEQ), BF16))
            out_specs.append(pl.BlockSpec((None, N_KV_HEADS, HEAD_DIM, tm),
                                          lambda i: (i // tps, 0, 0, i % tps)))
        else:
            nb = tm // SEQ
            out_shape.append(jax.ShapeDtypeStruct((BATCH, N_KV_HEADS, HEAD_DIM, SEQ), BF16))
            out_specs.append(pl.BlockSpec((nb, N_KV_HEADS, HEAD_DIM, SEQ), lambda i: (i, 0, 0, 0)))
            kv_shape = (BATCH, 1, N_KV_HEADS, SEQ, HEAD_DIM)
            out_shape += [jax.ShapeDtypeStruct(kv_shape, F32)] * 2
            out_specs += [pl.BlockSpec((nb, 1, N_KV_HEADS, SEQ, HEAD_DIM),
                                       lambda i: (i, 0, 0, 0, 0))] * 2
        return pl.pallas_call(
            functools.partial(_qkv_kernel, tm=tm, rope=rope),
            out_shape=out_shape,
            grid=(nrows // tm,),
            in_specs=in_specs,
            out_specs=out_specs,
            scratch_shapes=[pltpu.VMEM((D, qkv_out), BF16)] if rope else [],
            compiler_params=_params(40),
            name="qkv_rope" if rope else "qkv_ctx",
        )(*args)

    return call(False, 0, NP), call(True, NP, NS)


def _attn_kernel(*refs, past, hpb):
    if past:
        (x_ref, bound_ref, q_ref, k_ref, vt_ref, ck_ref, cv_ref, mod_ref, wo_ref,
         o_ref, kall_ref, vext_ref, heads_ref) = refs
    else:
        (x_ref, bound_ref, q_ref, k_ref, vt_ref, mod_ref, wo_ref,
         o_ref, kall_ref, vext_ref, heads_ref) = refs
    nkeys = kall_ref.shape[1]
    new_keys = nkeys - past
    nseq = vt_ref.shape[0]
    tq = q_ref.shape[0] // nseq

    @pl.when(pl.program_id(1) == 0)
    def _():
        for sq in range(nseq):
            for g in range(N_KV_HEADS):
                u = sq * N_KV_HEADS + g
                if past:
                    kall_ref[u, 0:past] = _pair_halves_layout(ck_ref[0, 0, g]).astype(BF16)
                    vext_ref[u, 0:HEAD_DIM, 0:past] = cv_ref[0, 0, g].T.astype(BF16)
                kall_ref[u, past:] = k_ref[sq * new_keys:(sq + 1) * new_keys,
                                           g * HEAD_DIM:(g + 1) * HEAD_DIM]
                vext_ref[u, 0:HEAD_DIM, past:] = vt_ref[sq, g]
                vext_ref[u, HEAD_DIM:] = jnp.ones((VEXT_ROWS - HEAD_DIM, nkeys), BF16)

    blocks = [(sq, list(range(h0, h0 + hpb)))
              for sq in range(nseq) for h0 in range(0, N_HEADS, hpb)]

    def scores(block):
        sq, heads = block
        qb = [q_ref[sq * tq:(sq + 1) * tq, hd * HEAD_DIM:(hd + 1) * HEAD_DIM] for hd in heads]
        qb = qb[0] if hpb == 1 else jnp.concatenate(qb, axis=0)
        return _dot_nt(kall_ref[sq * N_KV_HEADS + heads[0] // Q_PER_KV], qb)

    def attend(shift_of):
        st_next = scores(blocks[0])
        for b, (sq, heads) in enumerate(blocks):
            st = st_next
            if b + 1 < len(blocks):
                st_next = scores(blocks[b + 1])
            pt = jnp.exp2(st - shift_of(st)).astype(BF16)
            ot = _dot(vext_ref[sq * N_KV_HEADS + heads[0] // Q_PER_KV], pt)
            o = (ot[0:HEAD_DIM] / ot[HEAD_DIM:HEAD_DIM + 1]).T
            for r, hd in enumerate(heads):
                heads_ref[sq * tq:(sq + 1) * tq, hd * HEAD_DIM:(hd + 1) * HEAD_DIM] = (
                    o[r * tq:(r + 1) * tq].astype(BF16))
        gate = mod_ref[...][2:3]
        o_ref[...] = x_ref[...] + gate * _wdot(heads_ref[...], wo_ref[...])

    bound = bound_ref[0]
    use_bound = bound <= MAX_SAFE_SCORE_BOUND

    @pl.when(use_bound)
    def _():
        attend(lambda st: bound)

    @pl.when(jnp.logical_not(use_bound))
    def _():
        attend(lambda st: jnp.max(st, axis=0, keepdims=True))


def _attn_layer(x, qkv_p, qkv_s, cache_k, cache_v, mods_l, w_out, qn, kn, j):
    kvd = N_KV_HEADS * HEAD_DIM

    q_len = QK_SCALE_LOG2 * HEAD_DIM ** 0.5 * jnp.max(jnp.abs(qn))
    k_len = HEAD_DIM ** 0.5 * jnp.max(jnp.abs(kn))
    cache_len = jnp.sqrt(jnp.max(jnp.sum(jnp.square(cache_k[:, j]), axis=-1)))
    slack = 1.0 + 2.0 ** -6
    bound_p = (q_len * k_len * slack).reshape(1)
    bound_s = (q_len * jnp.maximum(k_len, cache_len) * slack).reshape(1)

    def call(x, bound, q, k, vt, past, row0, nb, seq, tq, hpb, nseq):
        rows = tq * nseq
        t0 = row0 // rows
        nq = seq // tq
        assert nseq == 1 or nq == 1
        in_specs = [
            pl.BlockSpec((rows, D), lambda b, i: (t0 + b * nq + i, 0)),
            pl.BlockSpec(memory_space=pltpu.SMEM),
            pl.BlockSpec((rows, D), lambda b, i: (b * nq + i, 0)),
            pl.BlockSpec((nseq * seq, kvd), lambda b, i: (b, 0)),
            pl.BlockSpec((nseq, N_KV_HEADS, HEAD_DIM, seq), lambda b, i: (b, 0, 0, 0)),
        ]
        args = [x, bound, q, k, vt]
        if past:
            cspec = pl.BlockSpec((1, 1, N_KV_HEADS, PAST_LEN, HEAD_DIM),
                                 lambda b, i: (b, j, 0, 0, 0))
            in_specs += [cspec, cspec]
            args += [cache_k, cache_v]
        in_specs += [
            pl.BlockSpec((None, N_MOD, D),
                         lambda b, i: (_mod_group(t0 + b * nq + i, rows), 0, 0)),
            _layer_spec((D, D), j),
        ]
        args += [mods_l, w_out]
        return pl.pallas_call(
            functools.partial(_attn_kernel, past=past, hpb=hpb),
            out_shape=jax.ShapeDtypeStruct((NT, D), F32),
            grid=(nb // nseq, nq),
            in_specs=in_specs,
            out_specs=pl.BlockSpec((rows, D), lambda b, i: (t0 + b * nq + i, 0)),
            scratch_shapes=[pltpu.VMEM((nseq * N_KV_HEADS, past + seq, HEAD_DIM), BF16),
                            pltpu.VMEM((nseq * N_KV_HEADS, VEXT_ROWS, past + seq), BF16),
                            pltpu.VMEM((rows, D), BF16)],
            input_output_aliases={0: 0},
            compiler_params=_params(48, 2),
            name="attn_cached" if past else "attn_ctx",
        )(*args)

    x = call(x, bound_p, *qkv_p, 0, 0, BATCH, SEQ, SEQ, Q_PER_KV, 2)
    return call(x, bound_s, *qkv_s, PAST_LEN, NP, DEC_BATCH, DEC_SEQ, 512, 1, 1)


def _ffn_kernel(*refs, final, tm, ada):
    side_work = ()
    if final:
        x_ref, mod_ref, nw_ref, win_ref, wout_ref, fw_ref, op_ref, os_ref, h_ref, acc_ref = refs
    elif ada:
        (x_ref, mod_ref, nw_ref, win_ref, wout_ref, c_ref, adaw_ref, adab_ref,
         o_ref, modo_ref, h_ref, acc_ref) = refs

        def later_layers_adaln():
            c = c_ref[...]
            sc = (c * jax.nn.sigmoid(c)).astype(BF16)
            modo_ref[...] = _wdot(sc, adaw_ref[...]) + adab_ref[...]

        side_work = [lambda: None] * (N_FF_CHUNKS // 2) + [later_layers_adaln]
    else:
        x_ref, mod_ref, nw_ref, win_ref, wout_ref, o_ref, h_ref, acc_ref = refs
    mod = mod_ref[...]
    x = x_ref[...]
    h_ref[...] = _norm_mod(x, nw_ref[...], mod[3:4], mod[4:5]).astype(BF16)
    _swiglu(h_ref[...], win_ref, wout_ref, acc_ref, side_work)
    y = x + mod[5:6] * acc_ref[...]
    if not final:
        o_ref[...] = y
        return
    y = y * lax.rsqrt(jnp.mean(y * y, axis=-1, keepdims=True) + EPS) * fw_ref[...]
    @pl.when(pl.program_id(0) < NP // tm)
    def _():
        op_ref[...] = y

    os_ref[...] = y


def _ffn_layer(x, mods_l, nw, w_in, w_out, layer, final_w=None, adaln_rest=None):
    tm = 512
    final = final_w is not None
    ada = adaln_rest is not None
    n_steps = NT // tm
    in_specs = [
        pl.BlockSpec((tm, D), lambda i: (i, 0)),
        pl.BlockSpec((None, N_MOD, D), lambda i: (_mod_group(i, tm), 0, 0)),
        _const_spec((1, D)),
        _layer_spec((D, 2 * D_FF), layer),
        _layer_spec((D_FF, D), layer),
    ]
    args = [x, mods_l, nw, w_in, w_out]
    if final:
        npt = NP // tm
        in_specs.append(_const_spec((1, D)))
        args.append(final_w)
        out_shape = [jax.ShapeDtypeStruct((NP, D), F32), jax.ShapeDtypeStruct((NS, D), F32)]
        out_specs = [pl.BlockSpec((tm, D), lambda i: (jnp.minimum(i, npt - 1), 0)),
                     pl.BlockSpec((tm, D), lambda i: (jnp.maximum(i - npt, 0), 0))]
    else:
        out_shape = jax.ShapeDtypeStruct((NT, D), F32)
        out_specs = pl.BlockSpec((tm, D), lambda i: (i, 0))
    if ada:
        rest = DEPTH - 1
        per_layer = n_steps // rest
        tn = N_MOD * D // per_layer
        assert per_layer * rest == n_steps and tn * per_layer == N_MOD * D and tn % LANES == 0
        in_specs += [
            _const_spec((8, D)),
            pl.BlockSpec((None, D, tn), lambda i: (1 + i // per_layer, 0, i % per_layer)),
            pl.BlockSpec((None, 1, tn), lambda i: (1 + i // per_layer, 0, i % per_layer)),
        ]
        args += list(adaln_rest)
        out_shape = [out_shape, jax.ShapeDtypeStruct((rest, 8, N_MOD * D), F32)]
        out_specs = [out_specs,
                     pl.BlockSpec((None, 8, tn), lambda i: (i // per_layer, 0, i % per_layer))]
    return pl.pallas_call(
        functools.partial(_ffn_kernel, final=final, tm=tm, ada=ada),
        out_shape=out_shape,
        grid=(n_steps,),
        in_specs=in_specs,
        out_specs=out_specs,
        scratch_shapes=[pltpu.VMEM((tm, D), BF16), pltpu.VMEM((tm, D), F32)],
        compiler_params=_params(56),
        name="ffn_final" if final else "ffn",
    )(*args)


def _ffn_streamed_kernel(x_ref, mod_ref, nw_ref, wg_ref, wu_ref, wo_ref, o_ref, h_ref, acc_ref):
    j = pl.program_id(1)
    mod = mod_ref[...]

    @pl.when(j == 0)
    def _():
        h_ref[...] = _norm_mod(x_ref[...], nw_ref[...], mod[3:4], mod[4:5]).astype(BF16)
        acc_ref[...] = jnp.zeros(acc_ref.shape, F32)

    h = h_ref[...]
    g = _wdot(h, wg_ref[...])
    u = _wdot(h, wu_ref[...])
    a = (g * jax.nn.sigmoid(g) * u).astype(BF16)
    acc_ref[...] += _wdot(a, wo_ref[...])
    o_ref[...] = x_ref[...] + mod[5:6] * acc_ref[...]


def _ffn_streamed_layer(x, mods_l, nw, w_in, w_out, layer):
    tm = 1024
    return pl.pallas_call(
        _ffn_streamed_kernel,
        out_shape=jax.ShapeDtypeStruct((NT, D), F32),
        grid=(NT // tm, N_FF_CHUNKS),
        in_specs=[
            pl.BlockSpec((tm, D), lambda i, j: (i, 0)),
            pl.BlockSpec((None, N_MOD, D), lambda i, j: (_mod_group(i, tm), 0, 0)),
            _const_spec((1, D)),
            pl.BlockSpec((None, D, FF_CHUNK), lambda i, j: (layer, 0, j)),
            pl.BlockSpec((None, D, FF_CHUNK), lambda i, j: (layer, 0, N_FF_CHUNKS + j)),
            pl.BlockSpec((None, FF_CHUNK, D), lambda i, j: (layer, j, 0)),
        ],
        out_specs=pl.BlockSpec((tm, D), lambda i, j: (i, 0)),
        scratch_shapes=[pltpu.VMEM((tm, D), BF16), pltpu.VMEM((tm, D), F32)],
        compiler_params=_params(40, 2),
        name="ffn_streamed",
    )(x, mods_l, nw, w_in, w_in, w_out)


def _rope_tables():
    t = np.arange(DEC_SEQ)
    inv_freq = ROPE_THETA ** (-np.arange(0, ROPE_AXIS_DIM, 2, dtype=np.float64) / ROPE_AXIS_DIM)
    ang_r = (t // GRID_W)[:, None] * inv_freq[None, :]
    ang_c = (t % GRID_W)[:, None] * inv_freq[None, :]
    cos = np.concatenate([np.cos(ang_r), np.cos(ang_c)] * 2, axis=-1)
    sin = np.concatenate([-np.sin(ang_r), -np.sin(ang_c), np.sin(ang_r), np.sin(ang_c)], axis=-1)
    return jnp.asarray(cos, F32), jnp.asarray(sin, F32)


def kernel(x_prompt, x_sample, cache_k, cache_v, c, c_ctx, norm1_w, norm2_w, ada_w, ada_b,
           conv_in_w, conv_w, conv_out_w, pool_w, pool_scale, attn_qkv_w, q_norm_w, k_norm_w,
           attn_out_w, ffn_in_w, ffn_out_w, final_norm_w):
    cvec = jnp.concatenate([c_ctx[None, :], c, jnp.zeros((8 - N_GROUPS, D), F32)], axis=0)
    ada_b3 = ada_b.reshape(DEPTH, 1, N_MOD * D)

    def mod_rows(m):
        return m[:, :N_GROUPS].reshape(m.shape[0], N_GROUPS, N_MOD, D)

    mods = [mod_rows(_adaln(cvec, ada_w, ada_b3, 1))[0]]

    pool_scale3 = pool_scale[:, None, :]
    qn3, kn3 = q_norm_w[:, None, :], k_norm_w[:, None, :]

    x = (x_prompt.reshape(NP, D), x_sample.reshape(NS, D))
    new_k = new_v = None
    for i in range(DEPTH):
        kind, j = i % 3, i // 3
        nw1 = norm1_w[i][None, :]
        if kind == 0:
            x = _conv_layer(x, mods[i], nw1, conv_in_w, conv_w, conv_out_w, j)
        elif kind == 1:
            x = _pool_ffn_layer(x, mods[i], nw1, pool_w, pool_scale3, j,
                                norm2_w[i][None, :], ffn_in_w, ffn_out_w, i)
            continue
        else:
            (qp, kp, vtp, new_k, new_v), qkv_s = _qkv_layer(
                x, mods[i], nw1, attn_qkv_w, qn3, kn3, _rope_tables(), j)
            x = _attn_layer(x, (qp, kp, vtp), qkv_s, cache_k, cache_v, mods[i], attn_out_w,
                            q_norm_w[j], k_norm_w[j], j)
        if i == 0:
            x, rest = _ffn_layer(x, mods[0], norm2_w[0][None, :], ffn_in_w, ffn_out_w, 0,
                                 adaln_rest=(cvec, ada_w, ada_b3))
            mods += list(mod_rows(rest))
            continue
        if i < DEPTH - 1:
            x = _ffn_streamed_layer(x, mods[i], norm2_w[i][None, :], ffn_in_w, ffn_out_w, i)
            continue
        x = _ffn_layer(x, mods[i], norm2_w[i][None, :], ffn_in_w, ffn_out_w, i, final_norm_w[None, :])

    y_prompt, y_sample = x
    return (y_prompt.reshape(BATCH, SEQ, D), y_sample.reshape(DEC_BATCH, DEC_SEQ, D), new_k, new_v)
```
